```python
import math
import jax, jax.numpy as jnp
from jax import lax
import numpy as np

D_MODEL = 1024
BATCH = 8
SEQ = 4096
DEPTH = 1

MOBA_HEADS = 8
MOBA_HEAD_DIM = 64
MOBA_WIDTH = MOBA_HEADS * MOBA_HEAD_DIM
MOBA_BLOCK = 256
MOBA_TOPK = 3
MOBA_Q_CHUNK = 16
ROPE_THETA = 10000.0

GLA_HEADS = 4
GLA_DK = D_MODEL // 4
GLA_DV = D_MODEL // 2
GLA_HEAD_K = GLA_DK // GLA_HEADS
GLA_HEAD_V = GLA_DV // GLA_HEADS
GLA_GATE_RANK = 16
GLA_GATE_NORM = 16.0
GLA_CHUNK = 64

N_BRANCH = 2
RMS_EPS = 1e-6
NEG_INF = -1e30

IN_SPLITS = (
    MOBA_WIDTH, MOBA_WIDTH, MOBA_WIDTH, MOBA_WIDTH,
    GLA_DK, GLA_DK, GLA_DV, GLA_DV,
    GLA_GATE_RANK,
    D_MODEL, D_MODEL,
)
N_IN = sum(IN_SPLITS)

kernel_name = "hybrid_moba_gla_gated_merge"


def rms_norm(x, g):
    xf = x.astype(jnp.float32)
    y = xf * lax.rsqrt(jnp.mean(xf * xf, axis=-1, keepdims=True) + RMS_EPS)
    return (y * g.astype(jnp.float32)).astype(x.dtype)


def rotary(t):
    S, dh = t.shape[2], t.shape[3]
    half = dh // 2
    inv_freq = 1.0 / (ROPE_THETA ** (jnp.arange(half, dtype=jnp.float32) / half))
    ang = jnp.arange(S, dtype=jnp.float32)[:, None] * inv_freq[None, :]
    cos, sin = jnp.cos(ang), jnp.sin(ang)
    tf = t.astype(jnp.float32)
    t1, t2 = tf[..., :half], tf[..., half:]
    return jnp.concatenate([t1 * cos - t2 * sin, t1 * sin + t2 * cos], axis=-1).astype(t.dtype)


def moba_attention(q, k, v):
    B, H, S, Dh = q.shape
    S_pad = -(-S // MOBA_BLOCK) * MOBA_BLOCK
    pad = ((0, 0), (0, 0), (0, S_pad - S), (0, 0))
    q, k, v = jnp.pad(q, pad), jnp.pad(k, pad), jnp.pad(v, pad)
    nb = S_pad // MOBA_BLOCK
    k_eff = min(MOBA_TOPK, nb)
    k_blk = k.reshape(B, H, nb, MOBA_BLOCK, Dh)
    v_blk = v.reshape(B, H, nb, MOBA_BLOCK, Dh)
    k_mean = jnp.mean(k_blk.astype(jnp.float32), axis=3)
    n_q = S_pad // MOBA_Q_CHUNK
    q_ch = q.reshape(B, H, n_q, MOBA_Q_CHUNK, Dh).transpose(2, 0, 1, 3, 4)
    scale = Dh ** -0.5
    bi = jnp.arange(B)[:, None, None, None]
    hi = jnp.arange(H)[None, :, None, None]
    blk_ids = jnp.arange(nb)

    def chunk_fn(args):
        qc, ci = args
        pos_q = ci * MOBA_Q_CHUNK + jnp.arange(MOBA_Q_CHUNK)
        own_blk = (ci * MOBA_Q_CHUNK) // MOBA_BLOCK
        gate = jnp.einsum('bhqd,bhnd->bhqn', qc.astype(jnp.float32), k_mean)
        gate = jnp.where(blk_ids < own_blk, gate, NEG_INF)
        _, top_idx = lax.top_k(gate, k_eff)
        own_idx = jnp.broadcast_to(own_blk, top_idx.shape[:-1] + (1,)).astype(top_idx.dtype)
        sel = jnp.concatenate([top_idx, own_idx], axis=-1)
        k_sel = k_blk[bi, hi, sel]
        v_sel = v_blk[bi, hi, sel]
        past_mask = jnp.broadcast_to((jnp.arange(k_eff) < own_blk)[None, :, None],
                                     (MOBA_Q_CHUNK, k_eff, MOBA_BLOCK))
        own_pos = own_blk * MOBA_BLOCK + jnp.arange(MOBA_BLOCK)
        own_mask = (own_pos[None, :] <= pos_q[:, None])[:, None, :]
        mask = jnp.concatenate([past_mask, own_mask], axis=1)
        logits = jnp.einsum('bhqd,bhqnld->bhqnl', qc, k_sel).astype(jnp.float32) * scale
        logits = jnp.where(mask, logits, NEG_INF)
        p = jax.nn.softmax(logits.reshape(B, H, MOBA_Q_CHUNK, -1), axis=-1)
        p = p.reshape(logits.shape).astype(v.dtype)
        return jnp.einsum('bhqnl,bhqnld->bhqd', p, v_sel)

    out = lax.map(chunk_fn, (q_ch, jnp.arange(n_q, dtype=jnp.int32)))
    out = out.transpose(1, 2, 0, 3, 4).reshape(B, H, S_pad, Dh)
    return out[:, :, :S]


def gla_chunked(q, k, v, g):
    B, H, S, Dk = q.shape
    Dv = v.shape[-1]
    nc = S // GLA_CHUNK

    def to_chunks(t):
        return t.astype(jnp.float32).reshape(B, H, nc, GLA_CHUNK, t.shape[-1]).transpose(2, 0, 1, 3, 4)

    causal = jnp.tril(jnp.ones((GLA_CHUNK, GLA_CHUNK), dtype=bool))

    def step(state, inp):
        qc, kc, vc, gc = inp
        b = jnp.cumsum(gc, axis=2)
        o_inter = jnp.einsum('bhtd,bhde->bhte', qc * jnp.exp(b), state)
        diff = b[:, :, :, None, :] - b[:, :, None, :, :]
        decay = jnp.exp(jnp.where(causal[:, :, None], diff, -jnp.inf))
        attn = jnp.einsum('bhtd,bhsd,bhtsd->bhts', qc, kc, decay)
        o = o_inter + jnp.einsum('bhts,bhse->bhte', attn, vc)
        b_last = b[:, :, -1:, :]
        state = (jnp.exp(b_last[:, :, 0, :])[..., None] * state
                 + jnp.einsum('bhsd,bhse->bhde', kc * jnp.exp(b_last - b), vc))
        return state, o

    state0 = jnp.zeros((B, H, Dk, Dv), jnp.float32)
    _, o = lax.scan(step, state0, (to_chunks(q), to_chunks(k), to_chunks(v), to_chunks(g)))
    return o.transpose(1, 2, 0, 3, 4).reshape(B, H, S, Dv)


def setup_inputs(seed: int = 0) -> dict:
    key = jax.random.key(seed)
    ks = jax.random.split(key, 11)
    nrm = jax.random.normal
    return {
        "x": nrm(ks[0], (BATCH, SEQ, D_MODEL), jnp.float32),
        "norm_in_g": 1.0 + 0.01 * nrm(ks[1], (DEPTH, D_MODEL), jnp.float32),
        "w_in": nrm(ks[2], (DEPTH, D_MODEL, N_IN), jnp.float32) * D_MODEL ** -0.5,
        "b_merge": 0.01 * nrm(ks[3], (DEPTH, N_BRANCH, D_MODEL), jnp.float32),
        "w_gla_fg2": nrm(ks[4], (DEPTH, GLA_GATE_RANK, GLA_DK), jnp.float32) * GLA_GATE_RANK ** -0.5,
        "b_gla_fg": 0.1 * nrm(ks[5], (DEPTH, GLA_DK), jnp.float32),
        "gla_norm_g": 1.0 + 0.01 * nrm(ks[6], (DEPTH, GLA_HEAD_V), jnp.float32),
        "w_proj_a": nrm(ks[7], (DEPTH, MOBA_WIDTH, D_MODEL), jnp.float32) * MOBA_WIDTH ** -0.5,
        "w_proj_b": nrm(ks[8], (DEPTH, GLA_DV, D_MODEL), jnp.float32) * GLA_DV ** -0.5,
        "w_out": nrm(ks[9], (DEPTH, D_MODEL, D_MODEL), jnp.float32) * D_MODEL ** -0.5,
        "norm_f_g": 1.0 + 0.01 * nrm(ks[10], (D_MODEL,), jnp.float32),
    }


def reference(x, norm_in_g, w_in, b_merge, w_gla_fg2, b_gla_fg, gla_norm_g,
              w_proj_a, w_proj_b, w_out, norm_f_g):
    B, S, _ = x.shape
    split_pts = [int(p) for p in np.cumsum(IN_SPLITS)[:-1]]

    def heads(t, n):
        return t.reshape(B, S, n, -1).transpose(0, 2, 1, 3)

    for layer in range(DEPTH):
        h = rms_norm(x, norm_in_g[layer])
        proj = jnp.einsum('bsd,de->bse', h, w_in[layer])
        (mq, mk, mv, mgate, gq, gk, gv, ggate, gfg, ga, gb) = jnp.split(proj, split_pts, axis=-1)

        qa = rotary(heads(mq, MOBA_HEADS))
        ka = rotary(heads(mk, MOBA_HEADS))
        va = heads(mv, MOBA_HEADS)
        oa = moba_attention(qa, ka, va)
        oa = oa.transpose(0, 2, 1, 3).reshape(B, S, MOBA_WIDTH) * jax.nn.silu(mgate)
        ya = jnp.einsum('bse,ed->bsd', oa, w_proj_a[layer])

        fg_logit = jnp.einsum('bsr,rk->bsk', gfg, w_gla_fg2[layer]) + b_gla_fg[layer]
        log_alpha = jax.nn.log_sigmoid(fg_logit.astype(jnp.float32)) / GLA_GATE_NORM
        qb = heads(gq, GLA_HEADS) * (GLA_HEAD_K ** -0.5)
        kb = heads(gk, GLA_HEADS)
        vb = heads(gv, GLA_HEADS)
        gdec = heads(log_alpha, GLA_HEADS)
        ob = gla_chunked(qb, kb, vb, gdec)
        ob = rms_norm(ob, gla_norm_g[layer]).astype(x.dtype)
        ob = ob.transpose(0, 2, 1, 3).reshape(B, S, GLA_DV) * jax.nn.silu(ggate)
        yb = jnp.einsum('bse,ed->bsd', ob, w_proj_b[layer])

        merged = (jax.nn.sigmoid(ga + b_merge[layer, 0]) * ya
                  + jax.nn.sigmoid(gb + b_merge[layer, 1]) * yb)
        x = x + jnp.einsum('bsd,de->bse', merged, w_out[layer])

    return rms_norm(x, norm_f_g)
```

```python
import functools
import math

import jax
import jax.numpy as jnp
import numpy as np
from jax import lax
from jax.experimental import pallas as pl
from jax.experimental.pallas import tpu as pltpu

F32 = jnp.float32
BF16 = jnp.bfloat16

D_MODEL = 1024
MOBA_HEADS = 8
MOBA_HEAD_DIM = 64
MOBA_WIDTH = MOBA_HEADS * MOBA_HEAD_DIM
MOBA_BLOCK = 256
MOBA_TOPK = 3
ROPE_THETA = 10000.0
GLA_HEADS = 4
GLA_DK = 256
GLA_DV = 512
GLA_HEAD_K = 64
GLA_HEAD_V = 128
GLA_GATE_RANK = 16
GLA_GATE_NORM = 16.0
GLA_CHUNK = 64
RMS_EPS = 1e-6
NEG_INF = -1e30

LANES = 128
ROW_TILE = MOBA_BLOCK
PAIR = 2 * MOBA_HEAD_DIM
N_PAIRS = MOBA_HEADS // 2
GLA_STEP_ROWS = 512
FG_PAD = LANES
VMEM_LIMIT = 56 * 1024 * 1024

_C_K = 0
_C_GQ = _C_K + MOBA_WIDTH
_C_GK = _C_GQ + GLA_DK
_C_GV = _C_GK + GLA_DK
_C_GG = _C_GV + GLA_DV
_C_FG = _C_GG + GLA_DV
_C_GA = _C_FG + FG_PAD
_C_GB = _C_GA + D_MODEL
_C_END = _C_GB + D_MODEL


def _dot(a, b):
    return jnp.dot(a, b, preferred_element_type=F32)


def _dot_nt(a, b):
    return lax.dot_general(a, b, (((1,), (1,)), ((), ())), preferred_element_type=F32)


def _dot_tn(a, b):
    return lax.dot_general(a, b, (((0,), (0,)), ((), ())), preferred_element_type=F32)


def _sigmoid(x):
    return 1.0 / (1.0 + jnp.exp(-x))


def _silu(x):
    return x * _sigmoid(x)


def _in_proj_kernel(x_ref, g_ref, wn_ref, wt_ref, cq_ref, sq_ref, ck_ref, sk_ref,
                    wfg2_ref, bfg_ref, bm_ref,
                    qt_ref, vt_ref, smgt_ref, k_ref, kmean_ref, gq_ref, gk_ref, gv_ref,
                    sgg_ref, la_ref, sga_ref, sgb_ref):
    x = x_ref[0]
    ms = jnp.mean(x * x, axis=-1, keepdims=True)
    hb = (x * lax.rsqrt(ms + RMS_EPS) * g_ref[...]).astype(BF16)

    pt = _dot_nt(wt_ref[...], hb)
    cq = cq_ref[...]
    sq = sq_ref[...]
    half = PAIR // 2
    for p in range(N_PAIRS):
        blk = pt[p * PAIR:(p + 1) * PAIR]
        swapped = jnp.concatenate([blk[half:], blk[:half]], axis=0)
        qt_ref[0, 0, p * PAIR:(p + 1) * PAIR, :] = (blk * cq + swapped * sq).astype(BF16)
    vt_ref[0, 0] = pt[MOBA_WIDTH:2 * MOBA_WIDTH].astype(BF16)
    smgt_ref[0, 0] = _silu(pt[2 * MOBA_WIDTH:3 * MOBA_WIDTH]).astype(BF16)

    pn = _dot(hb, wn_ref[...])
    ck = ck_ref[...]
    sk = sk_ref[...]
    for p in range(N_PAIRS):
        blk = pn[:, _C_K + p * PAIR:_C_K + (p + 1) * PAIR]
        kr = blk * ck + pltpu.roll(blk, half, axis=1) * sk
        k_ref[0, :, p * PAIR:(p + 1) * PAIR] = kr.astype(BF16)
        kmean_ref[0, 0, :, p * PAIR:(p + 1) * PAIR] = jnp.mean(kr, axis=0, keepdims=True)

    gq_ref[0] = (pn[:, _C_GQ:_C_GK] * (GLA_HEAD_K ** -0.5)).astype(BF16)
    gk_ref[0] = pn[:, _C_GK:_C_GV].astype(BF16)
    gv_ref[0] = pn[:, _C_GV:_C_GG].astype(BF16)
    sgg_ref[0] = _silu(pn[:, _C_GG:_C_FG]).astype(BF16)

    fg = pn[:, _C_FG:_C_GA].astype(BF16)
    z = _dot(fg, wfg2_ref[...]) + bfg_ref[...]
    log_sig = jnp.minimum(z, 0.0) - jnp.log1p(jnp.exp(-jnp.abs(z)))
    la_ref[0] = log_sig * (1.0 / GLA_GATE_NORM)

    sga_ref[0] = _sigmoid(pn[:, _C_GA:_C_GB] + bm_ref[0:1, :]).astype(BF16)
    sgb_ref[0] = _sigmoid(pn[:, _C_GB:_C_END] + bm_ref[1:2, :]).astype(BF16)


def _in_proj(x, g, wn, wt, cq, sq, ck, sk, wfg2, bfg, bm):
    B, S, D = x.shape
    nt = S // ROW_TILE
    rt = ROW_TILE
    const = lambda shape: pl.BlockSpec(shape, lambda b, t: (0,) * len(shape))
    t_blocked = pl.BlockSpec((1, 1, MOBA_WIDTH, rt), lambda b, t: (b, t, 0, 0))
    rows = lambda w: pl.BlockSpec((1, rt, w), lambda b, t: (b, t, 0))
    t_shape = jax.ShapeDtypeStruct((B, nt, MOBA_WIDTH, rt), BF16)
    rshape = lambda w, dt=BF16: jax.ShapeDtypeStruct((B, S, w), dt)
    return pl.pallas_call(
        _in_proj_kernel,
        grid=(B, nt),
        in_specs=[
            rows(D), const((1, D)), const(wn.shape), const(wt.shape),
            pl.BlockSpec((PAIR, rt), lambda b, t: (0, t)),
            pl.BlockSpec((PAIR, rt), lambda b, t: (0, t)),
            pl.BlockSpec((rt, PAIR), lambda b, t: (t, 0)),
            pl.BlockSpec((rt, PAIR), lambda b, t: (t, 0)),
            const(wfg2.shape), const((1, GLA_DK)), const((2, D)),
        ],
        out_specs=[
            t_blocked, t_blocked, t_blocked, rows(MOBA_WIDTH),
            pl.BlockSpec((1, 1, 1, MOBA_WIDTH), lambda b, t: (b, t, 0, 0)),
            rows(GLA_DK), rows(GLA_DK), rows(GLA_DV), rows(GLA_DV), rows(GLA_DK),
            rows(D), rows(D),
        ],
        out_shape=[
            t_shape, t_shape, t_shape, rshape(MOBA_WIDTH),
            jax.ShapeDtypeStruct((B, nt, 1, MOBA_WIDTH), F32),
            rshape(GLA_DK), rshape(GLA_DK), rshape(GLA_DV), rshape(GLA_DV),
            rshape(GLA_DK, F32), rshape(D), rshape(D),
        ],
        compiler_params=pltpu.CompilerParams(
            dimension_semantics=("arbitrary", "arbitrary"),
            vmem_limit_bytes=VMEM_LIMIT),
        name="in_proj",
    )(x, g, wn, wt, cq, sq, ck, sk, wfg2, bfg, bm)


def _moba_kernel(qt_ref, k_ref, vt_ref, smgt_ref, kmh_ref, kml_ref, o_ref, sel_ref):
    qi = pl.program_id(2)
    nb = MOBA_BLOCK
    hd = MOBA_HEAD_DIM
    n_blocks = k_ref.shape[1] // nb
    qt = qt_ref[0, 0]

    gates = _dot(kmh_ref[0, 0], qt) + _dot(kml_ref[0, 0], qt)
    blk_id = lax.broadcasted_iota(jnp.int32, (n_blocks, nb), 0)
    for hh in range(2):
        gh = gates[hh * n_blocks:(hh + 1) * n_blocks]
        cnt = jnp.zeros((n_blocks, nb), jnp.int32)
        for m in range(n_blocks):
            gm = gh[m:m + 1]
            beats = (gm > gh) | ((gm == gh) & (m < blk_id))
            cnt = cnt + jnp.where(beats, (m < qi).astype(jnp.int32), 0)
        sel = (blk_id < qi) & (cnt < MOBA_TOPK)
        sel_ref[hh * n_blocks:(hh + 1) * n_blocks, :] = jnp.where(sel, 1.0, 0.0)

    row = lax.broadcasted_iota(jnp.int32, (PAIR, nb), 0)
    in_head0 = (row % hd) < (hd // 2)
    qm = [jnp.where(in_head0, qt, jnp.zeros_like(qt)),
          jnp.where(in_head0, jnp.zeros_like(qt), qt)]

    key_pos = lax.broadcasted_iota(jnp.int32, (nb, nb), 0)
    qry_pos = lax.broadcasted_iota(jnp.int32, (nb, nb), 1)
    causal = key_pos <= qry_pos

    kb = k_ref[0, pl.ds(pl.multiple_of(qi * nb, nb), nb), :]
    carry = []
    for hh in range(2):
        s = jnp.where(causal, _dot(kb, qm[hh]), NEG_INF)
        m = jnp.max(s, axis=0, keepdims=True)
        p = jnp.exp2(s - m)
        l = jnp.sum(p, axis=0, keepdims=True)
        vt = vt_ref[0, qi, hh * hd:(hh + 1) * hd, :]
        acc = _dot(vt, p.astype(BF16))
        carry += [m, l, acc]

    def body(n, carry):
        kb = k_ref[0, pl.ds(pl.multiple_of(n * nb, nb), nb), :]
        out = []
        for hh in range(2):
            m, l, acc = carry[3 * hh:3 * hh + 3]
            keep = sel_ref[pl.ds(hh * n_blocks + n, 1), :] > 0.5
            s = jnp.where(keep, _dot(kb, qm[hh]), NEG_INF)
            m_new = jnp.maximum(m, jnp.max(s, axis=0, keepdims=True))
            alpha = jnp.exp2(m - m_new)
            p = jnp.exp2(s - m_new)
            l = alpha * l + jnp.sum(p, axis=0, keepdims=True)
            vt = vt_ref[0, n, hh * hd:(hh + 1) * hd, :]
            acc = alpha * acc + _dot(vt, p.astype(BF16))
            out += [m_new, l, acc]
        return out

    carry = lax.fori_loop(0, qi, body, carry)
    for hh in range(2):
        _, l, acc = carry[3 * hh:3 * hh + 3]
        gate = smgt_ref[0, 0, hh * hd:(hh + 1) * hd, :].astype(F32)
        o_ref[0, 0, hh * hd:(hh + 1) * hd, :] = (acc / l * gate).astype(BF16)


def _moba(qt, k, vt, smgt, kmh, kml):
    B, nt, W, rt = qt.shape
    S = k.shape[1]
    tile = pl.BlockSpec((1, 1, PAIR, rt), lambda b, p, t: (b, t, p, 0))
    km = pl.BlockSpec((1, 1, 2 * nt, PAIR), lambda b, p, t: (b, p, 0, 0))
    return pl.pallas_call(
        _moba_kernel,
        grid=(B, N_PAIRS, nt),
        in_specs=[
            tile,
            pl.BlockSpec((1, S, PAIR), lambda b, p, t: (b, 0, p)),
            pl.BlockSpec((1, nt, PAIR, rt), lambda b, p, t: (b, 0, p, 0)),
            tile, km, km,
        ],
        out_specs=tile,
        out_shape=jax.ShapeDtypeStruct((B, nt, W, rt), BF16),
        scratch_shapes=[pltpu.VMEM((2 * nt, rt), F32)],
        compiler_params=pltpu.CompilerParams(
            dimension_semantics=("arbitrary", "arbitrary", "arbitrary"),
            vmem_limit_bytes=VMEM_LIMIT),
        name="moba",
    )(qt, k, vt, smgt, kmh, kml)


def _gla_kernel(q_ref, k_ref, v_ref, la_ref, sgg_ref, gn_ref, o_ref, st_ref):
    C = GLA_CHUNK
    hk, hv = GLA_HEAD_K, GLA_HEAD_V

    @pl.when(pl.program_id(1) == 0)
    def _():
        st_ref[...] = jnp.zeros_like(st_ref)

    r_i = lax.broadcasted_iota(jnp.int32, (C, C), 0)
    c_i = lax.broadcasted_iota(jnp.int32, (C, C), 1)
    tril = r_i >= c_i
    tril_f = jnp.where(tril, 1.0, 0.0).astype(F32)
    sr = lax.broadcasted_iota(jnp.int32, (GLA_DV, GLA_DK), 0) // hv
    sc = lax.broadcasted_iota(jnp.int32, (GLA_DV, GLA_DK), 1) // hk
    same_head = sr == sc
    lane_head = lax.broadcasted_iota(jnp.int32, (C, GLA_DK), 1) // hk
    gn = gn_ref[...]

    for c in range(GLA_STEP_ROWS // C):
        rows = slice(c * C, (c + 1) * C)
        q = q_ref[0, rows, :].astype(F32)
        k = k_ref[0, rows, :].astype(F32)
        v = v_ref[0, rows, :]
        g = la_ref[0, rows, :]
        b = jnp.dot(tril_f, g, preferred_element_type=F32,
                    precision=lax.Precision.HIGHEST)
        b_mid = b[C // 2 - 1:C // 2]
        b_last = b[C - 1:C]
        qe = q * jnp.exp(b - b_mid)
        ke = k * jnp.exp(b_mid - b)
        qb = (qe * jnp.exp(b_mid)).astype(BF16)
        kl = (ke * jnp.exp(b_last - b_mid)).astype(BF16)
        ke = ke.astype(BF16)

        st = st_ref[...]
        o = _dot_nt(qb, st.astype(BF16))
        intra = []
        for h in range(GLA_HEADS):
            qh = jnp.where(lane_head == h, qe, 0.0).astype(BF16)
            attn = jnp.where(tril, _dot_nt(qh, ke), 0.0).astype(BF16)
            intra.append(_dot(attn, v[:, h * hv:(h + 1) * hv]))
        o = o + jnp.concatenate(intra, axis=1)

        for h in range(GLA_HEADS):
            oh = o[:, h * hv:(h + 1) * hv]
            ms = jnp.mean(oh * oh, axis=-1, keepdims=True)
            y = oh * lax.rsqrt(ms + RMS_EPS) * gn
            gate = sgg_ref[0, rows, h * hv:(h + 1) * hv].astype(F32)
            o_ref[0, rows, h * hv:(h + 1) * hv] = (y * gate).astype(BF16)

        upd = _dot_tn(v, kl)
        st_ref[...] = st * jnp.exp(b_last) + jnp.where(same_head, upd, 0.0)


def _gla(gq, gk, gv, la, sgg, gn):
    B, S, _ = gq.shape
    rows = lambda w: pl.BlockSpec((1, GLA_STEP_ROWS, w), lambda b, t: (b, t, 0))
    return pl.pallas_call(
        _gla_kernel,
        grid=(B, S // GLA_STEP_ROWS),
        in_specs=[rows(GLA_DK), rows(GLA_DK), rows(GLA_DV), rows(GLA_DK), rows(GLA_DV),
                  pl.BlockSpec((1, GLA_HEAD_V), lambda b, t: (0, 0))],
        out_specs=rows(GLA_DV),
        out_shape=jax.ShapeDtypeStruct((B, S, GLA_DV), BF16),
        scratch_shapes=[pltpu.VMEM((GLA_DV, GLA_DK), F32)],
        compiler_params=pltpu.CompilerParams(
            dimension_semantics=("arbitrary", "arbitrary"),
            vmem_limit_bytes=VMEM_LIMIT),
        name="gla",
    )(gq, gk, gv, la, sgg, gn)


def _out_kernel(oat_ref, ob_ref, sga_ref, sgb_ref, x_ref, wpa_ref, wpb_ref, wo_ref, gf_ref,
                y_ref, *, final_norm):
    ya = _dot_tn(oat_ref[0, 0], wpa_ref[...])
    yb = _dot(ob_ref[0], wpb_ref[...])
    merged = sga_ref[0].astype(F32) * ya + sgb_ref[0].astype(F32) * yb
    r = x_ref[0] + _dot(merged.astype(BF16), wo_ref[...])
    if final_norm:
        ms = jnp.mean(r * r, axis=-1, keepdims=True)
        r = r * lax.rsqrt(ms + RMS_EPS) * gf_ref[...]
    y_ref[0] = r


def _out_stage(oat, ob, sga, sgb, x, wpa, wpb, wo, gf, final_norm):
    B, S, D = x.shape
    nt = S // ROW_TILE
    rt = ROW_TILE
    const = lambda shape: pl.BlockSpec(shape, lambda b, t: (0,) * len(shape))
    rows = lambda w: pl.BlockSpec((1, rt, w), lambda b, t: (b, t, 0))
    return pl.pallas_call(
        functools.partial(_out_kernel, final_norm=final_norm),
        grid=(B, nt),
        in_specs=[
            pl.BlockSpec((1, 1, MOBA_WIDTH, rt), lambda b, t: (b, t, 0, 0)),
            rows(GLA_DV), rows(D), rows(D), rows(D),
            const(wpa.shape), const(wpb.shape), const(wo.shape), const((1, D)),
        ],
        out_specs=rows(D),
        out_shape=jax.ShapeDtypeStruct((B, S, D), F32),
        compiler_params=pltpu.CompilerParams(
            dimension_semantics=("arbitrary", "arbitrary"),
            vmem_limit_bytes=VMEM_LIMIT),
        name="out_stage",
    )(oat, ob, sga, sgb, x, wpa, wpb, wo, gf)


def _pair_perm():
    half = MOBA_HEAD_DIM // 2
    idx = []
    for p in range(N_PAIRS):
        for part in range(2):
            for hh in range(2):
                base = (2 * p + hh) * MOBA_HEAD_DIM + part * half
                idx.extend(range(base, base + half))
    return np.asarray(idx, np.int32)


def _rope_tables(S):
    half = MOBA_HEAD_DIM // 2
    inv_freq = 1.0 / (ROPE_THETA ** (jnp.arange(half, dtype=F32) / half))
    ang = jnp.arange(S, dtype=F32)[:, None] * inv_freq[None, :]
    cos = jnp.tile(jnp.cos(ang), (1, 4))
    sin = jnp.tile(jnp.sin(ang), (1, 4))
    sign = jnp.where(jnp.arange(PAIR) < PAIR // 2, -1.0, 1.0).astype(F32)
    return cos, sin * sign[None, :]


def kernel(x, norm_in_g, w_in, b_merge, w_gla_fg2, b_gla_fg, gla_norm_g,
           w_proj_a, w_proj_b, w_out, norm_f_g):
    B, S, D = x.shape
    depth = w_in.shape[0]
    nt = S // MOBA_BLOCK
    perm = _pair_perm()
    cos, sin = _rope_tables(S)
    q_scale = MOBA_HEAD_DIM ** -0.5 * math.log2(math.e)
    cq, sq = (cos * q_scale).T, (sin * q_scale).T
    head_of_lane = (np.arange(PAIR) // (MOBA_HEAD_DIM // 2)) % 2
    head_mask = jnp.asarray(head_of_lane[None, :] == np.arange(2)[:, None], F32)

    for layer in range(depth):
        w = w_in[layer]
        o = np.cumsum([0, MOBA_WIDTH, MOBA_WIDTH, MOBA_WIDTH, MOBA_WIDTH, GLA_DK, GLA_DK,
                       GLA_DV, GLA_DV, GLA_GATE_RANK, D_MODEL, D_MODEL])
        wq, wk, wv, wmg, wgq, wgk, wgv, wgg, wfg, wga, wgb = [
            w[:, o[i]:o[i + 1]] for i in range(11)]
        wfg_p = jnp.pad(wfg, ((0, 0), (0, FG_PAD - GLA_GATE_RANK)))
        wn = jnp.concatenate([wk[:, perm], wgq, wgk, wgv, wgg, wfg_p, wga, wgb],
                             axis=1).astype(BF16)
        wt = jnp.concatenate([wq[:, perm], wv, wmg], axis=1).T.astype(BF16)
        wfg2 = jnp.pad(w_gla_fg2[layer], ((0, FG_PAD - GLA_GATE_RANK), (0, 0))).astype(BF16)

        (qt, vt, smgt, k, kmean, gq, gk, gv, sgg, la, sga, sgb) = _in_proj(
            x, norm_in_g[layer][None, :], wn, wt, cq, sq, cos, sin, wfg2,
            b_gla_fg[layer][None, :], b_merge[layer])

        km = kmean.reshape(B, nt, N_PAIRS, PAIR).transpose(0, 2, 1, 3)
        km = (km[:, :, None, :, :] * head_mask[None, None, :, None, :]).reshape(
            B, N_PAIRS, 2 * nt, PAIR)
        km_hi = km.astype(BF16)
        km_lo = (km - km_hi.astype(F32)).astype(BF16)

        oat = _moba(qt, k, vt, smgt, km_hi, km_lo)
        ob = _gla(gq, gk, gv, la, sgg, gla_norm_g[layer][None, :])
        x = _out_stage(oat, ob, sga, sgb, x, w_proj_a[layer].astype(BF16),
                       w_proj_b[layer].astype(BF16), w_out[layer].astype(BF16),
                       norm_f_g[None, :], final_norm=(layer == depth - 1))
    return x
```

```python
import functools
import math

import jax
import jax.numpy as jnp
import numpy as np
from jax import lax
from jax.experimental import pallas as pl
from jax.experimental.pallas import tpu as pltpu

F32 = jnp.float32
BF16 = jnp.bfloat16

D_MODEL = 1024
MOBA_HEADS = 8
MOBA_HEAD_DIM = 64
MOBA_WIDTH = MOBA_HEADS * MOBA_HEAD_DIM
MOBA_BLOCK = 256
MOBA_TOPK = 3
ROPE_THETA = 10000.0
GLA_HEADS = 4
GLA_DK = 256
GLA_DV = 512
GLA_HEAD_K = 64
GLA_HEAD_V = 128
GLA_GATE_RANK = 16
GLA_GATE_NORM = 16.0
GLA_CHUNK = 64
RMS_EPS = 1e-6
NEG_INF = -1e30

LANES = 128
ROW_TILE = MOBA_BLOCK
PAIR = 2 * MOBA_HEAD_DIM
N_PAIRS = MOBA_HEADS // 2
GLA_STEP_ROWS = 512
QK_AHEAD = 4
FG_PAD = LANES
VMEM_LIMIT = 56 * 1024 * 1024

_C_K = 0
_C_GQ = _C_K + MOBA_WIDTH
_C_GK = _C_GQ + GLA_DK
_C_GV = _C_GK + GLA_DK
_C_GG = _C_GV + GLA_DV
_C_FG = _C_GG + GLA_DV
_C_GA = _C_FG + FG_PAD
_C_GB = _C_GA + D_MODEL
_C_END = _C_GB + D_MODEL


def _dot(a, b):
    return jnp.dot(a, b, preferred_element_type=F32)


def _dot_nt(a, b):
    return lax.dot_general(a, b, (((1,), (1,)), ((), ())), preferred_element_type=F32)


def _dot_tn(a, b):
    return lax.dot_general(a, b, (((0,), (0,)), ((), ())), preferred_element_type=F32)


def _sigmoid(x):
    return 1.0 / (1.0 + jnp.exp(-x))


def _silu(x):
    return x * _sigmoid(x)


def _in_proj_kernel(x_ref, g_ref, wn_ref, wt_ref, cq_ref, sq_ref, ck_ref, sk_ref,
                    wfg2_ref, bfg_ref, bm_ref,
                    qt_ref, vt_ref, smgt_ref, k_ref, kmean_ref, gq_ref, gk_ref, gv_ref,
                    sgg_ref, la_ref, sga_ref, sgb_ref):
    x = x_ref[0]
    ms = jnp.mean(x * x, axis=-1, keepdims=True)
    hb = (x * lax.rsqrt(ms + RMS_EPS) * g_ref[...]).astype(BF16)

    pt = _dot_nt(wt_ref[...], hb)
    cq = cq_ref[...]
    sq = sq_ref[...]
    half = PAIR // 2
    for p in range(N_PAIRS):
        blk = pt[p * PAIR:(p + 1) * PAIR]
        swapped = jnp.concatenate([blk[half:], blk[:half]], axis=0)
        qt_ref[0, 0, p * PAIR:(p + 1) * PAIR, :] = (blk * cq + swapped * sq).astype(BF16)
    vt_ref[0, 0] = pt[MOBA_WIDTH:2 * MOBA_WIDTH].astype(BF16)
    smgt_ref[0, 0] = _silu(pt[2 * MOBA_WIDTH:3 * MOBA_WIDTH]).astype(BF16)

    pn = _dot(hb, wn_ref[...])
    ck = ck_ref[...]
    sk = sk_ref[...]
    for p in range(N_PAIRS):
        blk = pn[:, _C_K + p * PAIR:_C_K + (p + 1) * PAIR]
        kr = blk * ck + pltpu.roll(blk, half, axis=1) * sk
        k_ref[0, :, p * PAIR:(p + 1) * PAIR] = kr.astype(BF16)
        kmean_ref[0, 0, :, p * PAIR:(p + 1) * PAIR] = jnp.mean(kr, axis=0, keepdims=True)

    gq_ref[0] = (pn[:, _C_GQ:_C_GK] * (GLA_HEAD_K ** -0.5)).astype(BF16)
    gk_ref[0] = pn[:, _C_GK:_C_GV].astype(BF16)
    gv_ref[0] = pn[:, _C_GV:_C_GG].astype(BF16)
    sgg_ref[0] = _silu(pn[:, _C_GG:_C_FG]).astype(BF16)

    fg = pn[:, _C_FG:_C_GA].astype(BF16)
    z = _dot(fg, wfg2_ref[...]) + bfg_ref[...]
    log_sig = jnp.minimum(z, 0.0) - jnp.log1p(jnp.exp(-jnp.abs(z)))
    la_ref[0] = log_sig * (1.0 / GLA_GATE_NORM)

    sga_ref[0] = _sigmoid(pn[:, _C_GA:_C_GB] + bm_ref[0:1, :]).astype(BF16)
    sgb_ref[0] = _sigmoid(pn[:, _C_GB:_C_END] + bm_ref[1:2, :]).astype(BF16)


def _in_proj(x, g, wn, wt, cq, sq, ck, sk, wfg2, bfg, bm):
    B, S, D = x.shape
    nt = S // ROW_TILE
    rt = ROW_TILE
    const = lambda shape: pl.BlockSpec(shape, lambda b, t: (0,) * len(shape))
    t_blocked = pl.BlockSpec((1, 1, MOBA_WIDTH, rt), lambda b, t: (b, t, 0, 0))
    rows = lambda w: pl.BlockSpec((1, rt, w), lambda b, t: (b, t, 0))
    t_shape = jax.ShapeDtypeStruct((B, nt, MOBA_WIDTH, rt), BF16)
    rshape = lambda w, dt=BF16: jax.ShapeDtypeStruct((B, S, w), dt)
    return pl.pallas_call(
        _in_proj_kernel,
        grid=(B, nt),
        in_specs=[
            rows(D), const((1, D)), const(wn.shape), const(wt.shape),
            pl.BlockSpec((PAIR, rt), lambda b, t: (0, t)),
            pl.BlockSpec((PAIR, rt), lambda b, t: (0, t)),
            pl.BlockSpec((rt, PAIR), lambda b, t: (t, 0)),
            pl.BlockSpec((rt, PAIR), lambda b, t: (t, 0)),
            const(wfg2.shape), const((1, GLA_DK)), const((2, D)),
        ],
        out_specs=[
            t_blocked, t_blocked, t_blocked, rows(MOBA_WIDTH),
            pl.BlockSpec((1, 1, 1, MOBA_WIDTH), lambda b, t: (b, t, 0, 0)),
            rows(GLA_DK), rows(GLA_DK), rows(GLA_DV), rows(GLA_DV), rows(GLA_DK),
            rows(D), rows(D),
        ],
        out_shape=[
            t_shape, t_shape, t_shape, rshape(MOBA_WIDTH),
            jax.ShapeDtypeStruct((B, nt, 1, MOBA_WIDTH), F32),
            rshape(GLA_DK), rshape(GLA_DK), rshape(GLA_DV), rshape(GLA_DV),
            rshape(GLA_DK, F32), rshape(D), rshape(D),
        ],
        compiler_params=pltpu.CompilerParams(
            dimension_semantics=("arbitrary", "arbitrary"),
            vmem_limit_bytes=VMEM_LIMIT),
        name="in_proj",
    )(x, g, wn, wt, cq, sq, ck, sk, wfg2, bfg, bm)


def _moba_kernel(qt_ref, k_ref, vt_ref, smgt_ref, kmh_ref, kml_ref, o_ref,
                 sel_ref, qm_ref, m_ref, l_ref, acc_ref):
    qi = pl.program_id(1)
    nb = MOBA_BLOCK
    hd = MOBA_HEAD_DIM
    n_blocks = k_ref.shape[1] // nb

    blk_id = lax.broadcasted_iota(jnp.int32, (n_blocks, nb), 0)
    row = lax.broadcasted_iota(jnp.int32, (PAIR, nb), 0)
    in_head0 = (row % hd) < (hd // 2)
    for p in range(N_PAIRS):
        qt = qt_ref[0, 0, p * PAIR:(p + 1) * PAIR, :]
        gates = _dot(kmh_ref[0, p], qt) + _dot(kml_ref[0, p], qt)
        for hh in range(2):
            h = 2 * p + hh
            gh = gates[hh * n_blocks:(hh + 1) * n_blocks]
            cnt = jnp.zeros((n_blocks, nb), jnp.int32)
            for m in range(n_blocks):
                gm = gh[m:m + 1]
                beats = (gm > gh) | ((gm == gh) & (m < blk_id))
                cnt = cnt + jnp.where(beats, (m < qi).astype(jnp.int32), 0)
            sel = (blk_id < qi) & (cnt < MOBA_TOPK)
            sel_ref[h * n_blocks:(h + 1) * n_blocks, :] = jnp.where(sel, 1.0, 0.0)
        zero = jnp.zeros_like(qt)
        qm_ref[2 * p] = jnp.where(in_head0, qt, zero)
        qm_ref[2 * p + 1] = jnp.where(in_head0, zero, qt)

    def attend(n, mask_of, first):
        row0 = pl.multiple_of(n * nb, nb)

        def scores(h):
            p = h // 2
            kb = k_ref[0, pl.ds(row0, nb), p * PAIR:(p + 1) * PAIR]
            return _dot(kb, qm_ref[h])

        pending = [scores(h) for h in range(QK_AHEAD)]
        for h in range(MOBA_HEADS):
            if h + QK_AHEAD < MOBA_HEADS:
                pending.append(scores(h + QK_AHEAD))
            s = jnp.where(mask_of(h), pending[h], NEG_INF)
            vt = vt_ref[0, n, h * hd:(h + 1) * hd, :]
            bmax = jnp.max(s, axis=0, keepdims=True)
            if first:
                pr = jnp.exp2(s - bmax)
                m_ref[h] = bmax
                l_ref[h] = jnp.sum(pr, axis=0, keepdims=True)
                acc_ref[h] = _dot(vt, pr.astype(BF16))
            else:
                m_old = m_ref[h]
                m_new = jnp.maximum(m_old, bmax)
                alpha = jnp.exp2(m_old - m_new)
                pr = jnp.exp2(s - m_new)
                m_ref[h] = m_new
                l_ref[h] = alpha * l_ref[h] + jnp.sum(pr, axis=0, keepdims=True)
                acc_ref[h] = alpha * acc_ref[h] + _dot(vt, pr.astype(BF16))

    key_pos = lax.broadcasted_iota(jnp.int32, (nb, nb), 0)
    qry_pos = lax.broadcasted_iota(jnp.int32, (nb, nb), 1)
    causal = key_pos <= qry_pos
    attend(qi, lambda h: causal, True)

    def body(n, c):
        attend(n, lambda h: sel_ref[pl.ds(h * n_blocks + n, 1), :] > 0.5, False)
        return c

    lax.fori_loop(0, qi, body, 0)
    for h in range(MOBA_HEADS):
        gate = smgt_ref[0, 0, h * hd:(h + 1) * hd, :].astype(F32)
        o_ref[0, 0, h * hd:(h + 1) * hd, :] = (
            acc_ref[h] / l_ref[h] * gate).astype(BF16)


def _moba(qt, k, vt, smgt, kmh, kml):
    B, nt, W, rt = qt.shape
    S = k.shape[1]
    tile = pl.BlockSpec((1, 1, W, rt), lambda b, t: (b, t, 0, 0))
    km = pl.BlockSpec((1, N_PAIRS, 2 * nt, PAIR), lambda b, t: (b, 0, 0, 0))
    return pl.pallas_call(
        _moba_kernel,
        grid=(B, nt),
        in_specs=[
            tile,
            pl.BlockSpec((1, S, W), lambda b, t: (b, 0, 0)),
            pl.BlockSpec((1, nt, W, rt), lambda b, t: (b, 0, 0, 0)),
            tile, km, km,
        ],
        out_specs=tile,
        out_shape=jax.ShapeDtypeStruct((B, nt, W, rt), BF16),
        scratch_shapes=[
            pltpu.VMEM((MOBA_HEADS * nt, rt), F32),
            pltpu.VMEM((MOBA_HEADS, PAIR, rt), BF16),
            pltpu.VMEM((MOBA_HEADS, 1, rt), F32),
            pltpu.VMEM((MOBA_HEADS, 1, rt), F32),
            pltpu.VMEM((MOBA_HEADS, MOBA_HEAD_DIM, rt), F32),
        ],
        compiler_params=pltpu.CompilerParams(
            dimension_semantics=("arbitrary", "arbitrary"),
            vmem_limit_bytes=VMEM_LIMIT),
        name="moba",
    )(qt, k, vt, smgt, kmh, kml)


def _gla_kernel(q_ref, k_ref, v_ref, la_ref, sgg_ref, gn_ref, o_ref, st_ref):
    C = GLA_CHUNK
    hk, hv = GLA_HEAD_K, GLA_HEAD_V

    @pl.when(pl.program_id(1) == 0)
    def _():
        st_ref[...] = jnp.zeros_like(st_ref)

    r_i = lax.broadcasted_iota(jnp.int32, (C, C), 0)
    c_i = lax.broadcasted_iota(jnp.int32, (C, C), 1)
    tril = r_i >= c_i
    tril_f = jnp.where(tril, 1.0, 0.0).astype(F32)
    sr = lax.broadcasted_iota(jnp.int32, (GLA_DV, GLA_DK), 0) // hv
    sc = lax.broadcasted_iota(jnp.int32, (GLA_DV, GLA_DK), 1) // hk
    same_head = sr == sc
    lane_head = lax.broadcasted_iota(jnp.int32, (C, GLA_DK), 1) // hk
    gn = gn_ref[...]

    for c in range(GLA_STEP_ROWS // C):
        rows = slice(c * C, (c + 1) * C)
        q = q_ref[0, rows, :].astype(F32)
        k = k_ref[0, rows, :].astype(F32)
        v = v_ref[0, rows, :]
        g = la_ref[0, rows, :]
        b = jnp.dot(tril_f, g, preferred_element_type=F32,
                    precision=lax.Precision.HIGHEST)
        b_mid = b[C // 2 - 1:C // 2]
        b_last = b[C - 1:C]
        qe = q * jnp.exp(b - b_mid)
        ke = k * jnp.exp(b_mid - b)
        qb = (qe * jnp.exp(b_mid)).astype(BF16)
        kl = (ke * jnp.exp(b_last - b_mid)).astype(BF16)
        ke = ke.astype(BF16)

        st = st_ref[...]
        o = _dot_nt(qb, st.astype(BF16))
        intra = []
        for h in range(GLA_HEADS):
            qh = jnp.where(lane_head == h, qe, 0.0).astype(BF16)
            attn = jnp.where(tril, _dot_nt(qh, ke), 0.0).astype(BF16)
            intra.append(_dot(attn, v[:, h * hv:(h + 1) * hv]))
        o = o + jnp.concatenate(intra, axis=1)

        for h in range(GLA_HEADS):
            oh = o[:, h * hv:(h + 1) * hv]
            ms = jnp.mean(oh * oh, axis=-1, keepdims=True)
            y = oh * lax.rsqrt(ms + RMS_EPS) * gn
            gate = sgg_ref[0, rows, h * hv:(h + 1) * hv].astype(F32)
            o_ref[0, rows, h * hv:(h + 1) * hv] = (y * gate).astype(BF16)

        upd = _dot_tn(v, kl)
        st_ref[...] = st * jnp.exp(b_last) + jnp.where(same_head, upd, 0.0)


def _gla(gq, gk, gv, la, sgg, gn):
    B, S, _ = gq.shape
    rows = lambda w: pl.BlockSpec((1, GLA_STEP_ROWS, w), lambda b, t: (b, t, 0))
    return pl.pallas_call(
        _gla_kernel,
        grid=(B, S // GLA_STEP_ROWS),
        in_specs=[rows(GLA_DK), rows(GLA_DK), rows(GLA_DV), rows(GLA_DK), rows(GLA_DV),
                  pl.BlockSpec((1, GLA_HEAD_V), lambda b, t: (0, 0))],
        out_specs=rows(GLA_DV),
        out_shape=jax.ShapeDtypeStruct((B, S, GLA_DV), BF16),
        scratch_shapes=[pltpu.VMEM((GLA_DV, GLA_DK), F32)],
        compiler_params=pltpu.CompilerParams(
            dimension_semantics=("arbitrary", "arbitrary"),
            vmem_limit_bytes=VMEM_LIMIT),
        name="gla",
    )(gq, gk, gv, la, sgg, gn)


def _out_kernel(oat_ref, ob_ref, sga_ref, sgb_ref, x_ref, wpa_ref, wpb_ref, wo_ref, gf_ref,
                y_ref, *, final_norm):
    ya = _dot_tn(oat_ref[0, 0], wpa_ref[...])
    yb = _dot(ob_ref[0], wpb_ref[...])
    merged = sga_ref[0].astype(F32) * ya + sgb_ref[0].astype(F32) * yb
    r = x_ref[0] + _dot(merged.astype(BF16), wo_ref[...])
    if final_norm:
        ms = jnp.mean(r * r, axis=-1, keepdims=True)
        r = r * lax.rsqrt(ms + RMS_EPS) * gf_ref[...]
    y_ref[0] = r


def _out_stage(oat, ob, sga, sgb, x, wpa, wpb, wo, gf, final_norm):
    B, S, D = x.shape
    nt = S // ROW_TILE
    rt = ROW_TILE
    const = lambda shape: pl.BlockSpec(shape, lambda b, t: (0,) * len(shape))
    rows = lambda w: pl.BlockSpec((1, rt, w), lambda b, t: (b, t, 0))
    return pl.pallas_call(
        functools.partial(_out_kernel, final_norm=final_norm),
        grid=(B, nt),
        in_specs=[
            pl.BlockSpec((1, 1, MOBA_WIDTH, rt), lambda b, t: (b, t, 0, 0)),
            rows(GLA_DV), rows(D), rows(D), rows(D),
            const(wpa.shape), const(wpb.shape), const(wo.shape), const((1, D)),
        ],
        out_specs=rows(D),
        out_shape=jax.ShapeDtypeStruct((B, S, D), F32),
        compiler_params=pltpu.CompilerParams(
            dimension_semantics=("arbitrary", "arbitrary"),
            vmem_limit_bytes=VMEM_LIMIT),
        name="out_stage",
    )(oat, ob, sga, sgb, x, wpa, wpb, wo, gf)


def _pair_perm():
    half = MOBA_HEAD_DIM // 2
    idx = []
    for p in range(N_PAIRS):
        for part in range(2):
            for hh in range(2):
                base = (2 * p + hh) * MOBA_HEAD_DIM + part * half
                idx.extend(range(base, base + half))
    return np.asarray(idx, np.int32)


def _rope_tables(S):
    half = MOBA_HEAD_DIM // 2
    inv_freq = 1.0 / (ROPE_THETA ** (jnp.arange(half, dtype=F32) / half))
    ang = jnp.arange(S, dtype=F32)[:, None] * inv_freq[None, :]
    cos = jnp.tile(jnp.cos(ang), (1, 4))
    sin = jnp.tile(jnp.sin(ang), (1, 4))
    sign = jnp.where(jnp.arange(PAIR) < PAIR // 2, -1.0, 1.0).astype(F32)
    return cos, sin * sign[None, :]


def kernel(x, norm_in_g, w_in, b_merge, w_gla_fg2, b_gla_fg, gla_norm_g,
           w_proj_a, w_proj_b, w_out, norm_f_g):
    B, S, D = x.shape
    depth = w_in.shape[0]
    nt = S // MOBA_BLOCK
    perm = _pair_perm()
    cos, sin = _rope_tables(S)
    q_scale = MOBA_HEAD_DIM ** -0.5 * math.log2(math.e)
    cq, sq = (cos * q_scale).T, (sin * q_scale).T
    head_of_lane = (np.arange(PAIR) // (MOBA_HEAD_DIM // 2)) % 2
    head_mask = jnp.asarray(head_of_lane[None, :] == np.arange(2)[:, None], F32)

    for layer in range(depth):
        w = w_in[layer]
        o = np.cumsum([0, MOBA_WIDTH, MOBA_WIDTH, MOBA_WIDTH, MOBA_WIDTH, GLA_DK, GLA_DK,
                       GLA_DV, GLA_DV, GLA_GATE_RANK, D_MODEL, D_MODEL])
        wq, wk, wv, wmg, wgq, wgk, wgv, wgg, wfg, wga, wgb = [
            w[:, o[i]:o[i + 1]] for i in range(11)]
        wfg_p = jnp.pad(wfg, ((0, 0), (0, FG_PAD - GLA_GATE_RANK)))
        wn = jnp.concatenate([wk[:, perm], wgq, wgk, wgv, wgg, wfg_p, wga, wgb],
                             axis=1).astype(BF16)
        wt = jnp.concatenate([wq[:, perm], wv, wmg], axis=1).T.astype(BF16)
        wfg2 = jnp.pad(w_gla_fg2[layer], ((0, FG_PAD - GLA_GATE_RANK), (0, 0))).astype(BF16)

        (qt, vt, smgt, k, kmean, gq, gk, gv, sgg, la, sga, sgb) = _in_proj(
            x, norm_in_g[layer][None, :], wn, wt, cq, sq, cos, sin, wfg2,
            b_gla_fg[layer][None, :], b_merge[layer])

        km = kmean.reshape(B, nt, N_PAIRS, PAIR).transpose(0, 2, 1, 3)
        km = (km[:, :, None, :, :] * head_mask[None, None, :, None, :]).reshape(
            B, N_PAIRS, 2 * nt, PAIR)
        km_hi = km.astype(BF16)
        km_lo = (km - km_hi.astype(F32)).astype(BF16)

        oat = _moba(qt, k, vt, smgt, km_hi, km_lo)
        ob = _gla(gq, gk, gv, la, sgg, gla_norm_g[layer][None, :])
        x = _out_stage(oat, ob, sga, sgb, x, w_proj_a[layer].astype(BF16),
                       w_proj_b[layer].astype(BF16), w_out[layer].astype(BF16),
                       norm_f_g[None, :], final_norm=(layer == depth - 1))
    return x
```

```python
import functools
import math

import jax
import jax.numpy as jnp
import numpy as np
from jax import lax
from jax.experimental import pallas as pl
from jax.experimental.pallas import tpu as pltpu

F32 = jnp.float32
BF16 = jnp.bfloat16

D_MODEL = 1024
MOBA_HEADS = 8
MOBA_HEAD_DIM = 64
MOBA_WIDTH = MOBA_HEADS * MOBA_HEAD_DIM
MOBA_BLOCK = 256
MOBA_TOPK = 3
ROPE_THETA = 10000.0
GLA_HEADS = 4
GLA_DK = 256
GLA_DV = 512
GLA_HEAD_K = 64
GLA_HEAD_V = 128
GLA_GATE_RANK = 16
GLA_GATE_NORM = 16.0
GLA_CHUNK = 64
RMS_EPS = 1e-6
NEG_INF = -1e30

LANES = 128
BF16_ROWS = 16
ROW_TILE = MOBA_BLOCK
PAIR = 2 * MOBA_HEAD_DIM
N_PAIRS = MOBA_HEADS // 2
GLA_STEP_ROWS = 512
QK_AHEAD = 4
FG_PAD = LANES
VMEM_LIMIT = 56 * 1024 * 1024

HALF = MOBA_HEAD_DIM // 2
KX = LANES
KX_LO, KX_OH, KX_HI = 0, HALF, 2 * HALF
N_BLOCKS_MAX = BF16_ROWS
VX = MOBA_HEAD_DIM + BF16_ROWS

_C_K = 0
_C_GQ = _C_K + MOBA_HEADS * KX
_C_GK = _C_GQ + GLA_DK
_C_GV = _C_GK + GLA_DK
_C_GG = _C_GV + GLA_DV
_C_FG = _C_GG + GLA_DV
_C_GA = _C_FG + FG_PAD
_C_GB = _C_GA + D_MODEL
_C_END = _C_GB + D_MODEL


def _dot(a, b):
    return jnp.dot(a, b, preferred_element_type=F32)


def _dot_nt(a, b):
    return lax.dot_general(a, b, (((1,), (1,)), ((), ())), preferred_element_type=F32)


def _dot_tn(a, b):
    return lax.dot_general(a, b, (((0,), (0,)), ((), ())), preferred_element_type=F32)


def _sigmoid(x):
    return 1.0 / (1.0 + jnp.exp(-x))


def _silu(x):
    return x * _sigmoid(x)


def _in_proj_kernel(x_ref, g_ref, wn_ref, wt_ref, cq_ref, sq_ref, ck_ref, sk_ref,
                    wfg2_ref, bfg_ref, bm_ref,
                    qt_ref, vt_ref, smgt_ref, k_ref, kmean_ref, gq_ref, gk_ref, gv_ref,
                    sgg_ref, la_ref, sga_ref, sgb_ref):
    t = pl.program_id(1)
    rt = x_ref.shape[1]
    hd = MOBA_HEAD_DIM
    x = x_ref[0]
    ms = jnp.mean(x * x, axis=-1, keepdims=True)
    hb = (x * lax.rsqrt(ms + RMS_EPS) * g_ref[...]).astype(BF16)

    pt = _dot_nt(wt_ref[...], hb)
    cq = cq_ref[...]
    sq = sq_ref[...]
    ones_rows = jnp.where(lax.broadcasted_iota(jnp.int32, (BF16_ROWS, rt), 0) == 0,
                          1.0, 0.0).astype(BF16)
    for h in range(MOBA_HEADS):
        blk = pt[h * hd:(h + 1) * hd]
        swapped = jnp.concatenate([blk[HALF:], blk[:HALF]], axis=0)
        qt_ref[0, 0, h * hd:(h + 1) * hd, :] = (blk * cq + swapped * sq).astype(BF16)
        v_rows = pt[MOBA_WIDTH + h * hd:MOBA_WIDTH + (h + 1) * hd]
        vt_ref[0, 0, h * VX:h * VX + hd, :] = v_rows.astype(BF16)
        vt_ref[0, 0, h * VX + hd:(h + 1) * VX, :] = ones_rows
    smgt_ref[0, 0] = _silu(pt[2 * MOBA_WIDTH:3 * MOBA_WIDTH]).astype(BF16)

    pn = _dot(hb, wn_ref[...])
    ck = ck_ref[...]
    sk = sk_ref[...]
    lane = lax.broadcasted_iota(jnp.int32, (rt, KX), 1)
    block_onehot = jnp.where(lane == KX_OH + t, 1.0, 0.0)
    for h in range(MOBA_HEADS):
        blk = pn[:, _C_K + h * KX:_C_K + (h + 1) * KX]
        kr = blk * ck + pltpu.roll(blk, KX_HI, axis=1) * sk
        kmean_ref[0, 0, :, h * KX:(h + 1) * KX] = jnp.mean(kr, axis=0, keepdims=True)
        k_ref[0, :, h * KX:(h + 1) * KX] = (kr + block_onehot).astype(BF16)

    gq_ref[0] = (pn[:, _C_GQ:_C_GK] * (GLA_HEAD_K ** -0.5)).astype(BF16)
    gk_ref[0] = pn[:, _C_GK:_C_GV].astype(BF16)
    gv_ref[0] = pn[:, _C_GV:_C_GG].astype(BF16)
    sgg_ref[0] = _silu(pn[:, _C_GG:_C_FG]).astype(BF16)

    fg = pn[:, _C_FG:_C_GA].astype(BF16)
    z = _dot(fg, wfg2_ref[...]) + bfg_ref[...]
    log_sig = jnp.minimum(z, 0.0) - jnp.log1p(jnp.exp(-jnp.abs(z)))
    la_ref[0] = log_sig * (1.0 / GLA_GATE_NORM)

    sga_ref[0] = _sigmoid(pn[:, _C_GA:_C_GB] + bm_ref[0:1, :]).astype(BF16)
    sgb_ref[0] = _sigmoid(pn[:, _C_GB:_C_END] + bm_ref[1:2, :]).astype(BF16)


def _in_proj(x, g, wn, wt, cq, sq, ck, sk, wfg2, bfg, bm):
    B, S, D = x.shape
    nt = S // ROW_TILE
    rt = ROW_TILE
    const = lambda shape: pl.BlockSpec(shape, lambda b, t: (0,) * len(shape))
    t_blocked = lambda r: pl.BlockSpec((1, 1, r, rt), lambda b, t: (b, t, 0, 0))
    rows = lambda w: pl.BlockSpec((1, rt, w), lambda b, t: (b, t, 0))
    t_shape = lambda r: jax.ShapeDtypeStruct((B, nt, r, rt), BF16)
    rshape = lambda w, dt=BF16: jax.ShapeDtypeStruct((B, S, w), dt)
    return pl.pallas_call(
        _in_proj_kernel,
        grid=(B, nt),
        in_specs=[
            rows(D), const((1, D)), const(wn.shape), const(wt.shape),
            pl.BlockSpec((MOBA_HEAD_DIM, rt), lambda b, t: (0, t)),
            pl.BlockSpec((MOBA_HEAD_DIM, rt), lambda b, t: (0, t)),
            pl.BlockSpec((rt, KX), lambda b, t: (t, 0)),
            pl.BlockSpec((rt, KX), lambda b, t: (t, 0)),
            const(wfg2.shape), const((1, GLA_DK)), const((2, D)),
        ],
        out_specs=[
            t_blocked(MOBA_WIDTH), t_blocked(MOBA_HEADS * VX), t_blocked(MOBA_WIDTH),
            rows(MOBA_HEADS * KX),
            pl.BlockSpec((1, 1, 1, MOBA_HEADS * KX), lambda b, t: (b, t, 0, 0)),
            rows(GLA_DK), rows(GLA_DK), rows(GLA_DV), rows(GLA_DV), rows(GLA_DK),
            rows(D), rows(D),
        ],
        out_shape=[
            t_shape(MOBA_WIDTH), t_shape(MOBA_HEADS * VX), t_shape(MOBA_WIDTH),
            rshape(MOBA_HEADS * KX),
            jax.ShapeDtypeStruct((B, nt, 1, MOBA_HEADS * KX), F32),
            rshape(GLA_DK), rshape(GLA_DK), rshape(GLA_DV), rshape(GLA_DV),
            rshape(GLA_DK, F32), rshape(D), rshape(D),
        ],
        compiler_params=pltpu.CompilerParams(
            dimension_semantics=("arbitrary", "arbitrary"),
            vmem_limit_bytes=VMEM_LIMIT),
        name="in_proj",
    )(x, g, wn, wt, cq, sq, ck, sk, wfg2, bfg, bm)


def _moba_kernel(qt_ref, k_ref, vt_ref, smgt_ref, kmh_ref, kml_ref, o_ref,
                 qx_ref, s_ref, m_ref, acc_ref):
    qi = pl.program_id(1)
    nb = MOBA_BLOCK
    hd = MOBA_HEAD_DIM
    n_blocks = k_ref.shape[1] // nb
    assert n_blocks <= N_BLOCKS_MAX

    blk_id = lax.broadcasted_iota(jnp.int32, (n_blocks, nb), 0)
    qx_ref[...] = jnp.zeros_like(qx_ref)
    for p in range(N_PAIRS):
        qt = qt_ref[0, 0, p * PAIR:(p + 1) * PAIR, :]
        gates = _dot(kmh_ref[0, p], qt) + _dot(kml_ref[0, p], qt)
        for hh in range(2):
            h = 2 * p + hh
            gh = gates[hh * n_blocks:(hh + 1) * n_blocks]
            cnt = jnp.zeros((n_blocks, nb), jnp.int32)
            for m in range(n_blocks):
                gm = gh[m:m + 1]
                beats = (gm > gh) | ((gm == gh) & (m < blk_id))
                cnt = cnt + jnp.where(beats, (m < qi).astype(jnp.int32), 0)
            keep = (blk_id >= qi) | (cnt < MOBA_TOPK)
            qx_ref[h, KX_LO:KX_LO + HALF, :] = qt[hh * hd:hh * hd + HALF]
            qx_ref[h, KX_OH:KX_OH + n_blocks, :] = jnp.where(keep, 0.0, NEG_INF).astype(BF16)
            qx_ref[h, KX_HI:KX_HI + HALF, :] = qt[hh * hd + HALF:(hh + 1) * hd]

    def scores(n, h):
        kb = k_ref[0, pl.ds(pl.multiple_of(n * nb, nb), nb), h * KX:(h + 1) * KX]
        return _dot(kb, qx_ref[h])

    def attend(n, causal):
        pending = [s_ref[h] for h in range(QK_AHEAD)]
        for h in range(MOBA_HEADS):
            if h + QK_AHEAD < MOBA_HEADS:
                pending.append(scores(n, h + QK_AHEAD))
            elif causal is None:
                s_ref[h + QK_AHEAD - MOBA_HEADS] = scores(n + 1, h + QK_AHEAD - MOBA_HEADS)
            s = pending[h]
            if causal is not None:
                s = jnp.where(causal, s, NEG_INF)
            vt = vt_ref[0, n, h * VX:(h + 1) * VX, :]
            m_old = m_ref[h]
            m_new = jnp.maximum(m_old, jnp.max(s, axis=0, keepdims=True))
            m_ref[h] = m_new
            acc_ref[h] = (jnp.exp2(m_old - m_new) * acc_ref[h]
                          + _dot(vt, jnp.exp2(s - m_new).astype(BF16)))

    m_ref[...] = jnp.full(m_ref.shape, NEG_INF, F32)
    acc_ref[...] = jnp.zeros_like(acc_ref)
    for h in range(QK_AHEAD):
        s_ref[h] = scores(0, h)

    def body(n, c):
        attend(n, None)
        return c

    lax.fori_loop(0, qi, body, 0)
    key_pos = lax.broadcasted_iota(jnp.int32, (nb, nb), 0)
    qry_pos = lax.broadcasted_iota(jnp.int32, (nb, nb), 1)
    attend(qi, key_pos <= qry_pos)
    for h in range(MOBA_HEADS):
        gate = smgt_ref[0, 0, h * hd:(h + 1) * hd, :].astype(F32)
        o_ref[0, 0, h * hd:(h + 1) * hd, :] = (
            acc_ref[h, 0:hd, :] / acc_ref[h, hd:hd + 1, :] * gate).astype(BF16)


def _moba(qt, k, vt, smgt, kmh, kml):
    B, nt, W, rt = qt.shape
    S = k.shape[1]
    tile = pl.BlockSpec((1, 1, W, rt), lambda b, t: (b, t, 0, 0))
    km = pl.BlockSpec((1, N_PAIRS, 2 * nt, PAIR), lambda b, t: (b, 0, 0, 0))
    return pl.pallas_call(
        _moba_kernel,
        grid=(B, nt),
        in_specs=[
            tile,
            pl.BlockSpec((1, S, MOBA_HEADS * KX), lambda b, t: (b, 0, 0)),
            pl.BlockSpec((1, nt, MOBA_HEADS * VX, rt), lambda b, t: (b, 0, 0, 0)),
            tile, km, km,
        ],
        out_specs=tile,
        out_shape=jax.ShapeDtypeStruct((B, nt, W, rt), BF16),
        scratch_shapes=[
            pltpu.VMEM((MOBA_HEADS, KX, rt), BF16),
            pltpu.VMEM((QK_AHEAD, MOBA_BLOCK, rt), F32),
            pltpu.VMEM((MOBA_HEADS, 1, rt), F32),
            pltpu.VMEM((MOBA_HEADS, VX, rt), F32),
        ],
        compiler_params=pltpu.CompilerParams(
            dimension_semantics=("arbitrary", "arbitrary"),
            vmem_limit_bytes=VMEM_LIMIT),
        name="moba",
    )(qt, k, vt, smgt, kmh, kml)


def _gla_kernel(q_ref, k_ref, v_ref, la_ref, sgg_ref, gn_ref, o_ref, st_ref):
    C = GLA_CHUNK
    hk, hv = GLA_HEAD_K, GLA_HEAD_V

    @pl.when(pl.program_id(1) == 0)
    def _():
        st_ref[...] = jnp.zeros_like(st_ref)

    r_i = lax.broadcasted_iota(jnp.int32, (C, C), 0)
    c_i = lax.broadcasted_iota(jnp.int32, (C, C), 1)
    tril = r_i >= c_i
    tril_f = jnp.where(tril, 1.0, 0.0).astype(F32)
    sr = lax.broadcasted_iota(jnp.int32, (GLA_DV, GLA_DK), 0) // hv
    sc = lax.broadcasted_iota(jnp.int32, (GLA_DV, GLA_DK), 1) // hk
    same_head = sr == sc
    lane_head = lax.broadcasted_iota(jnp.int32, (C, GLA_DK), 1) // hk
    gn = gn_ref[...]

    for c in range(GLA_STEP_ROWS // C):
        rows = slice(c * C, (c + 1) * C)
        q = q_ref[0, rows, :].astype(F32)
        k = k_ref[0, rows, :].astype(F32)
        v = v_ref[0, rows, :]
        g = la_ref[0, rows, :]
        b = jnp.dot(tril_f, g, preferred_element_type=F32,
                    precision=lax.Precision.HIGHEST)
        b_mid = b[C // 2 - 1:C // 2]
        b_last = b[C - 1:C]
        qe = q * jnp.exp(b - b_mid)
        ke = k * jnp.exp(b_mid - b)
        qb = (qe * jnp.exp(b_mid)).astype(BF16)
        kl = (ke * jnp.exp(b_last - b_mid)).astype(BF16)
        ke = ke.astype(BF16)

        st = st_ref[...]
        o = _dot_nt(qb, st.astype(BF16))
        intra = []
        for h in range(GLA_HEADS):
            qh = jnp.where(lane_head == h, qe, 0.0).astype(BF16)
            attn = jnp.where(tril, _dot_nt(qh, ke), 0.0).astype(BF16)
            intra.append(_dot(attn, v[:, h * hv:(h + 1) * hv]))
        o = o + jnp.concatenate(intra, axis=1)

        for h in range(GLA_HEADS):
            oh = o[:, h * hv:(h + 1) * hv]
            ms = jnp.mean(oh * oh, axis=-1, keepdims=True)
            y = oh * lax.rsqrt(ms + RMS_EPS) * gn
            gate = sgg_ref[0, rows, h * hv:(h + 1) * hv].astype(F32)
            o_ref[0, rows, h * hv:(h + 1) * hv] = (y * gate).astype(BF16)

        upd = _dot_tn(v, kl)
        st_ref[...] = st * jnp.exp(b_last) + jnp.where(same_head, upd, 0.0)


def _gla(gq, gk, gv, la, sgg, gn):
    B, S, _ = gq.shape
    rows = lambda w: pl.BlockSpec((1, GLA_STEP_ROWS, w), lambda b, t: (b, t, 0))
    return pl.pallas_call(
        _gla_kernel,
        grid=(B, S // GLA_STEP_ROWS),
        in_specs=[rows(GLA_DK), rows(GLA_DK), rows(GLA_DV), rows(GLA_DK), rows(GLA_DV),
                  pl.BlockSpec((1, GLA_HEAD_V), lambda b, t: (0, 0))],
        out_specs=rows(GLA_DV),
        out_shape=jax.ShapeDtypeStruct((B, S, GLA_DV), BF16),
        scratch_shapes=[pltpu.VMEM((GLA_DV, GLA_DK), F32)],
        compiler_params=pltpu.CompilerParams(
            dimension_semantics=("arbitrary", "arbitrary"),
            vmem_limit_bytes=VMEM_LIMIT),
        name="gla",
    )(gq, gk, gv, la, sgg, gn)


def _out_kernel(oat_ref, ob_ref, sga_ref, sgb_ref, x_ref, wpa_ref, wpb_ref, wo_ref, gf_ref,
                y_ref, *, final_norm):
    ya = _dot_tn(oat_ref[0, 0], wpa_ref[...])
    yb = _dot(ob_ref[0], wpb_ref[...])
    merged = sga_ref[0].astype(F32) * ya + sgb_ref[0].astype(F32) * yb
    r = x_ref[0] + _dot(merged.astype(BF16), wo_ref[...])
    if final_norm:
        ms = jnp.mean(r * r, axis=-1, keepdims=True)
        r = r * lax.rsqrt(ms + RMS_EPS) * gf_ref[...]
    y_ref[0] = r


def _out_stage(oat, ob, sga, sgb, x, wpa, wpb, wo, gf, final_norm):
    B, S, D = x.shape
    nt = S // ROW_TILE
    rt = ROW_TILE
    const = lambda shape: pl.BlockSpec(shape, lambda b, t: (0,) * len(shape))
    rows = lambda w: pl.BlockSpec((1, rt, w), lambda b, t: (b, t, 0))
    return pl.pallas_call(
        functools.partial(_out_kernel, final_norm=final_norm),
        grid=(B, nt),
        in_specs=[
            pl.BlockSpec((1, 1, MOBA_WIDTH, rt), lambda b, t: (b, t, 0, 0)),
            rows(GLA_DV), rows(D), rows(D), rows(D),
            const(wpa.shape), const(wpb.shape), const(wo.shape), const((1, D)),
        ],
        out_specs=rows(D),
        out_shape=jax.ShapeDtypeStruct((B, S, D), F32),
        compiler_params=pltpu.CompilerParams(
            dimension_semantics=("arbitrary", "arbitrary"),
            vmem_limit_bytes=VMEM_LIMIT),
        name="out_stage",
    )(oat, ob, sga, sgb, x, wpa, wpb, wo, gf)


def _rope_tables(S):
    inv_freq = 1.0 / (ROPE_THETA ** (jnp.arange(HALF, dtype=F32) / HALF))
    ang = jnp.arange(S, dtype=F32)[:, None] * inv_freq[None, :]
    return jnp.cos(ang), jnp.sin(ang)


def _extend_key_columns(wk):
    D = wk.shape[0]
    w = wk.reshape(D, MOBA_HEADS, 2, HALF)
    out = jnp.zeros((D, MOBA_HEADS, KX), wk.dtype)
    out = out.at[:, :, KX_LO:KX_LO + HALF].set(w[:, :, 0])
    out = out.at[:, :, KX_HI:KX_HI + HALF].set(w[:, :, 1])
    return out.reshape(D, MOBA_HEADS * KX)


def kernel(x, norm_in_g, w_in, b_merge, w_gla_fg2, b_gla_fg, gla_norm_g,
           w_proj_a, w_proj_b, w_out, norm_f_g):
    B, S, D = x.shape
    depth = w_in.shape[0]
    nt = S // MOBA_BLOCK
    cos, sin = _rope_tables(S)
    q_scale = MOBA_HEAD_DIM ** -0.5 * math.log2(math.e)
    cq = (jnp.concatenate([cos, cos], axis=1) * q_scale).T
    sq = (jnp.concatenate([-sin, sin], axis=1) * q_scale).T
    ck = jnp.tile(cos, (1, KX // HALF))
    sk = jnp.tile(sin, (1, KX // HALF)) * jnp.where(jnp.arange(KX) < KX_HI, -1.0, 1.0)[None, :]
    pair_mask = jnp.asarray(
        np.arange(PAIR)[None, :] // MOBA_HEAD_DIM == np.arange(2)[:, None], F32)

    for layer in range(depth):
        w = w_in[layer]
        o = np.cumsum([0, MOBA_WIDTH, MOBA_WIDTH, MOBA_WIDTH, MOBA_WIDTH, GLA_DK, GLA_DK,
                       GLA_DV, GLA_DV, GLA_GATE_RANK, D_MODEL, D_MODEL])
        wq, wk, wv, wmg, wgq, wgk, wgv, wgg, wfg, wga, wgb = [
            w[:, o[i]:o[i + 1]] for i in range(11)]
        wfg_p = jnp.pad(wfg, ((0, 0), (0, FG_PAD - GLA_GATE_RANK)))
        wn = jnp.concatenate([_extend_key_columns(wk), wgq, wgk, wgv, wgg, wfg_p, wga, wgb],
                             axis=1).astype(BF16)
        wt = jnp.concatenate([wq, wv, wmg], axis=1).T.astype(BF16)
        wfg2 = jnp.pad(w_gla_fg2[layer], ((0, FG_PAD - GLA_GATE_RANK), (0, 0))).astype(BF16)

        (qt, vt, smgt, k, kmean, gq, gk, gv, sgg, la, sga, sgb) = _in_proj(
            x, norm_in_g[layer][None, :], wn, wt, cq, sq, ck, sk, wfg2,
            b_gla_fg[layer][None, :], b_merge[layer])

        km = kmean.reshape(B, nt, MOBA_HEADS, KX)
        km = jnp.concatenate([km[..., KX_LO:KX_LO + HALF], km[..., KX_HI:KX_HI + HALF]], axis=-1)
        km = km.reshape(B, nt, N_PAIRS, PAIR).transpose(0, 2, 1, 3)
        km = (km[:, :, None, :, :] * pair_mask[None, None, :, None, :]).reshape(
            B, N_PAIRS, 2 * nt, PAIR)
        km_hi = km.astype(BF16)
        km_lo = (km - km_hi.astype(F32)).astype(BF16)

        oat = _moba(qt, k, vt, smgt, km_hi, km_lo)
        ob = _gla(gq, gk, gv, la, sgg, gla_norm_g[layer][None, :])
        x = _out_stage(oat, ob, sga, sgb, x, w_proj_a[layer].astype(BF16),
                       w_proj_b[layer].astype(BF16), w_out[layer].astype(BF16),
                       norm_f_g[None, :], final_norm=(layer == depth - 1))
    return x
```

```python
import functools
import math

import jax
import jax.numpy as jnp
import numpy as np
from jax import lax
from jax.experimental import pallas as pl
from jax.experimental.pallas import tpu as pltpu

F32 = jnp.float32
BF16 = jnp.bfloat16

D_MODEL = 1024
MOBA_HEADS = 8
MOBA_HEAD_DIM = 64
MOBA_WIDTH = MOBA_HEADS * MOBA_HEAD_DIM
MOBA_BLOCK = 256
MOBA_TOPK = 3
ROPE_THETA = 10000.0
GLA_HEADS = 4
GLA_DK = 256
GLA_DV = 512
GLA_HEAD_K = 64
GLA_HEAD_V = 128
GLA_GATE_RANK = 16
GLA_GATE_NORM = 16.0
GLA_CHUNK = 64
RMS_EPS = 1e-6
NEG_INF = -1e30

LANES = 128
BF16_ROWS = 16
ROW_TILE = MOBA_BLOCK
PAIR = 2 * MOBA_HEAD_DIM
N_PAIRS = MOBA_HEADS // 2
GLA_STEP_ROWS = 512
QK_AHEAD = 4
MOBA_Q_TILE = 512
FG_PAD = LANES
VMEM_LIMIT = 56 * 1024 * 1024

HALF = MOBA_HEAD_DIM // 2
KX = LANES
KX_LO, KX_OH, KX_HI = 0, HALF, 2 * HALF
N_BLOCKS_MAX = BF16_ROWS
VX = MOBA_HEAD_DIM + BF16_ROWS

_C_K = 0
_C_GQ = _C_K + MOBA_HEADS * KX
_C_GK = _C_GQ + GLA_DK
_C_GV = _C_GK + GLA_DK
_C_GG = _C_GV + GLA_DV
_C_FG = _C_GG + GLA_DV
_C_GA = _C_FG + FG_PAD
_C_GB = _C_GA + D_MODEL
_C_END = _C_GB + D_MODEL


def _dot(a, b):
    return jnp.dot(a, b, preferred_element_type=F32)


def _dot_nt(a, b):
    return lax.dot_general(a, b, (((1,), (1,)), ((), ())), preferred_element_type=F32)


def _dot_tn(a, b):
    return lax.dot_general(a, b, (((0,), (0,)), ((), ())), preferred_element_type=F32)


def _sigmoid(x):
    return 1.0 / (1.0 + jnp.exp(-x))


def _silu(x):
    return x * _sigmoid(x)


def _in_proj_kernel(x_ref, g_ref, wn_ref, wt_ref, cq_ref, sq_ref, ck_ref, sk_ref,
                    wfg2_ref, bfg_ref, bm_ref,
                    qt_ref, vt_ref, smgt_ref, k_ref, kmean_ref, gq_ref, gk_ref, gv_ref,
                    sgg_ref, la_ref, sga_ref, sgb_ref):
    t = pl.program_id(1)
    rt = x_ref.shape[1]
    hd = MOBA_HEAD_DIM
    x = x_ref[0]
    ms = jnp.mean(x * x, axis=-1, keepdims=True)
    hb = (x * lax.rsqrt(ms + RMS_EPS) * g_ref[...]).astype(BF16)

    pt = _dot_nt(wt_ref[...], hb)
    cq = cq_ref[...]
    sq = sq_ref[...]
    ones_rows = jnp.where(lax.broadcasted_iota(jnp.int32, (BF16_ROWS, rt), 0) == 0,
                          1.0, 0.0).astype(BF16)
    for h in range(MOBA_HEADS):
        blk = pt[h * hd:(h + 1) * hd]
        swapped = jnp.concatenate([blk[HALF:], blk[:HALF]], axis=0)
        qt_ref[0, 0, h * hd:(h + 1) * hd, :] = (blk * cq + swapped * sq).astype(BF16)
        v_rows = pt[MOBA_WIDTH + h * hd:MOBA_WIDTH + (h + 1) * hd]
        vt_ref[0, 0, h * VX:h * VX + hd, :] = v_rows.astype(BF16)
        vt_ref[0, 0, h * VX + hd:(h + 1) * VX, :] = ones_rows
    smgt_ref[0, 0] = _silu(pt[2 * MOBA_WIDTH:3 * MOBA_WIDTH]).astype(BF16)

    pn = _dot(hb, wn_ref[...])
    ck = ck_ref[...]
    sk = sk_ref[...]
    lane = lax.broadcasted_iota(jnp.int32, (rt, KX), 1)
    block_onehot = jnp.where(lane == KX_OH + t, 1.0, 0.0)
    for h in range(MOBA_HEADS):
        blk = pn[:, _C_K + h * KX:_C_K + (h + 1) * KX]
        kr = blk * ck + pltpu.roll(blk, KX_HI, axis=1) * sk
        kmean_ref[0, 0, :, h * KX:(h + 1) * KX] = jnp.mean(kr, axis=0, keepdims=True)
        k_ref[0, :, h * KX:(h + 1) * KX] = (kr + block_onehot).astype(BF16)

    gq_ref[0] = (pn[:, _C_GQ:_C_GK] * (GLA_HEAD_K ** -0.5)).astype(BF16)
    gk_ref[0] = pn[:, _C_GK:_C_GV].astype(BF16)
    gv_ref[0] = pn[:, _C_GV:_C_GG].astype(BF16)
    sgg_ref[0] = _silu(pn[:, _C_GG:_C_FG]).astype(BF16)

    fg = pn[:, _C_FG:_C_GA].astype(BF16)
    z = _dot(fg, wfg2_ref[...]) + bfg_ref[...]
    log_sig = jnp.minimum(z, 0.0) - jnp.log1p(jnp.exp(-jnp.abs(z)))
    la_ref[0] = log_sig * (1.0 / GLA_GATE_NORM)

    sga_ref[0] = _sigmoid(pn[:, _C_GA:_C_GB] + bm_ref[0:1, :]).astype(BF16)
    sgb_ref[0] = _sigmoid(pn[:, _C_GB:_C_END] + bm_ref[1:2, :]).astype(BF16)


def _in_proj(x, g, wn, wt, cq, sq, ck, sk, wfg2, bfg, bm):
    B, S, D = x.shape
    nt = S // ROW_TILE
    rt = ROW_TILE
    const = lambda shape: pl.BlockSpec(shape, lambda b, t: (0,) * len(shape))
    t_blocked = lambda r: pl.BlockSpec((1, 1, r, rt), lambda b, t: (b, t, 0, 0))
    rows = lambda w: pl.BlockSpec((1, rt, w), lambda b, t: (b, t, 0))
    t_shape = lambda r: jax.ShapeDtypeStruct((B, nt, r, rt), BF16)
    rshape = lambda w, dt=BF16: jax.ShapeDtypeStruct((B, S, w), dt)
    return pl.pallas_call(
        _in_proj_kernel,
        grid=(B, nt),
        in_specs=[
            rows(D), const((1, D)), const(wn.shape), const(wt.shape),
            pl.BlockSpec((MOBA_HEAD_DIM, rt), lambda b, t: (0, t)),
            pl.BlockSpec((MOBA_HEAD_DIM, rt), lambda b, t: (0, t)),
            pl.BlockSpec((rt, KX), lambda b, t: (t, 0)),
            pl.BlockSpec((rt, KX), lambda b, t: (t, 0)),
            const(wfg2.shape), const((1, GLA_DK)), const((2, D)),
        ],
        out_specs=[
            t_blocked(MOBA_WIDTH), t_blocked(MOBA_HEADS * VX), t_blocked(MOBA_WIDTH),
            rows(MOBA_HEADS * KX),
            pl.BlockSpec((1, 1, 1, MOBA_HEADS * KX), lambda b, t: (b, t, 0, 0)),
            rows(GLA_DK), rows(GLA_DK), rows(GLA_DV), rows(GLA_DV), rows(GLA_DK),
            rows(D), rows(D),
        ],
        out_shape=[
            t_shape(MOBA_WIDTH), t_shape(MOBA_HEADS * VX), t_shape(MOBA_WIDTH),
            rshape(MOBA_HEADS * KX),
            jax.ShapeDtypeStruct((B, nt, 1, MOBA_HEADS * KX), F32),
            rshape(GLA_DK), rshape(GLA_DK), rshape(GLA_DV), rshape(GLA_DV),
            rshape(GLA_DK, F32), rshape(D), rshape(D),
        ],
        compiler_params=pltpu.CompilerParams(
            dimension_semantics=("arbitrary", "arbitrary"),
            vmem_limit_bytes=VMEM_LIMIT),
        name="in_proj",
    )(x, g, wn, wt, cq, sq, ck, sk, wfg2, bfg, bm)


def _moba_kernel(qt_ref, k_ref, vt_ref, smgt_ref, kmh_ref, kml_ref, o_ref,
                 qx_ref, s_ref, smax_ref, m_ref, acc_ref):
    nb = MOBA_BLOCK
    hd = MOBA_HEAD_DIM
    n_sub = qt_ref.shape[1]
    qw = n_sub * nb
    n_blocks = k_ref.shape[1] // nb
    assert n_blocks <= N_BLOCKS_MAX
    first_own = pl.program_id(1) * n_sub

    blk_id = lax.broadcasted_iota(jnp.int32, (n_blocks, nb), 0)
    qx_ref[...] = jnp.zeros_like(qx_ref)
    for j in range(n_sub):
        own = first_own + j
        cols = slice(j * nb, (j + 1) * nb)
        for p in range(N_PAIRS):
            qt = qt_ref[0, j, p * PAIR:(p + 1) * PAIR, :]
            gates = _dot(kmh_ref[0, p], qt) + _dot(kml_ref[0, p], qt)
            for hh in range(2):
                h = 2 * p + hh
                past = blk_id < own
                gh = jnp.where(past, gates[hh * n_blocks:(hh + 1) * n_blocks], NEG_INF)
                picked = jnp.zeros((n_blocks, nb), jnp.bool_)
                for _ in range(MOBA_TOPK):
                    best = jnp.max(gh, axis=0, keepdims=True)
                    first = jnp.min(jnp.where(gh == best, blk_id, n_blocks), axis=0, keepdims=True)
                    hit = blk_id == first
                    picked = picked | hit
                    gh = jnp.where(hit, -jnp.inf, gh)
                keep = (blk_id == own) | (past & picked)
                qx_ref[h, KX_LO:KX_LO + HALF, cols] = qt[hh * hd:hh * hd + HALF]
                qx_ref[h, KX_OH:KX_OH + n_blocks, cols] = (
                    jnp.where(keep, 0.0, NEG_INF).astype(BF16))
                qx_ref[h, KX_HI:KX_HI + HALF, cols] = qt[hh * hd + HALF:(hh + 1) * hd]

    def produce_scores(n, h):
        kb = k_ref[0, pl.ds(pl.multiple_of(n * nb, nb), nb), h * KX:(h + 1) * KX]
        s = _dot(kb, qx_ref[h])
        s_ref[h] = s
        smax_ref[h] = jnp.max(s, axis=0, keepdims=True)

    def attend(n, diagonal, prefetch):
        if diagonal:
            key_pos = n * nb + lax.broadcasted_iota(jnp.int32, (nb, qw), 0)
            qry_pos = first_own * nb + lax.broadcasted_iota(jnp.int32, (nb, qw), 1)
            causal = key_pos <= qry_pos
        jobs = [(n, h) for h in range(QK_AHEAD, MOBA_HEADS)]
        if prefetch:
            jobs += [(n + 1, h) for h in range(QK_AHEAD)]

        def issue():
            if jobs:
                produce_scores(*jobs.pop(0))

        issue()
        issue()
        for h in range(MOBA_HEADS):
            s = s_ref[h]
            if diagonal:
                s = jnp.where(causal, s, NEG_INF)
                smax = jnp.max(s, axis=0, keepdims=True)
            else:
                smax = smax_ref[h]
            vt = vt_ref[0, n, h * VX:(h + 1) * VX, :]
            m_old = m_ref[h]
            m_new = jnp.maximum(m_old, smax)
            m_ref[h] = m_new
            acc_ref[h] = (jnp.exp2(m_old - m_new) * acc_ref[h]
                          + _dot(vt, jnp.exp2(s - m_new).astype(BF16)))
            issue()

    m_ref[...] = jnp.full(m_ref.shape, NEG_INF, F32)
    acc_ref[...] = jnp.zeros_like(acc_ref)
    for h in range(QK_AHEAD):
        produce_scores(0, h)

    def body(n, c):
        attend(n, False, True)
        return c

    lax.fori_loop(0, first_own, body, 0)
    for j in range(n_sub):
        attend(first_own + j, True, j + 1 < n_sub)
    for h in range(MOBA_HEADS):
        o = acc_ref[h, 0:hd, :] / acc_ref[h, hd:hd + 1, :]
        for j in range(n_sub):
            gate = smgt_ref[0, j, h * hd:(h + 1) * hd, :].astype(F32)
            o_ref[0, j, h * hd:(h + 1) * hd, :] = (o[:, j * nb:(j + 1) * nb] * gate).astype(BF16)


def _moba(qt, k, vt, smgt, kmh, kml):
    B, nt, W, rt = qt.shape
    S = k.shape[1]
    n_sub = MOBA_Q_TILE // rt
    qw = MOBA_Q_TILE
    tile = pl.BlockSpec((1, n_sub, W, rt), lambda b, t: (b, t, 0, 0))
    km = pl.BlockSpec((1, N_PAIRS, 2 * nt, PAIR), lambda b, t: (b, 0, 0, 0))
    return pl.pallas_call(
        _moba_kernel,
        grid=(B, nt // n_sub),
        in_specs=[
            tile,
            pl.BlockSpec((1, S, MOBA_HEADS * KX), lambda b, t: (b, 0, 0)),
            pl.BlockSpec((1, nt, MOBA_HEADS * VX, rt), lambda b, t: (b, 0, 0, 0)),
            tile, km, km,
        ],
        out_specs=tile,
        out_shape=jax.ShapeDtypeStruct((B, nt, W, rt), BF16),
        scratch_shapes=[
            pltpu.VMEM((MOBA_HEADS, KX, qw), BF16),
            pltpu.VMEM((MOBA_HEADS, MOBA_BLOCK, qw), F32),
            pltpu.VMEM((MOBA_HEADS, 1, qw), F32),
            pltpu.VMEM((MOBA_HEADS, 1, qw), F32),
            pltpu.VMEM((MOBA_HEADS, VX, qw), F32),
        ],
        compiler_params=pltpu.CompilerParams(
            dimension_semantics=("arbitrary", "arbitrary"),
            vmem_limit_bytes=VMEM_LIMIT),
        name="moba",
    )(qt, k, vt, smgt, kmh, kml)


def _gla_kernel(q_ref, k_ref, v_ref, la_ref, sgg_ref, gn_ref, o_ref, st_ref):
    C = GLA_CHUNK
    hk, hv = GLA_HEAD_K, GLA_HEAD_V

    @pl.when(pl.program_id(1) == 0)
    def _():
        st_ref[...] = jnp.zeros_like(st_ref)

    r_i = lax.broadcasted_iota(jnp.int32, (C, C), 0)
    c_i = lax.broadcasted_iota(jnp.int32, (C, C), 1)
    tril = r_i >= c_i
    tril_f = jnp.where(tril, 1.0, 0.0).astype(F32)
    sr = lax.broadcasted_iota(jnp.int32, (GLA_DV, GLA_DK), 0) // hv
    sc = lax.broadcasted_iota(jnp.int32, (GLA_DV, GLA_DK), 1) // hk
    same_head = sr == sc
    lane_head = lax.broadcasted_iota(jnp.int32, (C, GLA_DK), 1) // hk
    gn = gn_ref[...]

    for c in range(GLA_STEP_ROWS // C):
        rows = slice(c * C, (c + 1) * C)
        q = q_ref[0, rows, :].astype(F32)
        k = k_ref[0, rows, :].astype(F32)
        v = v_ref[0, rows, :]
        g = la_ref[0, rows, :]
        b = jnp.dot(tril_f, g, preferred_element_type=F32,
                    precision=lax.Precision.HIGHEST)
        b_mid = b[C // 2 - 1:C // 2]
        b_last = b[C - 1:C]
        qe = q * jnp.exp(b - b_mid)
        ke = k * jnp.exp(b_mid - b)
        qb = (qe * jnp.exp(b_mid)).astype(BF16)
        kl = (ke * jnp.exp(b_last - b_mid)).astype(BF16)
        ke = ke.astype(BF16)

        st = st_ref[...]
        o = _dot_nt(qb, st.astype(BF16))
        intra = []
        for h in range(GLA_HEADS):
            qh = jnp.where(lane_head == h, qe, 0.0).astype(BF16)
            attn = jnp.where(tril, _dot_nt(qh, ke), 0.0).astype(BF16)
            intra.append(_dot(attn, v[:, h * hv:(h + 1) * hv]))
        o = o + jnp.concatenate(intra, axis=1)

        for h in range(GLA_HEADS):
            oh = o[:, h * hv:(h + 1) * hv]
            ms = jnp.mean(oh * oh, axis=-1, keepdims=True)
            y = oh * lax.rsqrt(ms + RMS_EPS) * gn
            gate = sgg_ref[0, rows, h * hv:(h + 1) * hv].astype(F32)
            o_ref[0, rows, h * hv:(h + 1) * hv] = (y * gate).astype(BF16)

        upd = _dot_tn(v, kl)
        st_ref[...] = st * jnp.exp(b_last) + jnp.where(same_head, upd, 0.0)


def _gla(gq, gk, gv, la, sgg, gn):
    B, S, _ = gq.shape
    rows = lambda w: pl.BlockSpec((1, GLA_STEP_ROWS, w), lambda b, t: (b, t, 0))
    return pl.pallas_call(
        _gla_kernel,
        grid=(B, S // GLA_STEP_ROWS),
        in_specs=[rows(GLA_DK), rows(GLA_DK), rows(GLA_DV), rows(GLA_DK), rows(GLA_DV),
                  pl.BlockSpec((1, GLA_HEAD_V), lambda b, t: (0, 0))],
        out_specs=rows(GLA_DV),
        out_shape=jax.ShapeDtypeStruct((B, S, GLA_DV), BF16),
        scratch_shapes=[pltpu.VMEM((GLA_DV, GLA_DK), F32)],
        compiler_params=pltpu.CompilerParams(
            dimension_semantics=("arbitrary", "arbitrary"),
            vmem_limit_bytes=VMEM_LIMIT),
        name="gla",
    )(gq, gk, gv, la, sgg, gn)


def _out_kernel(oat_ref, ob_ref, sga_ref, sgb_ref, x_ref, wpa_ref, wpb_ref, wo_ref, gf_ref,
                y_ref, *, final_norm):
    ya = _dot_tn(oat_ref[0, 0], wpa_ref[...])
    yb = _dot(ob_ref[0], wpb_ref[...])
    merged = sga_ref[0].astype(F32) * ya + sgb_ref[0].astype(F32) * yb
    r = x_ref[0] + _dot(merged.astype(BF16), wo_ref[...])
    if final_norm:
        ms = jnp.mean(r * r, axis=-1, keepdims=True)
        r = r * lax.rsqrt(ms + RMS_EPS) * gf_ref[...]
    y_ref[0] = r


def _out_stage(oat, ob, sga, sgb, x, wpa, wpb, wo, gf, final_norm):
    B, S, D = x.shape
    nt = S // ROW_TILE
    rt = ROW_TILE
    const = lambda shape: pl.BlockSpec(shape, lambda b, t: (0,) * len(shape))
    rows = lambda w: pl.BlockSpec((1, rt, w), lambda b, t: (b, t, 0))
    return pl.pallas_call(
        functools.partial(_out_kernel, final_norm=final_norm),
        grid=(B, nt),
        in_specs=[
            pl.BlockSpec((1, 1, MOBA_WIDTH, rt), lambda b, t: (b, t, 0, 0)),
            rows(GLA_DV), rows(D), rows(D), rows(D),
            const(wpa.shape), const(wpb.shape), const(wo.shape), const((1, D)),
        ],
        out_specs=rows(D),
        out_shape=jax.ShapeDtypeStruct((B, S, D), F32),
        compiler_params=pltpu.CompilerParams(
            dimension_semantics=("arbitrary", "arbitrary"),
            vmem_limit_bytes=VMEM_LIMIT),
        name="out_stage",
    )(oat, ob, sga, sgb, x, wpa, wpb, wo, gf)


def _rope_tables(S):
    inv_freq = 1.0 / (ROPE_THETA ** (jnp.arange(HALF, dtype=F32) / HALF))
    ang = jnp.arange(S, dtype=F32)[:, None] * inv_freq[None, :]
    return jnp.cos(ang), jnp.sin(ang)


def _extend_key_columns(wk):
    D = wk.shape[0]
    w = wk.reshape(D, MOBA_HEADS, 2, HALF)
    out = jnp.zeros((D, MOBA_HEADS, KX), wk.dtype)
    out = out.at[:, :, KX_LO:KX_LO + HALF].set(w[:, :, 0])
    out = out.at[:, :, KX_HI:KX_HI + HALF].set(w[:, :, 1])
    return out.reshape(D, MOBA_HEADS * KX)


def kernel(x, norm_in_g, w_in, b_merge, w_gla_fg2, b_gla_fg, gla_norm_g,
           w_proj_a, w_proj_b, w_out, norm_f_g):
    B, S, D = x.shape
    depth = w_in.shape[0]
    nt = S // MOBA_BLOCK
    cos, sin = _rope_tables(S)
    q_scale = MOBA_HEAD_DIM ** -0.5 * math.log2(math.e)
    cq = (jnp.concatenate([cos, cos], axis=1) * q_scale).T
    sq = (jnp.concatenate([-sin, sin], axis=1) * q_scale).T
    ck = jnp.tile(cos, (1, KX // HALF))
    sk = jnp.tile(sin, (1, KX // HALF)) * jnp.where(jnp.arange(KX) < KX_HI, -1.0, 1.0)[None, :]
    pair_mask = jnp.asarray(
        np.arange(PAIR)[None, :] // MOBA_HEAD_DIM == np.arange(2)[:, None], F32)

    for layer in range(depth):
        w = w_in[layer]
        o = np.cumsum([0, MOBA_WIDTH, MOBA_WIDTH, MOBA_WIDTH, MOBA_WIDTH, GLA_DK, GLA_DK,
                       GLA_DV, GLA_DV, GLA_GATE_RANK, D_MODEL, D_MODEL])
        wq, wk, wv, wmg, wgq, wgk, wgv, wgg, wfg, wga, wgb = [
            w[:, o[i]:o[i + 1]] for i in range(11)]
        wfg_p = jnp.pad(wfg, ((0, 0), (0, FG_PAD - GLA_GATE_RANK)))
        wn = jnp.concatenate([_extend_key_columns(wk), wgq, wgk, wgv, wgg, wfg_p, wga, wgb],
                             axis=1).astype(BF16)
        wt = jnp.concatenate([wq, wv, wmg], axis=1).T.astype(BF16)
        wfg2 = jnp.pad(w_gla_fg2[layer], ((0, FG_PAD - GLA_GATE_RANK), (0, 0))).astype(BF16)

        (qt, vt, smgt, k, kmean, gq, gk, gv, sgg, la, sga, sgb) = _in_proj(
            x, norm_in_g[layer][None, :], wn, wt, cq, sq, ck, sk, wfg2,
            b_gla_fg[layer][None, :], b_merge[layer])

        km = kmean.reshape(B, nt, MOBA_HEADS, KX)
        km = jnp.concatenate([km[..., KX_LO:KX_LO + HALF], km[..., KX_HI:KX_HI + HALF]], axis=-1)
        km = km.reshape(B, nt, N_PAIRS, PAIR).transpose(0, 2, 1, 3)
        km = (km[:, :, None, :, :] * pair_mask[None, None, :, None, :]).reshape(
            B, N_PAIRS, 2 * nt, PAIR)
        km_hi = km.astype(BF16)
        km_lo = (km - km_hi.astype(F32)).astype(BF16)

        oat = _moba(qt, k, vt, smgt, km_hi, km_lo)
        ob = _gla(gq, gk, gv, la, sgg, gla_norm_g[layer][None, :])
        x = _out_stage(oat, ob, sga, sgb, x, w_proj_a[layer].astype(BF16),
                       w_proj_b[layer].astype(BF16), w_out[layer].astype(BF16),
                       norm_f_g[None, :], final_norm=(layer == depth - 1))
    return x
```

```python
import functools
import math

import jax
import jax.numpy as jnp
import numpy as np
from jax import lax
from jax.experimental import pallas as pl
from jax.experimental.pallas import tpu as pltpu

F32 = jnp.float32
BF16 = jnp.bfloat16

D_MODEL = 1024
MOBA_HEADS = 8
MOBA_HEAD_DIM = 64
MOBA_WIDTH = MOBA_HEADS * MOBA_HEAD_DIM
MOBA_BLOCK = 256
MOBA_TOPK = 3
ROPE_THETA = 10000.0
GLA_HEADS = 4
GLA_DK = 256
GLA_DV = 512
GLA_HEAD_K = 64
GLA_HEAD_V = 128
GLA_GATE_RANK = 16
GLA_GATE_NORM = 16.0
GLA_CHUNK = 64
RMS_EPS = 1e-6
NEG_INF = -1e30

LANES = 128
BF16_ROWS = 16
ROW_TILE = MOBA_BLOCK
PAIR = 2 * MOBA_HEAD_DIM
N_PAIRS = MOBA_HEADS // 2
GLA_STEP_ROWS = 512
OUT_STEP_ROWS = 512
QK_AHEAD = 4
MOBA_Q_TILE = 512
FG_PAD = LANES
VMEM_LIMIT = 56 * 1024 * 1024

HALF = MOBA_HEAD_DIM // 2
KX = LANES
KX_LO, KX_OH, KX_HI = 0, HALF, 2 * HALF
N_BLOCKS_MAX = BF16_ROWS
VX = MOBA_HEAD_DIM + BF16_ROWS

_C_K = 0
_C_GQ = _C_K + MOBA_HEADS * KX
_C_GK = _C_GQ + GLA_DK
_C_GV = _C_GK + GLA_DK
_C_GG = _C_GV + GLA_DV
_C_FG = _C_GG + GLA_DV
_C_GA = _C_FG + FG_PAD
_C_GB = _C_GA + D_MODEL
_C_END = _C_GB + D_MODEL


def _dot(a, b):
    return jnp.dot(a, b, preferred_element_type=F32)


def _dot_nt(a, b):
    return lax.dot_general(a, b, (((1,), (1,)), ((), ())), preferred_element_type=F32)


def _dot_tn(a, b):
    return lax.dot_general(a, b, (((0,), (0,)), ((), ())), preferred_element_type=F32)


def _sigmoid(x):
    return 1.0 / (1.0 + jnp.exp(-x))


def _silu(x):
    return x * _sigmoid(x)


def _in_proj_kernel(x_ref, g_ref, wn_ref, wt_ref, cq_ref, sq_ref, ck_ref, sk_ref,
                    wfg2_ref, bfg_ref, bm_ref,
                    qt_ref, vt_ref, smgt_ref, k_ref, kmean_ref, gq_ref, gk_ref, gv_ref,
                    sgg_ref, la_ref, sga_ref, sgb_ref):
    t = pl.program_id(1)
    rt = x_ref.shape[1]
    hd = MOBA_HEAD_DIM
    x = x_ref[0]
    ms = jnp.mean(x * x, axis=-1, keepdims=True)
    hb = (x * lax.rsqrt(ms + RMS_EPS) * g_ref[...]).astype(BF16)

    pt = _dot_nt(wt_ref[...], hb)
    cq = cq_ref[...]
    sq = sq_ref[...]
    ones_rows = jnp.where(lax.broadcasted_iota(jnp.int32, (BF16_ROWS, rt), 0) == 0,
                          1.0, 0.0).astype(BF16)
    for h in range(MOBA_HEADS):
        blk = pt[h * hd:(h + 1) * hd]
        swapped = jnp.concatenate([blk[HALF:], blk[:HALF]], axis=0)
        qt_ref[0, 0, h * hd:(h + 1) * hd, :] = (blk * cq + swapped * sq).astype(BF16)
        v_rows = pt[MOBA_WIDTH + h * hd:MOBA_WIDTH + (h + 1) * hd]
        vt_ref[0, 0, h * VX:h * VX + hd, :] = v_rows.astype(BF16)
        vt_ref[0, 0, h * VX + hd:(h + 1) * VX, :] = ones_rows
    smgt_ref[0, 0] = _silu(pt[2 * MOBA_WIDTH:3 * MOBA_WIDTH]).astype(BF16)

    pn = _dot(hb, wn_ref[...])
    ck = ck_ref[...]
    sk = sk_ref[...]
    lane = lax.broadcasted_iota(jnp.int32, (rt, KX), 1)
    block_onehot = jnp.where(lane == KX_OH + t, 1.0, 0.0)
    for h in range(MOBA_HEADS):
        blk = pn[:, _C_K + h * KX:_C_K + (h + 1) * KX]
        kr = blk * ck + pltpu.roll(blk, KX_HI, axis=1) * sk
        kmean_ref[0, 0, :, h * KX:(h + 1) * KX] = jnp.mean(kr, axis=0, keepdims=True)
        k_ref[0, :, h * KX:(h + 1) * KX] = (kr + block_onehot).astype(BF16)

    gq_ref[0] = (pn[:, _C_GQ:_C_GK] * (GLA_HEAD_K ** -0.5)).astype(BF16)
    gk_ref[0] = pn[:, _C_GK:_C_GV].astype(BF16)
    gv_ref[0] = pn[:, _C_GV:_C_GG].astype(BF16)
    sgg_ref[0] = _silu(pn[:, _C_GG:_C_FG]).astype(BF16)

    fg = pn[:, _C_FG:_C_GA].astype(BF16)
    z = _dot(fg, wfg2_ref[...]) + bfg_ref[...]
    log_sig = jnp.minimum(z, 0.0) - jnp.log1p(jnp.exp(-jnp.abs(z)))
    la_ref[0] = log_sig * (1.0 / GLA_GATE_NORM)

    sga_ref[0] = _sigmoid(pn[:, _C_GA:_C_GB] + bm_ref[0:1, :]).astype(BF16)
    sgb_ref[0] = _sigmoid(pn[:, _C_GB:_C_END] + bm_ref[1:2, :]).astype(BF16)


def _in_proj(x, g, wn, wt, cq, sq, ck, sk, wfg2, bfg, bm):
    B, S, D = x.shape
    nt = S // ROW_TILE
    rt = ROW_TILE
    const = lambda shape: pl.BlockSpec(shape, lambda b, t: (0,) * len(shape))
    t_blocked = lambda r: pl.BlockSpec((1, 1, r, rt), lambda b, t: (b, t, 0, 0))
    rows = lambda w: pl.BlockSpec((1, rt, w), lambda b, t: (b, t, 0))
    t_shape = lambda r: jax.ShapeDtypeStruct((B, nt, r, rt), BF16)
    rshape = lambda w, dt=BF16: jax.ShapeDtypeStruct((B, S, w), dt)
    return pl.pallas_call(
        _in_proj_kernel,
        grid=(B, nt),
        in_specs=[
            rows(D), const((1, D)), const(wn.shape), const(wt.shape),
            pl.BlockSpec((MOBA_HEAD_DIM, rt), lambda b, t: (0, t)),
            pl.BlockSpec((MOBA_HEAD_DIM, rt), lambda b, t: (0, t)),
            pl.BlockSpec((rt, KX), lambda b, t: (t, 0)),
            pl.BlockSpec((rt, KX), lambda b, t: (t, 0)),
            const(wfg2.shape), const((1, GLA_DK)), const((2, D)),
        ],
        out_specs=[
            t_blocked(MOBA_WIDTH), t_blocked(MOBA_HEADS * VX), t_blocked(MOBA_WIDTH),
            rows(MOBA_HEADS * KX),
            pl.BlockSpec((1, 1, 1, MOBA_HEADS * KX), lambda b, t: (b, t, 0, 0)),
            rows(GLA_DK), rows(GLA_DK), rows(GLA_DV), rows(GLA_DV), rows(GLA_DK),
            rows(D), rows(D),
        ],
        out_shape=[
            t_shape(MOBA_WIDTH), t_shape(MOBA_HEADS * VX), t_shape(MOBA_WIDTH),
            rshape(MOBA_HEADS * KX),
            jax.ShapeDtypeStruct((B, nt, 1, MOBA_HEADS * KX), F32),
            rshape(GLA_DK), rshape(GLA_DK), rshape(GLA_DV), rshape(GLA_DV),
            rshape(GLA_DK, F32), rshape(D), rshape(D),
        ],
        compiler_params=pltpu.CompilerParams(
            dimension_semantics=("arbitrary", "arbitrary"),
            vmem_limit_bytes=VMEM_LIMIT),
        name="in_proj",
    )(x, g, wn, wt, cq, sq, ck, sk, wfg2, bfg, bm)


def _moba_kernel(qt_ref, k_ref, vt_ref, smgt_ref, kmh_ref, kml_ref, o_ref,
                 qx_ref, s_ref, smax_ref, m_ref, acc_ref):
    nb = MOBA_BLOCK
    hd = MOBA_HEAD_DIM
    n_sub = qt_ref.shape[1]
    qw = n_sub * nb
    n_blocks = k_ref.shape[1] // nb
    assert n_blocks <= N_BLOCKS_MAX
    first_own = pl.program_id(1) * n_sub

    blk_id = lax.broadcasted_iota(jnp.int32, (n_blocks, nb), 0)
    qx_ref[...] = jnp.zeros_like(qx_ref)
    for j in range(n_sub):
        own = first_own + j
        cols = slice(j * nb, (j + 1) * nb)
        for p in range(N_PAIRS):
            qt = qt_ref[0, j, p * PAIR:(p + 1) * PAIR, :]
            gates = _dot(kmh_ref[0, p], qt) + _dot(kml_ref[0, p], qt)
            for hh in range(2):
                h = 2 * p + hh
                past = blk_id < own
                gh = jnp.where(past, gates[hh * n_blocks:(hh + 1) * n_blocks], NEG_INF)
                picked = jnp.zeros((n_blocks, nb), jnp.bool_)
                for _ in range(MOBA_TOPK):
                    best = jnp.max(gh, axis=0, keepdims=True)
                    first = jnp.min(jnp.where(gh == best, blk_id, n_blocks), axis=0, keepdims=True)
                    hit = blk_id == first
                    picked = picked | hit
                    gh = jnp.where(hit, -jnp.inf, gh)
                keep = (blk_id == own) | (past & picked)
                qx_ref[h, KX_LO:KX_LO + HALF, cols] = qt[hh * hd:hh * hd + HALF]
                qx_ref[h, KX_OH:KX_OH + n_blocks, cols] = (
                    jnp.where(keep, 0.0, NEG_INF).astype(BF16))
                qx_ref[h, KX_HI:KX_HI + HALF, cols] = qt[hh * hd + HALF:(hh + 1) * hd]

    def produce_scores(n, h):
        kb = k_ref[0, pl.ds(pl.multiple_of(n * nb, nb), nb), h * KX:(h + 1) * KX]
        s = _dot(kb, qx_ref[h])
        s_ref[h] = s
        smax_ref[h] = jnp.max(s, axis=0, keepdims=True)

    def attend(n, diagonal, prefetch):
        if diagonal:
            key_pos = n * nb + lax.broadcasted_iota(jnp.int32, (nb, qw), 0)
            qry_pos = first_own * nb + lax.broadcasted_iota(jnp.int32, (nb, qw), 1)
            causal = key_pos <= qry_pos
        jobs = [(n, h) for h in range(QK_AHEAD, MOBA_HEADS)]
        if prefetch:
            jobs += [(n + 1, h) for h in range(QK_AHEAD)]

        def issue():
            if jobs:
                produce_scores(*jobs.pop(0))

        issue()
        issue()
        for h in range(MOBA_HEADS):
            s = s_ref[h]
            if diagonal:
                s = jnp.where(causal, s, NEG_INF)
                smax = jnp.max(s, axis=0, keepdims=True)
            else:
                smax = smax_ref[h]
            vt = vt_ref[0, n, h * VX:(h + 1) * VX, :]
            m_old = m_ref[h]
            m_new = jnp.maximum(m_old, smax)
            m_ref[h] = m_new
            acc_ref[h] = (jnp.exp2(m_old - m_new) * acc_ref[h]
                          + _dot(vt, jnp.exp2(s - m_new).astype(BF16)))
            issue()

    m_ref[...] = jnp.full(m_ref.shape, NEG_INF, F32)
    acc_ref[...] = jnp.zeros_like(acc_ref)
    for h in range(QK_AHEAD):
        produce_scores(0, h)

    def body(n, c):
        attend(n, False, True)
        return c

    lax.fori_loop(0, first_own, body, 0)
    for j in range(n_sub):
        attend(first_own + j, True, j + 1 < n_sub)
    for h in range(MOBA_HEADS):
        o = acc_ref[h, 0:hd, :] / acc_ref[h, hd:hd + 1, :]
        for j in range(n_sub):
            gate = smgt_ref[0, j, h * hd:(h + 1) * hd, :].astype(F32)
            o_ref[0, j, h * hd:(h + 1) * hd, :] = (o[:, j * nb:(j + 1) * nb] * gate).astype(BF16)


def _moba(qt, k, vt, smgt, kmh, kml):
    B, nt, W, rt = qt.shape
    S = k.shape[1]
    n_sub = MOBA_Q_TILE // rt
    qw = MOBA_Q_TILE
    tile = pl.BlockSpec((1, n_sub, W, rt), lambda b, t: (b, t, 0, 0))
    km = pl.BlockSpec((1, N_PAIRS, 2 * nt, PAIR), lambda b, t: (b, 0, 0, 0))
    return pl.pallas_call(
        _moba_kernel,
        grid=(B, nt // n_sub),
        in_specs=[
            tile,
            pl.BlockSpec((1, S, MOBA_HEADS * KX), lambda b, t: (b, 0, 0)),
            pl.BlockSpec((1, nt, MOBA_HEADS * VX, rt), lambda b, t: (b, 0, 0, 0)),
            tile, km, km,
        ],
        out_specs=tile,
        out_shape=jax.ShapeDtypeStruct((B, nt, W, rt), BF16),
        scratch_shapes=[
            pltpu.VMEM((MOBA_HEADS, KX, qw), BF16),
            pltpu.VMEM((MOBA_HEADS, MOBA_BLOCK, qw), F32),
            pltpu.VMEM((MOBA_HEADS, 1, qw), F32),
            pltpu.VMEM((MOBA_HEADS, 1, qw), F32),
            pltpu.VMEM((MOBA_HEADS, VX, qw), F32),
        ],
        compiler_params=pltpu.CompilerParams(
            dimension_semantics=("arbitrary", "arbitrary"),
            vmem_limit_bytes=VMEM_LIMIT),
        name="moba",
    )(qt, k, vt, smgt, kmh, kml)


def _gla_kernel(q_ref, k_ref, v_ref, la_ref, sgg_ref, gn_ref, o_ref, st_ref):
    C = GLA_CHUNK
    hk, hv = GLA_HEAD_K, GLA_HEAD_V

    @pl.when(pl.program_id(1) == 0)
    def _():
        st_ref[...] = jnp.zeros_like(st_ref)

    nc = GLA_STEP_ROWS // C
    H = GLA_HEADS
    tril_b = (lax.broadcasted_iota(jnp.int32, (C, C), 0)
              >= lax.broadcasted_iota(jnp.int32, (C, C), 1)).astype(BF16)
    tril_stack = (lax.broadcasted_iota(jnp.int32, (H * C, C), 0) % C
                  >= lax.broadcasted_iota(jnp.int32, (H * C, C), 1))
    lane_head = lax.broadcasted_iota(jnp.int32, (C, GLA_DK), 1) // hk
    zero_b = jnp.zeros((C, GLA_DK), BF16)
    gn = gn_ref[...]

    def stack_heads(x):
        return jnp.concatenate([jnp.where(lane_head == h, x, zero_b) for h in range(H)], axis=0)

    chunk = lambda c: slice(c * C, (c + 1) * C)
    g = la_ref[0]
    g_hi = g.astype(BF16)
    g_lo = (g - g_hi.astype(F32)).astype(BF16)
    b = [_dot(tril_b, g_hi[chunk(c)]) + _dot(tril_b, g_lo[chunk(c)]) for c in range(nc)]

    qe_s, qb_s, ke_b, kl_s, decay = [], [], [], [], []
    for c in range(nc):
        b_mid = b[c][C // 2 - 1:C // 2]
        b_last = b[c][C - 1:C]
        qe = q_ref[0, chunk(c), :].astype(F32) * jnp.exp(b[c] - b_mid)
        ke = k_ref[0, chunk(c), :].astype(F32) * jnp.exp(b_mid - b[c])
        qe_s.append(stack_heads(qe.astype(BF16)))
        qb_s.append(stack_heads((qe * jnp.exp(b_mid)).astype(BF16)))
        kl_s.append(stack_heads((ke * jnp.exp(b_last - b_mid)).astype(BF16)))
        ke_b.append(ke.astype(BF16))
        decay.append(jnp.exp(b_last))

    attn = [jnp.where(tril_stack, _dot_nt(qe_s[c], ke_b[c]), 0.0).astype(BF16) for c in range(nc)]
    intra = [jnp.concatenate(
        [_dot(attn[c][h * C:(h + 1) * C], v_ref[0, chunk(c), h * hv:(h + 1) * hv])
         for h in range(H)], axis=1) for c in range(nc)]

    v_s = [jnp.concatenate([v_ref[0, chunk(c), h * hv:(h + 1) * hv] for h in range(H)], axis=0)
           for c in range(nc)]
    upd = [_dot_tn(v_s[c], kl_s[c]) for c in range(nc)]

    st = st_ref[...]
    inter = []
    for c in range(nc):
        o_s = _dot_nt(qb_s[c], st.astype(BF16))
        inter.append(jnp.concatenate([o_s[h * C:(h + 1) * C] for h in range(H)], axis=1))
        st = st * decay[c] + upd[c]
    st_ref[...] = st

    for c in range(nc):
        o = inter[c] + intra[c]
        for h in range(H):
            oh = o[:, h * hv:(h + 1) * hv]
            ms = jnp.mean(oh * oh, axis=-1, keepdims=True)
            y = oh * lax.rsqrt(ms + RMS_EPS) * gn
            gate = sgg_ref[0, chunk(c), h * hv:(h + 1) * hv].astype(F32)
            o_ref[0, chunk(c), h * hv:(h + 1) * hv] = (y * gate).astype(BF16)


def _gla(gq, gk, gv, la, sgg, gn):
    B, S, _ = gq.shape
    rows = lambda w: pl.BlockSpec((1, GLA_STEP_ROWS, w), lambda b, t: (b, t, 0))
    return pl.pallas_call(
        _gla_kernel,
        grid=(B, S // GLA_STEP_ROWS),
        in_specs=[rows(GLA_DK), rows(GLA_DK), rows(GLA_DV), rows(GLA_DK), rows(GLA_DV),
                  pl.BlockSpec((1, GLA_HEAD_V), lambda b, t: (0, 0))],
        out_specs=rows(GLA_DV),
        out_shape=jax.ShapeDtypeStruct((B, S, GLA_DV), BF16),
        scratch_shapes=[pltpu.VMEM((GLA_HEAD_V, GLA_DK), F32)],
        compiler_params=pltpu.CompilerParams(
            dimension_semantics=("arbitrary", "arbitrary"),
            vmem_limit_bytes=VMEM_LIMIT),
        name="gla",
    )(gq, gk, gv, la, sgg, gn)


def _out_kernel(oat_ref, ob_ref, sga_ref, sgb_ref, x_ref, wpa_ref, wpb_ref, wo_ref, gf_ref,
                y_ref, *, final_norm):
    n_sub, rt = oat_ref.shape[1], oat_ref.shape[3]
    sub = lambda j: slice(j * rt, (j + 1) * rt)
    ya = [_dot_tn(oat_ref[0, j], wpa_ref[...]) for j in range(n_sub)]
    yb = [_dot(ob_ref[0, sub(j), :], wpb_ref[...]) for j in range(n_sub)]
    merged = [(sga_ref[0, sub(j), :].astype(F32) * ya[j]
               + sgb_ref[0, sub(j), :].astype(F32) * yb[j]).astype(BF16) for j in range(n_sub)]
    proj = [_dot(merged[j], wo_ref[...]) for j in range(n_sub)]
    for j in range(n_sub):
        r = x_ref[0, sub(j), :] + proj[j]
        if final_norm:
            ms = jnp.mean(r * r, axis=-1, keepdims=True)
            r = r * lax.rsqrt(ms + RMS_EPS) * gf_ref[...]
        y_ref[0, sub(j), :] = r


def _out_stage(oat, ob, sga, sgb, x, wpa, wpb, wo, gf, final_norm):
    B, S, D = x.shape
    n_sub = OUT_STEP_ROWS // ROW_TILE
    nt = S // OUT_STEP_ROWS
    rt = OUT_STEP_ROWS
    const = lambda shape: pl.BlockSpec(shape, lambda b, t: (0,) * len(shape))
    rows = lambda w: pl.BlockSpec((1, rt, w), lambda b, t: (b, t, 0))
    return pl.pallas_call(
        functools.partial(_out_kernel, final_norm=final_norm),
        grid=(B, nt),
        in_specs=[
            pl.BlockSpec((1, n_sub, MOBA_WIDTH, ROW_TILE), lambda b, t: (b, t, 0, 0)),
            rows(GLA_DV), rows(D), rows(D), rows(D),
            const(wpa.shape), const(wpb.shape), const(wo.shape), const((1, D)),
        ],
        out_specs=rows(D),
        out_shape=jax.ShapeDtypeStruct((B, S, D), F32),
        compiler_params=pltpu.CompilerParams(
            dimension_semantics=("arbitrary", "arbitrary"),
            vmem_limit_bytes=VMEM_LIMIT),
        name="out_stage",
    )(oat, ob, sga, sgb, x, wpa, wpb, wo, gf)


def _rope_tables(S):
    inv_freq = 1.0 / (ROPE_THETA ** (jnp.arange(HALF, dtype=F32) / HALF))
    ang = jnp.arange(S, dtype=F32)[:, None] * inv_freq[None, :]
    return jnp.cos(ang), jnp.sin(ang)


def _extend_key_columns(wk):
    D = wk.shape[0]
    w = wk.reshape(D, MOBA_HEADS, 2, HALF)
    out = jnp.zeros((D, MOBA_HEADS, KX), wk.dtype)
    out = out.at[:, :, KX_LO:KX_LO + HALF].set(w[:, :, 0])
    out = out.at[:, :, KX_HI:KX_HI + HALF].set(w[:, :, 1])
    return out.reshape(D, MOBA_HEADS * KX)


def kernel(x, norm_in_g, w_in, b_merge, w_gla_fg2, b_gla_fg, gla_norm_g,
           w_proj_a, w_proj_b, w_out, norm_f_g):
    B, S, D = x.shape
    depth = w_in.shape[0]
    nt = S // MOBA_BLOCK
    cos, sin = _rope_tables(S)
    q_scale = MOBA_HEAD_DIM ** -0.5 * math.log2(math.e)
    cq = (jnp.concatenate([cos, cos], axis=1) * q_scale).T
    sq = (jnp.concatenate([-sin, sin], axis=1) * q_scale).T
    ck = jnp.tile(cos, (1, KX // HALF))
    sk = jnp.tile(sin, (1, KX // HALF)) * jnp.where(jnp.arange(KX) < KX_HI, -1.0, 1.0)[None, :]
    pair_mask = jnp.asarray(
        np.arange(PAIR)[None, :] // MOBA_HEAD_DIM == np.arange(2)[:, None], F32)

    for layer in range(depth):
        w = w_in[layer]
        o = np.cumsum([0, MOBA_WIDTH, MOBA_WIDTH, MOBA_WIDTH, MOBA_WIDTH, GLA_DK, GLA_DK,
                       GLA_DV, GLA_DV, GLA_GATE_RANK, D_MODEL, D_MODEL])
        wq, wk, wv, wmg, wgq, wgk, wgv, wgg, wfg, wga, wgb = [
            w[:, o[i]:o[i + 1]] for i in range(11)]
        wfg_p = jnp.pad(wfg, ((0, 0), (0, FG_PAD - GLA_GATE_RANK)))
        wn = jnp.concatenate([_extend_key_columns(wk), wgq, wgk, wgv, wgg, wfg_p, wga, wgb],
                             axis=1).astype(BF16)
        wt = jnp.concatenate([wq, wv, wmg], axis=1).T.astype(BF16)
        wfg2 = jnp.pad(w_gla_fg2[layer], ((0, FG_PAD - GLA_GATE_RANK), (0, 0))).astype(BF16)

        (qt, vt, smgt, k, kmean, gq, gk, gv, sgg, la, sga, sgb) = _in_proj(
            x, norm_in_g[layer][None, :], wn, wt, cq, sq, ck, sk, wfg2,
            b_gla_fg[layer][None, :], b_merge[layer])

        km = kmean.reshape(B, nt, MOBA_HEADS, KX)
        km = jnp.concatenate([km[..., KX_LO:KX_LO + HALF], km[..., KX_HI:KX_HI + HALF]], axis=-1)
        km = km.reshape(B, nt, N_PAIRS, PAIR).transpose(0, 2, 1, 3)
        km = (km[:, :, None, :, :] * pair_mask[None, None, :, None, :]).reshape(
            B, N_PAIRS, 2 * nt, PAIR)
        km_hi = km.astype(BF16)
        km_lo = (km - km_hi.astype(F32)).astype(BF16)

        oat = _moba(qt, k, vt, smgt, km_hi, km_lo)
        ob = _gla(gq, gk, gv, la, sgg, gla_norm_g[layer][None, :])
        x = _out_stage(oat, ob, sga, sgb, x, w_proj_a[layer].astype(BF16),
                       w_proj_b[layer].astype(BF16), w_out[layer].astype(BF16),
                       norm_f_g[None, :], final_norm=(layer == depth - 1))
    return x
```

```python
import functools
import math

import jax
import jax.numpy as jnp
import numpy as np
from jax import lax
from jax.experimental import pallas as pl
from jax.experimental.pallas import tpu as pltpu

F32 = jnp.float32
BF16 = jnp.bfloat16

D_MODEL = 1024
MOBA_HEADS = 8
MOBA_HEAD_DIM = 64
MOBA_WIDTH = MOBA_HEADS * MOBA_HEAD_DIM
MOBA_BLOCK = 256
MOBA_TOPK = 3
ROPE_THETA = 10000.0
GLA_HEADS = 4
GLA_DK = 256
GLA_DV = 512
GLA_HEAD_K = 64
GLA_HEAD_V = 128
GLA_GATE_RANK = 16
GLA_GATE_NORM = 16.0
GLA_CHUNK = 64
RMS_EPS = 1e-6
NEG_INF = -1e30

LANES = 128
BF16_ROWS = 16
ROW_TILE = MOBA_BLOCK
PAIR = 2 * MOBA_HEAD_DIM
N_PAIRS = MOBA_HEADS // 2
GLA_STEP_ROWS = 512
OUT_STEP_ROWS = 512
QK_AHEAD = 4
MOBA_Q_TILE = 512
FG_PAD = LANES
VMEM_LIMIT = 56 * 1024 * 1024

HALF = MOBA_HEAD_DIM // 2
KX = LANES
KX_LO, KX_OH, KX_HI = 0, HALF, 2 * HALF
N_BLOCKS_MAX = BF16_ROWS
VX = MOBA_HEAD_DIM + BF16_ROWS

_C_K = 0
_C_GQ = _C_K + MOBA_HEADS * KX
_C_GK = _C_GQ + GLA_DK
_C_GV = _C_GK + GLA_DK
_C_GG = _C_GV + GLA_DV
_C_FG = _C_GG + GLA_DV
_C_GA = _C_FG + FG_PAD
_C_GB = _C_GA + D_MODEL
_C_END = _C_GB + D_MODEL


def _dot(a, b):
    return jnp.dot(a, b, preferred_element_type=F32)


def _dot_nt(a, b):
    return lax.dot_general(a, b, (((1,), (1,)), ((), ())), preferred_element_type=F32)


def _dot_tn(a, b):
    return lax.dot_general(a, b, (((0,), (0,)), ((), ())), preferred_element_type=F32)


def _sigmoid(x):
    return 1.0 / (1.0 + jnp.exp(-x))


def _silu(x):
    return x * _sigmoid(x)


def _in_proj_kernel(x_ref, g_ref, wn_ref, wt_ref, cq_ref, sq_ref, ck_ref, sk_ref,
                    wfg2_ref, bfg_ref, bm_ref,
                    qt_ref, vt_ref, smgt_ref, k_ref, kmean_ref, gq_ref, gk_ref, gv_ref,
                    sgg_ref, la_ref, sga_ref, sgb_ref):
    t = pl.program_id(1)
    rt = x_ref.shape[1]
    hd = MOBA_HEAD_DIM
    x = x_ref[0]
    ms = jnp.mean(x * x, axis=-1, keepdims=True)
    hb = (x * lax.rsqrt(ms + RMS_EPS) * g_ref[...]).astype(BF16)

    pt = _dot_nt(wt_ref[...], hb)
    cq = cq_ref[...]
    sq = sq_ref[...]
    ones_rows = jnp.where(lax.broadcasted_iota(jnp.int32, (BF16_ROWS, rt), 0) == 0,
                          1.0, 0.0).astype(BF16)
    for h in range(MOBA_HEADS):
        blk = pt[h * hd:(h + 1) * hd]
        swapped = jnp.concatenate([blk[HALF:], blk[:HALF]], axis=0)
        qt_ref[0, 0, h * hd:(h + 1) * hd, :] = (blk * cq + swapped * sq).astype(BF16)
        v_rows = pt[MOBA_WIDTH + h * hd:MOBA_WIDTH + (h + 1) * hd]
        vt_ref[0, 0, h * VX:h * VX + hd, :] = v_rows.astype(BF16)
        vt_ref[0, 0, h * VX + hd:(h + 1) * VX, :] = ones_rows
    smgt_ref[0, 0] = _silu(pt[2 * MOBA_WIDTH:3 * MOBA_WIDTH]).astype(BF16)

    pn = _dot(hb, wn_ref[...])
    ck = ck_ref[...]
    sk = sk_ref[...]
    lane = lax.broadcasted_iota(jnp.int32, (rt, KX), 1)
    block_onehot = jnp.where(lane == KX_OH + t, 1.0, 0.0)
    for h in range(MOBA_HEADS):
        blk = pn[:, _C_K + h * KX:_C_K + (h + 1) * KX]
        kr = blk * ck + pltpu.roll(blk, KX_HI, axis=1) * sk
        kmean_ref[0, 0, :, h * KX:(h + 1) * KX] = jnp.mean(kr, axis=0, keepdims=True)
        k_ref[0, :, h * KX:(h + 1) * KX] = (kr + block_onehot).astype(BF16)

    gq_ref[0] = (pn[:, _C_GQ:_C_GK] * (GLA_HEAD_K ** -0.5)).astype(BF16)
    gk_ref[0] = pn[:, _C_GK:_C_GV].astype(BF16)
    gv_ref[0] = pn[:, _C_GV:_C_GG].astype(BF16)
    sgg_ref[0] = _silu(pn[:, _C_GG:_C_FG]).astype(BF16)

    fg = pn[:, _C_FG:_C_GA].astype(BF16)
    z = _dot(fg, wfg2_ref[...]) + bfg_ref[...]
    log_sig = jnp.minimum(z, 0.0) - jnp.log1p(jnp.exp(-jnp.abs(z)))
    la_ref[0] = log_sig * (1.0 / GLA_GATE_NORM)

    sga_ref[0] = _sigmoid(pn[:, _C_GA:_C_GB] + bm_ref[0:1, :]).astype(BF16)
    sgb_ref[0] = _sigmoid(pn[:, _C_GB:_C_END] + bm_ref[1:2, :]).astype(BF16)


def _in_proj(x, g, wn, wt, cq, sq, ck, sk, wfg2, bfg, bm):
    B, S, D = x.shape
    nt = S // ROW_TILE
    rt = ROW_TILE
    const = lambda shape: pl.BlockSpec(shape, lambda b, t: (0,) * len(shape))
    t_blocked = lambda r: pl.BlockSpec((1, 1, r, rt), lambda b, t: (b, t, 0, 0))
    rows = lambda w: pl.BlockSpec((1, rt, w), lambda b, t: (b, t, 0))
    t_shape = lambda r: jax.ShapeDtypeStruct((B, nt, r, rt), BF16)
    rshape = lambda w, dt=BF16: jax.ShapeDtypeStruct((B, S, w), dt)
    return pl.pallas_call(
        _in_proj_kernel,
        grid=(B, nt),
        in_specs=[
            rows(D), const((1, D)), const(wn.shape), const(wt.shape),
            pl.BlockSpec((MOBA_HEAD_DIM, rt), lambda b, t: (0, t)),
            pl.BlockSpec((MOBA_HEAD_DIM, rt), lambda b, t: (0, t)),
            pl.BlockSpec((rt, KX), lambda b, t: (t, 0)),
            pl.BlockSpec((rt, KX), lambda b, t: (t, 0)),
            const(wfg2.shape), const((1, GLA_DK)), const((2, D)),
        ],
        out_specs=[
            t_blocked(MOBA_WIDTH), t_blocked(MOBA_HEADS * VX), t_blocked(MOBA_WIDTH),
            rows(MOBA_HEADS * KX),
            pl.BlockSpec((1, 1, 1, MOBA_HEADS * KX), lambda b, t: (b, t, 0, 0)),
            rows(GLA_DK), rows(GLA_DK), rows(GLA_DV), rows(GLA_DV), rows(GLA_DK),
            rows(D), rows(D),
        ],
        out_shape=[
            t_shape(MOBA_WIDTH), t_shape(MOBA_HEADS * VX), t_shape(MOBA_WIDTH),
            rshape(MOBA_HEADS * KX),
            jax.ShapeDtypeStruct((B, nt, 1, MOBA_HEADS * KX), F32),
            rshape(GLA_DK), rshape(GLA_DK), rshape(GLA_DV), rshape(GLA_DV),
            rshape(GLA_DK, F32), rshape(D), rshape(D),
        ],
        compiler_params=pltpu.CompilerParams(
            dimension_semantics=("arbitrary", "arbitrary"),
            vmem_limit_bytes=VMEM_LIMIT),
        name="in_proj",
    )(x, g, wn, wt, cq, sq, ck, sk, wfg2, bfg, bm)


def _moba_kernel(qt_ref, k_ref, vt_ref, smgt_ref, kmh_ref, kml_ref, o_ref,
                 qx_ref, s_ref, smax_ref, m_ref, acc_ref):
    nb = MOBA_BLOCK
    hd = MOBA_HEAD_DIM
    n_sub = qt_ref.shape[1]
    qw = n_sub * nb
    n_blocks = k_ref.shape[1] // nb
    assert n_blocks <= N_BLOCKS_MAX
    first_own = pl.program_id(1) * n_sub

    blk_id = lax.broadcasted_iota(jnp.int32, (n_blocks, nb), 0).astype(F32)
    qx_ref[...] = jnp.zeros_like(qx_ref)

    def select_blocks(p):
        for j in range(n_sub):
            own = (first_own + j).astype(F32)
            cols = slice(j * nb, (j + 1) * nb)
            qt = qt_ref[0, j, p * PAIR:(p + 1) * PAIR, :]
            gates = _dot(kmh_ref[0, p], qt) + _dot(kml_ref[0, p], qt)
            for hh in range(2):
                h = 2 * p + hh
                past = blk_id < own
                gh = jnp.where(past, gates[hh * n_blocks:(hh + 1) * n_blocks], NEG_INF)
                picked = jnp.zeros((n_blocks, nb), jnp.bool_)
                for _ in range(MOBA_TOPK):
                    best = jnp.max(gh, axis=0, keepdims=True)
                    first = jnp.min(jnp.where(gh == best, blk_id, n_blocks), axis=0, keepdims=True)
                    hit = blk_id == first
                    picked = picked | hit
                    gh = jnp.where(hit, -jnp.inf, gh)
                keep = (blk_id == own) | (past & picked)
                qx_ref[h, KX_LO:KX_LO + HALF, cols] = qt[hh * hd:hh * hd + HALF]
                qx_ref[h, KX_OH:KX_OH + n_blocks, cols] = (
                    jnp.where(keep, 0.0, NEG_INF).astype(BF16))
                qx_ref[h, KX_HI:KX_HI + HALF, cols] = qt[hh * hd + HALF:(hh + 1) * hd]

    def produce_scores(n, h):
        kb = k_ref[0, pl.ds(pl.multiple_of(n * nb, nb), nb), h * KX:(h + 1) * KX]
        s = _dot(kb, qx_ref[h])
        s_ref[h] = s
        smax_ref[h] = jnp.max(s, axis=0, keepdims=True)

    causal = (lax.broadcasted_iota(jnp.int32, (nb, nb), 0)
              <= lax.broadcasted_iota(jnp.int32, (nb, nb), 1))

    def attend(n, diagonal, prefetch):
        jobs = [(n, h) for h in range(QK_AHEAD, MOBA_HEADS)]
        if prefetch:
            jobs += [(n + 1, h) for h in range(QK_AHEAD)]

        def issue():
            if jobs:
                produce_scores(*jobs.pop(0))

        issue()
        issue()
        for h in range(MOBA_HEADS):
            s = s_ref[h]
            smax = smax_ref[h]
            if diagonal is not None:
                own_cols = slice(diagonal * nb, (diagonal + 1) * nb)
                tri = jnp.where(causal, s[:, own_cols], NEG_INF)
                parts = [s[:, :own_cols.start], tri, s[:, own_cols.stop:]]
                maxes = [smax[:, :own_cols.start], jnp.max(tri, axis=0, keepdims=True),
                         smax[:, own_cols.stop:]]
                s = jnp.concatenate([x for x in parts if x.shape[1]], axis=1)
                smax = jnp.concatenate([x for x in maxes if x.shape[1]], axis=1)
            vt = vt_ref[0, n, h * VX:(h + 1) * VX, :]
            m_old = m_ref[h]
            m_new = jnp.maximum(m_old, smax)
            m_ref[h] = m_new
            acc_ref[h] = (jnp.exp2(m_old - m_new) * acc_ref[h]
                          + _dot(vt, jnp.exp2(s - m_new).astype(BF16)))
            issue()

    m_ref[...] = jnp.full(m_ref.shape, NEG_INF, F32)
    acc_ref[...] = jnp.zeros_like(acc_ref)
    for p in range(N_PAIRS):
        select_blocks(p)
        if 2 * p + 2 == QK_AHEAD:
            for h in range(QK_AHEAD):
                produce_scores(0, h)

    def body(n, c):
        attend(n, None, True)
        return c

    lax.fori_loop(0, first_own, body, 0)
    for j in range(n_sub):
        attend(first_own + j, j, j + 1 < n_sub)
    for h in range(MOBA_HEADS):
        o = acc_ref[h, 0:hd, :] / acc_ref[h, hd:hd + 1, :]
        for j in range(n_sub):
            gate = smgt_ref[0, j, h * hd:(h + 1) * hd, :].astype(F32)
            o_ref[0, j, h * hd:(h + 1) * hd, :] = (o[:, j * nb:(j + 1) * nb] * gate).astype(BF16)


def _moba(qt, k, vt, smgt, kmh, kml):
    B, nt, W, rt = qt.shape
    S = k.shape[1]
    n_sub = MOBA_Q_TILE // rt
    qw = MOBA_Q_TILE
    tile = pl.BlockSpec((1, n_sub, W, rt), lambda b, t: (b, t, 0, 0))
    km = pl.BlockSpec((1, N_PAIRS, 2 * nt, PAIR), lambda b, t: (b, 0, 0, 0))
    return pl.pallas_call(
        _moba_kernel,
        grid=(B, nt // n_sub),
        in_specs=[
            tile,
            pl.BlockSpec((1, S, MOBA_HEADS * KX), lambda b, t: (b, 0, 0)),
            pl.BlockSpec((1, nt, MOBA_HEADS * VX, rt), lambda b, t: (b, 0, 0, 0)),
            tile, km, km,
        ],
        out_specs=tile,
        out_shape=jax.ShapeDtypeStruct((B, nt, W, rt), BF16),
        scratch_shapes=[
            pltpu.VMEM((MOBA_HEADS, KX, qw), BF16),
            pltpu.VMEM((MOBA_HEADS, MOBA_BLOCK, qw), F32),
            pltpu.VMEM((MOBA_HEADS, 1, qw), F32),
            pltpu.VMEM((MOBA_HEADS, 1, qw), F32),
            pltpu.VMEM((MOBA_HEADS, VX, qw), F32),
        ],
        compiler_params=pltpu.CompilerParams(
            dimension_semantics=("arbitrary", "arbitrary"),
            vmem_limit_bytes=VMEM_LIMIT),
        name="moba",
    )(qt, k, vt, smgt, kmh, kml)


def _gla_kernel(q_ref, k_ref, v_ref, la_ref, sgg_ref, gn_ref, o_ref, st_ref):
    C = GLA_CHUNK
    hk, hv = GLA_HEAD_K, GLA_HEAD_V

    @pl.when(pl.program_id(1) == 0)
    def _():
        st_ref[...] = jnp.zeros_like(st_ref)

    nc = GLA_STEP_ROWS // C
    H = GLA_HEADS
    tril_b = (lax.broadcasted_iota(jnp.int32, (C, C), 0)
              >= lax.broadcasted_iota(jnp.int32, (C, C), 1)).astype(BF16)
    tril_stack = (lax.broadcasted_iota(jnp.int32, (H * C, C), 0) % C
                  >= lax.broadcasted_iota(jnp.int32, (H * C, C), 1))
    lane_head = lax.broadcasted_iota(jnp.int32, (C, GLA_DK), 1) // hk
    zero_b = jnp.zeros((C, GLA_DK), BF16)
    gn = gn_ref[...]

    def stack_heads(x):
        return jnp.concatenate([jnp.where(lane_head == h, x, zero_b) for h in range(H)], axis=0)

    chunk = lambda c: slice(c * C, (c + 1) * C)
    g = la_ref[0]
    g_hi = g.astype(BF16)
    g_lo = (g - g_hi.astype(F32)).astype(BF16)
    b = [_dot(tril_b, g_hi[chunk(c)]) + _dot(tril_b, g_lo[chunk(c)]) for c in range(nc)]

    qe_s, qb_s, ke_b, kl_s, decay = [], [], [], [], []
    for c in range(nc):
        b_mid = b[c][C // 2 - 1:C // 2]
        b_last = b[c][C - 1:C]
        qe = q_ref[0, chunk(c), :].astype(F32) * jnp.exp(b[c] - b_mid)
        ke = k_ref[0, chunk(c), :].astype(F32) * jnp.exp(b_mid - b[c])
        qe_s.append(stack_heads(qe.astype(BF16)))
        qb_s.append(stack_heads((qe * jnp.exp(b_mid)).astype(BF16)))
        kl_s.append(stack_heads((ke * jnp.exp(b_last - b_mid)).astype(BF16)))
        ke_b.append(ke.astype(BF16))
        decay.append(jnp.exp(b_last))

    attn = [jnp.where(tril_stack, _dot_nt(qe_s[c], ke_b[c]), 0.0).astype(BF16) for c in range(nc)]
    intra = [jnp.concatenate(
        [_dot(attn[c][h * C:(h + 1) * C], v_ref[0, chunk(c), h * hv:(h + 1) * hv])
         for h in range(H)], axis=1) for c in range(nc)]

    v_s = [jnp.concatenate([v_ref[0, chunk(c), h * hv:(h + 1) * hv] for h in range(H)], axis=0)
           for c in range(nc)]
    upd = [_dot_tn(v_s[c], kl_s[c]) for c in range(nc)]

    st = st_ref[...]
    inter = []
    for c in range(nc):
        o_s = _dot_nt(qb_s[c], st.astype(BF16))
        inter.append(jnp.concatenate([o_s[h * C:(h + 1) * C] for h in range(H)], axis=1))
        st = st * decay[c] + upd[c]
    st_ref[...] = st

    for c in range(nc):
        o = inter[c] + intra[c]
        for h in range(H):
            oh = o[:, h * hv:(h + 1) * hv]
            ms = jnp.mean(oh * oh, axis=-1, keepdims=True)
            y = oh * lax.rsqrt(ms + RMS_EPS) * gn
            gate = sgg_ref[0, chunk(c), h * hv:(h + 1) * hv].astype(F32)
            o_ref[0, chunk(c), h * hv:(h + 1) * hv] = (y * gate).astype(BF16)


def _gla(gq, gk, gv, la, sgg, gn):
    B, S, _ = gq.shape
    rows = lambda w: pl.BlockSpec((1, GLA_STEP_ROWS, w), lambda b, t: (b, t, 0))
    return pl.pallas_call(
        _gla_kernel,
        grid=(B, S // GLA_STEP_ROWS),
        in_specs=[rows(GLA_DK), rows(GLA_DK), rows(GLA_DV), rows(GLA_DK), rows(GLA_DV),
                  pl.BlockSpec((1, GLA_HEAD_V), lambda b, t: (0, 0))],
        out_specs=rows(GLA_DV),
        out_shape=jax.ShapeDtypeStruct((B, S, GLA_DV), BF16),
        scratch_shapes=[pltpu.VMEM((GLA_HEAD_V, GLA_DK), F32)],
        compiler_params=pltpu.CompilerParams(
            dimension_semantics=("arbitrary", "arbitrary"),
            vmem_limit_bytes=VMEM_LIMIT),
        name="gla",
    )(gq, gk, gv, la, sgg, gn)


def _out_kernel(oat_ref, ob_ref, sga_ref, sgb_ref, x_ref, wpa_ref, wpb_ref, wo_ref, gf_ref,
                y_ref, *, final_norm):
    n_sub, rt = oat_ref.shape[1], oat_ref.shape[3]
    sub = lambda j: slice(j * rt, (j + 1) * rt)
    ya = [_dot_tn(oat_ref[0, j], wpa_ref[...]) for j in range(n_sub)]
    yb = [_dot(ob_ref[0, sub(j), :], wpb_ref[...]) for j in range(n_sub)]
    merged = [(sga_ref[0, sub(j), :].astype(F32) * ya[j]
               + sgb_ref[0, sub(j), :].astype(F32) * yb[j]).astype(BF16) for j in range(n_sub)]
    proj = [_dot(merged[j], wo_ref[...]) for j in range(n_sub)]
    for j in range(n_sub):
        r = x_ref[0, sub(j), :] + proj[j]
        if final_norm:
            ms = jnp.mean(r * r, axis=-1, keepdims=True)
            r = r * lax.rsqrt(ms + RMS_EPS) * gf_ref[...]
        y_ref[0, sub(j), :] = r


def _out_stage(oat, ob, sga, sgb, x, wpa, wpb, wo, gf, final_norm):
    B, S, D = x.shape
    n_sub = OUT_STEP_ROWS // ROW_TILE
    nt = S // OUT_STEP_ROWS
    rt = OUT_STEP_ROWS
    const = lambda shape: pl.BlockSpec(shape, lambda b, t: (0,) * len(shape))
    rows = lambda w: pl.BlockSpec((1, rt, w), lambda b, t: (b, t, 0))
    return pl.pallas_call(
        functools.partial(_out_kernel, final_norm=final_norm),
        grid=(B, nt),
        in_specs=[
            pl.BlockSpec((1, n_sub, MOBA_WIDTH, ROW_TILE), lambda b, t: (b, t, 0, 0)),
            rows(GLA_DV), rows(D), rows(D), rows(D),
            const(wpa.shape), const(wpb.shape), const(wo.shape), const((1, D)),
        ],
        out_specs=rows(D),
        out_shape=jax.ShapeDtypeStruct((B, S, D), F32),
        compiler_params=pltpu.CompilerParams(
            dimension_semantics=("arbitrary", "arbitrary"),
            vmem_limit_bytes=VMEM_LIMIT),
        name="out_stage",
    )(oat, ob, sga, sgb, x, wpa, wpb, wo, gf)


def _rope_tables(S):
    inv_freq = 1.0 / (ROPE_THETA ** (jnp.arange(HALF, dtype=F32) / HALF))
    ang = jnp.arange(S, dtype=F32)[:, None] * inv_freq[None, :]
    return jnp.cos(ang), jnp.sin(ang)


def _extend_key_columns(wk):
    D = wk.shape[0]
    w = wk.reshape(D, MOBA_HEADS, 2, HALF)
    out = jnp.zeros((D, MOBA_HEADS, KX), wk.dtype)
    out = out.at[:, :, KX_LO:KX_LO + HALF].set(w[:, :, 0])
    out = out.at[:, :, KX_HI:KX_HI + HALF].set(w[:, :, 1])
    return out.reshape(D, MOBA_HEADS * KX)


def kernel(x, norm_in_g, w_in, b_merge, w_gla_fg2, b_gla_fg, gla_norm_g,
           w_proj_a, w_proj_b, w_out, norm_f_g):
    B, S, D = x.shape
    depth = w_in.shape[0]
    nt = S // MOBA_BLOCK
    cos, sin = _rope_tables(S)
    q_scale = MOBA_HEAD_DIM ** -0.5 * math.log2(math.e)
    cq = (jnp.concatenate([cos, cos], axis=1) * q_scale).T
    sq = (jnp.concatenate([-sin, sin], axis=1) * q_scale).T
    ck = jnp.tile(cos, (1, KX // HALF))
    sk = jnp.tile(sin, (1, KX // HALF)) * jnp.where(jnp.arange(KX) < KX_HI, -1.0, 1.0)[None, :]
    pair_mask = jnp.asarray(
        np.arange(PAIR)[None, :] // MOBA_HEAD_DIM == np.arange(2)[:, None], F32)

    for layer in range(depth):
        w = w_in[layer]
        o = np.cumsum([0, MOBA_WIDTH, MOBA_WIDTH, MOBA_WIDTH, MOBA_WIDTH, GLA_DK, GLA_DK,
                       GLA_DV, GLA_DV, GLA_GATE_RANK, D_MODEL, D_MODEL])
        wq, wk, wv, wmg, wgq, wgk, wgv, wgg, wfg, wga, wgb = [
            w[:, o[i]:o[i + 1]] for i in range(11)]
        wfg_p = jnp.pad(wfg, ((0, 0), (0, FG_PAD - GLA_GATE_RANK)))
        wn = jnp.concatenate([_extend_key_columns(wk), wgq, wgk, wgv, wgg, wfg_p, wga, wgb],
                             axis=1).astype(BF16)
        wt = jnp.concatenate([wq, wv, wmg], axis=1).T.astype(BF16)
        wfg2 = jnp.pad(w_gla_fg2[layer], ((0, FG_PAD - GLA_GATE_RANK), (0, 0))).astype(BF16)

        (qt, vt, smgt, k, kmean, gq, gk, gv, sgg, la, sga, sgb) = _in_proj(
            x, norm_in_g[layer][None, :], wn, wt, cq, sq, ck, sk, wfg2,
            b_gla_fg[layer][None, :], b_merge[layer])

        km = kmean.reshape(B, nt, MOBA_HEADS, KX)
        km = jnp.concatenate([km[..., KX_LO:KX_LO + HALF], km[..., KX_HI:KX_HI + HALF]], axis=-1)
        km = km.reshape(B, nt, N_PAIRS, PAIR).transpose(0, 2, 1, 3)
        km = (km[:, :, None, :, :] * pair_mask[None, None, :, None, :]).reshape(
            B, N_PAIRS, 2 * nt, PAIR)
        km_hi = km.astype(BF16)
        km_lo = (km - km_hi.astype(F32)).astype(BF16)

        oat = _moba(qt, k, vt, smgt, km_hi, km_lo)
        ob = _gla(gq, gk, gv, la, sgg, gla_norm_g[layer][None, :])
        x = _out_stage(oat, ob, sga, sgb, x, w_proj_a[layer].astype(BF16),
                       w_proj_b[layer].astype(BF16), w_out[layer].astype(BF16),
                       norm_f_g[None, :], final_norm=(layer == depth - 1))
    return x
```

```python
import functools
import math

import jax
import jax.numpy as jnp
import numpy as np
from jax import lax
from jax.experimental import pallas as pl
from jax.experimental.pallas import tpu as pltpu

F32 = jnp.float32
BF16 = jnp.bfloat16

D_MODEL = 1024
MOBA_HEADS = 8
MOBA_HEAD_DIM = 64
MOBA_WIDTH = MOBA_HEADS * MOBA_HEAD_DIM
MOBA_BLOCK = 256
MOBA_TOPK = 3
ROPE_THETA = 10000.0
GLA_HEADS = 4
GLA_DK = 256
GLA_DV = 512
GLA_HEAD_K = 64
GLA_HEAD_V = 128
GLA_GATE_RANK = 16
GLA_GATE_NORM = 16.0
GLA_CHUNK = 64
RMS_EPS = 1e-6
NEG_INF = -1e30

LANES = 128
BF16_ROWS = 16
ROW_TILE = MOBA_BLOCK
PAIR = 2 * MOBA_HEAD_DIM
N_PAIRS = MOBA_HEADS // 2
GLA_STEP_ROWS = 512
OUT_STEP_ROWS = 512
IN_STEP_ROWS = 512
QK_AHEAD = 4
MOBA_Q_TILE = 512
VMEM_LIMIT = 56 * 1024 * 1024

HALF = MOBA_HEAD_DIM // 2
KX = LANES
KX_LO, KX_OH, KX_HI = 0, HALF, 2 * HALF
N_BLOCKS_MAX = BF16_ROWS
VX = MOBA_HEAD_DIM + BF16_ROWS

_C_K = 0
_C_GQ = _C_K + MOBA_WIDTH
_C_GK = _C_GQ + GLA_DK
_C_GV = _C_GK + GLA_DK
_C_GG = _C_GV + GLA_DV
_C_GA = _C_GG + GLA_DV
_C_GB = _C_GA + D_MODEL
_C_END = _C_GB + D_MODEL
_R_Q = 0
_R_V = _R_Q + MOBA_WIDTH
_R_MG = _R_V + MOBA_WIDTH
_R_FG = _R_MG + MOBA_WIDTH
_R_END = _R_FG + GLA_GATE_RANK


def _dot(a, b):
    return jnp.dot(a, b, preferred_element_type=F32)


def _dot_nt(a, b):
    return lax.dot_general(a, b, (((1,), (1,)), ((), ())), preferred_element_type=F32)


def _dot_tn(a, b):
    return lax.dot_general(a, b, (((0,), (0,)), ((), ())), preferred_element_type=F32)


def _sigmoid(x):
    return 1.0 / (1.0 + jnp.exp(-x))


def _silu(x):
    return x * _sigmoid(x)


def _in_proj_kernel(x_ref, g_ref, wn_ref, wt_ref, cq_ref, sq_ref, ck_ref, sk_ref,
                    wfg2_ref, bfg_ref, bm_ref,
                    qt_ref, vt_ref, smgt_ref, k_ref, kmean_ref, gq_ref, gk_ref, gv_ref,
                    sgg_ref, la_ref, sga_ref, sgb_ref):
    t = pl.program_id(1)
    rt = x_ref.shape[1]
    nb = MOBA_BLOCK
    n_sub = rt // nb
    hd = MOBA_HEAD_DIM
    blk_cols = lambda j: slice(j * nb, (j + 1) * nb)
    x = x_ref[0]
    ms = jnp.mean(x * x, axis=-1, keepdims=True)
    hb = (x * lax.rsqrt(ms + RMS_EPS) * g_ref[...]).astype(BF16)

    pt = _dot_nt(wt_ref[...], hb)
    cq = cq_ref[...]
    sq = sq_ref[...]
    ones_rows = jnp.where(lax.broadcasted_iota(jnp.int32, (BF16_ROWS, nb), 0) == 0,
                          1.0, 0.0).astype(BF16)
    for h in range(MOBA_HEADS):
        blk = pt[_R_Q + h * hd:_R_Q + (h + 1) * hd]
        swapped = jnp.concatenate([blk[HALF:], blk[:HALF]], axis=0)
        q_rot = (blk * cq + swapped * sq).astype(BF16)
        v_rows = pt[_R_V + h * hd:_R_V + (h + 1) * hd].astype(BF16)
        for j in range(n_sub):
            qt_ref[0, j, h * hd:(h + 1) * hd, :] = q_rot[:, blk_cols(j)]
            vt_ref[0, j, h * VX:h * VX + hd, :] = v_rows[:, blk_cols(j)]
            vt_ref[0, j, h * VX + hd:(h + 1) * VX, :] = ones_rows
    smg = _silu(pt[_R_MG:_R_FG]).astype(BF16)
    for j in range(n_sub):
        smgt_ref[0, j] = smg[:, blk_cols(j)]

    pn = _dot(hb, wn_ref[...])
    ck = ck_ref[...]
    sk = sk_ref[...]
    lane = lax.broadcasted_iota(jnp.int32, (rt, KX), 1)
    row_block = t * n_sub + lax.broadcasted_iota(jnp.int32, (rt, KX), 0) // nb
    block_onehot = jnp.where(lane == KX_OH + row_block, 1.0, 0.0)
    own_lanes = (lane % (2 * HALF)) < HALF
    for p in range(N_PAIRS):
        blk = pn[:, _C_K + p * PAIR:_C_K + (p + 1) * PAIR]
        kr = blk * ck + pltpu.roll(blk, 2 * HALF, axis=1) * sk
        for j in range(n_sub):
            kmean_ref[0, j, :, p * PAIR:(p + 1) * PAIR] = jnp.mean(
                kr[blk_cols(j)], axis=0, keepdims=True)
        for hh, src in ((0, kr), (1, pltpu.roll(kr, KX - HALF, axis=1))):
            h = 2 * p + hh
            k_ref[0, :, h * KX:(h + 1) * KX] = (
                jnp.where(own_lanes, src, 0.0) + block_onehot).astype(BF16)

    gq_ref[0] = (pn[:, _C_GQ:_C_GK] * (GLA_HEAD_K ** -0.5)).astype(BF16)
    gk_ref[0] = pn[:, _C_GK:_C_GV].astype(BF16)
    gv_ref[0] = pn[:, _C_GV:_C_GG].astype(BF16)
    sgg_ref[0] = _silu(pn[:, _C_GG:_C_GA]).astype(BF16)

    z = _dot_tn(pt[_R_FG:_R_END].astype(BF16), wfg2_ref[...]) + bfg_ref[...]
    log_sig = jnp.minimum(z, 0.0) - jnp.log1p(jnp.exp(-jnp.abs(z)))
    la_ref[0] = log_sig * (1.0 / GLA_GATE_NORM)

    sga_ref[0] = _sigmoid(pn[:, _C_GA:_C_GB] + bm_ref[0:1, :]).astype(BF16)
    sgb_ref[0] = _sigmoid(pn[:, _C_GB:_C_END] + bm_ref[1:2, :]).astype(BF16)


def _in_proj(x, g, wn, wt, cq, sq, ck, sk, wfg2, bfg, bm):
    B, S, D = x.shape
    nt = S // ROW_TILE
    rt = IN_STEP_ROWS
    n_sub = rt // ROW_TILE
    const = lambda shape: pl.BlockSpec(shape, lambda b, t: (0,) * len(shape))
    t_blocked = lambda r: pl.BlockSpec((1, n_sub, r, ROW_TILE), lambda b, t: (b, t, 0, 0))
    rows = lambda w: pl.BlockSpec((1, rt, w), lambda b, t: (b, t, 0))
    t_shape = lambda r: jax.ShapeDtypeStruct((B, nt, r, ROW_TILE), BF16)
    rshape = lambda w, dt=BF16: jax.ShapeDtypeStruct((B, S, w), dt)
    return pl.pallas_call(
        _in_proj_kernel,
        grid=(B, S // rt),
        in_specs=[
            rows(D), const((1, D)), const(wn.shape), const(wt.shape),
            pl.BlockSpec((MOBA_HEAD_DIM, rt), lambda b, t: (0, t)),
            pl.BlockSpec((MOBA_HEAD_DIM, rt), lambda b, t: (0, t)),
            pl.BlockSpec((rt, KX), lambda b, t: (t, 0)),
            pl.BlockSpec((rt, KX), lambda b, t: (t, 0)),
            const(wfg2.shape), const((1, GLA_DK)), const((2, D)),
        ],
        out_specs=[
            t_blocked(MOBA_WIDTH), t_blocked(MOBA_HEADS * VX), t_blocked(MOBA_WIDTH),
            rows(MOBA_HEADS * KX),
            pl.BlockSpec((1, n_sub, 1, MOBA_WIDTH), lambda b, t: (b, t, 0, 0)),
            rows(GLA_DK), rows(GLA_DK), rows(GLA_DV), rows(GLA_DV), rows(GLA_DK),
            rows(D), rows(D),
        ],
        out_shape=[
            t_shape(MOBA_WIDTH), t_shape(MOBA_HEADS * VX), t_shape(MOBA_WIDTH),
            rshape(MOBA_HEADS * KX),
            jax.ShapeDtypeStruct((B, nt, 1, MOBA_WIDTH), F32),
            rshape(GLA_DK), rshape(GLA_DK), rshape(GLA_DV), rshape(GLA_DV),
            rshape(GLA_DK, F32), rshape(D), rshape(D),
        ],
        compiler_params=pltpu.CompilerParams(
            dimension_semantics=("arbitrary", "arbitrary"),
            vmem_limit_bytes=VMEM_LIMIT),
        name="in_proj",
    )(x, g, wn, wt, cq, sq, ck, sk, wfg2, bfg, bm)


def _moba_kernel(qt_ref, k_ref, vt_ref, smgt_ref, kmh_ref, kml_ref, o_ref,
                 qx_ref, s_ref, smax_ref, m_ref, acc_ref):
    nb = MOBA_BLOCK
    hd = MOBA_HEAD_DIM
    n_sub = qt_ref.shape[1]
    qw = n_sub * nb
    n_blocks = k_ref.shape[1] // nb
    assert n_blocks <= N_BLOCKS_MAX
    first_own = pl.program_id(1) * n_sub

    blk_id = lax.broadcasted_iota(jnp.int32, (n_blocks, nb), 0).astype(F32)
    qx_ref[...] = jnp.zeros_like(qx_ref)

    def select_blocks(p):
        for j in range(n_sub):
            own = (first_own + j).astype(F32)
            cols = slice(j * nb, (j + 1) * nb)
            qt = qt_ref[0, j, p * PAIR:(p + 1) * PAIR, :]
            gates = _dot(kmh_ref[0, p], qt) + _dot(kml_ref[0, p], qt)
            for hh in range(2):
                h = 2 * p + hh
                past = blk_id < own
                gh = jnp.where(past, gates[hh * n_blocks:(hh + 1) * n_blocks], NEG_INF)
                picked = jnp.zeros((n_blocks, nb), jnp.bool_)
                for _ in range(MOBA_TOPK):
                    best = jnp.max(gh, axis=0, keepdims=True)
                    first = jnp.min(jnp.where(gh == best, blk_id, n_blocks), axis=0, keepdims=True)
                    hit = blk_id == first
                    picked = picked | hit
                    gh = jnp.where(hit, -jnp.inf, gh)
                keep = (blk_id == own) | (past & picked)
                qx_ref[h, KX_LO:KX_LO + HALF, cols] = qt[hh * hd:hh * hd + HALF]
                qx_ref[h, KX_OH:KX_OH + n_blocks, cols] = (
                    jnp.where(keep, 0.0, NEG_INF).astype(BF16))
                qx_ref[h, KX_HI:KX_HI + HALF, cols] = qt[hh * hd + HALF:(hh + 1) * hd]

    def produce_scores(n, h):
        kb = k_ref[0, pl.ds(pl.multiple_of(n * nb, nb), nb), h * KX:(h + 1) * KX]
        s = _dot(kb, qx_ref[h])
        s_ref[h] = s
        smax_ref[h] = jnp.max(s, axis=0, keepdims=True)

    causal = (lax.broadcasted_iota(jnp.int32, (nb, nb), 0)
              <= lax.broadcasted_iota(jnp.int32, (nb, nb), 1))

    def attend(n, diagonal, prefetch):
        jobs = [(n, h) for h in range(QK_AHEAD, MOBA_HEADS)]
        if prefetch:
            jobs += [(n + 1, h) for h in range(QK_AHEAD)]

        def issue():
            if jobs:
                produce_scores(*jobs.pop(0))

        issue()
        issue()
        for h in range(MOBA_HEADS):
            s = s_ref[h]
            smax = smax_ref[h]
            if diagonal is not None:
                own_cols = slice(diagonal * nb, (diagonal + 1) * nb)
                tri = jnp.where(causal, s[:, own_cols], NEG_INF)
                parts = [s[:, :own_cols.start], tri, s[:, own_cols.stop:]]
                maxes = [smax[:, :own_cols.start], jnp.max(tri, axis=0, keepdims=True),
                         smax[:, own_cols.stop:]]
                s = jnp.concatenate([x for x in parts if x.shape[1]], axis=1)
                smax = jnp.concatenate([x for x in maxes if x.shape[1]], axis=1)
            vt = vt_ref[0, n, h * VX:(h + 1) * VX, :]
            m_old = m_ref[h]
            m_new = jnp.maximum(m_old, smax)
            m_ref[h] = m_new
            acc_ref[h] = (jnp.exp2(m_old - m_new) * acc_ref[h]
                          + _dot(vt, jnp.exp2(s - m_new).astype(BF16)))
            issue()

    m_ref[...] = jnp.full(m_ref.shape, NEG_INF, F32)
    acc_ref[...] = jnp.zeros_like(acc_ref)
    for p in range(N_PAIRS):
        select_blocks(p)
        if 2 * p + 2 == QK_AHEAD:
            for h in range(QK_AHEAD):
                produce_scores(0, h)

    def body(n, c):
        attend(n, None, True)
        return c

    lax.fori_loop(0, first_own, body, 0)
    for j in range(n_sub):
        attend(first_own + j, j, j + 1 < n_sub)
    for h in range(MOBA_HEADS):
        o = acc_ref[h, 0:hd, :] / acc_ref[h, hd:hd + 1, :]
        for j in range(n_sub):
            gate = smgt_ref[0, j, h * hd:(h + 1) * hd, :].astype(F32)
            o_ref[0, j, h * hd:(h + 1) * hd, :] = (o[:, j * nb:(j + 1) * nb] * gate).astype(BF16)


def _moba(qt, k, vt, smgt, kmh, kml):
    B, nt, W, rt = qt.shape
    S = k.shape[1]
    n_sub = MOBA_Q_TILE // rt
    qw = MOBA_Q_TILE
    tile = pl.BlockSpec((1, n_sub, W, rt), lambda b, t: (b, t, 0, 0))
    km = pl.BlockSpec((1, N_PAIRS, 2 * nt, PAIR), lambda b, t: (b, 0, 0, 0))
    return pl.pallas_call(
        _moba_kernel,
        grid=(B, nt // n_sub),
        in_specs=[
            tile,
            pl.BlockSpec((1, S, MOBA_HEADS * KX), lambda b, t: (b, 0, 0)),
            pl.BlockSpec((1, nt, MOBA_HEADS * VX, rt), lambda b, t: (b, 0, 0, 0)),
            tile, km, km,
        ],
        out_specs=tile,
        out_shape=jax.ShapeDtypeStruct((B, nt, W, rt), BF16),
        scratch_shapes=[
            pltpu.VMEM((MOBA_HEADS, KX, qw), BF16),
            pltpu.VMEM((MOBA_HEADS, MOBA_BLOCK, qw), F32),
            pltpu.VMEM((MOBA_HEADS, 1, qw), F32),
            pltpu.VMEM((MOBA_HEADS, 1, qw), F32),
            pltpu.VMEM((MOBA_HEADS, VX, qw), F32),
        ],
        compiler_params=pltpu.CompilerParams(
            dimension_semantics=("arbitrary", "arbitrary"),
            vmem_limit_bytes=VMEM_LIMIT),
        name="moba",
    )(qt, k, vt, smgt, kmh, kml)


def _gla_kernel(q_ref, k_ref, v_ref, la_ref, sgg_ref, gn_ref, o_ref, st_ref):
    C = GLA_CHUNK
    hk, hv = GLA_HEAD_K, GLA_HEAD_V

    @pl.when(pl.program_id(1) == 0)
    def _():
        st_ref[...] = jnp.zeros_like(st_ref)

    nc = GLA_STEP_ROWS // C
    H = GLA_HEADS
    tril_b = (lax.broadcasted_iota(jnp.int32, (C, C), 0)
              >= lax.broadcasted_iota(jnp.int32, (C, C), 1)).astype(BF16)
    tril_stack = (lax.broadcasted_iota(jnp.int32, (H * C, C), 0) % C
                  >= lax.broadcasted_iota(jnp.int32, (H * C, C), 1))
    lane_head = lax.broadcasted_iota(jnp.int32, (C, GLA_DK), 1) // hk
    zero_b = jnp.zeros((C, GLA_DK), BF16)
    gn = gn_ref[...]

    def stack_heads(x):
        return jnp.concatenate([jnp.where(lane_head == h, x, zero_b) for h in range(H)], axis=0)

    chunk = lambda c: slice(c * C, (c + 1) * C)
    g = la_ref[0]
    g_hi = g.astype(BF16)
    g_lo = (g - g_hi.astype(F32)).astype(BF16)
    b = [_dot(tril_b, g_hi[chunk(c)]) + _dot(tril_b, g_lo[chunk(c)]) for c in range(nc)]

    qe_s, qb_s, ke_b, kl_s, decay = [], [], [], [], []
    for c in range(nc):
        b_mid = b[c][C // 2 - 1:C // 2]
        b_last = b[c][C - 1:C]
        qe = q_ref[0, chunk(c), :].astype(F32) * jnp.exp(b[c] - b_mid)
        ke = k_ref[0, chunk(c), :].astype(F32) * jnp.exp(b_mid - b[c])
        qe_s.append(stack_heads(qe.astype(BF16)))
        qb_s.append(stack_heads((qe * jnp.exp(b_mid)).astype(BF16)))
        kl_s.append(stack_heads((ke * jnp.exp(b_last - b_mid)).astype(BF16)))
        ke_b.append(ke.astype(BF16))
        decay.append(jnp.exp(b_last))

    attn = [jnp.where(tril_stack, _dot_nt(qe_s[c], ke_b[c]), 0.0).astype(BF16) for c in range(nc)]
    intra = [jnp.concatenate(
        [_dot(attn[c][h * C:(h + 1) * C], v_ref[0, chunk(c), h * hv:(h + 1) * hv])
         for h in range(H)], axis=1) for c in range(nc)]

    v_s = [jnp.concatenate([v_ref[0, chunk(c), h * hv:(h + 1) * hv] for h in range(H)], axis=0)
           for c in range(nc)]
    upd = [_dot_tn(v_s[c], kl_s[c]) for c in range(nc)]

    st = st_ref[...]
    inter = []
    for c in range(nc):
        o_s = _dot_nt(qb_s[c], st.astype(BF16))
        inter.append(jnp.concatenate([o_s[h * C:(h + 1) * C] for h in range(H)], axis=1))
        st = st * decay[c] + upd[c]
    st_ref[...] = st

    for c in range(nc):
        o = inter[c] + intra[c]
        for h in range(H):
            oh = o[:, h * hv:(h + 1) * hv]
            ms = jnp.mean(oh * oh, axis=-1, keepdims=True)
            y = oh * lax.rsqrt(ms + RMS_EPS) * gn
            gate = sgg_ref[0, chunk(c), h * hv:(h + 1) * hv].astype(F32)
            o_ref[0, chunk(c), h * hv:(h + 1) * hv] = (y * gate).astype(BF16)


def _gla(gq, gk, gv, la, sgg, gn):
    B, S, _ = gq.shape
    rows = lambda w: pl.BlockSpec((1, GLA_STEP_ROWS, w), lambda b, t: (b, t, 0))
    return pl.pallas_call(
        _gla_kernel,
        grid=(B, S // GLA_STEP_ROWS),
        in_specs=[rows(GLA_DK), rows(GLA_DK), rows(GLA_DV), rows(GLA_DK), rows(GLA_DV),
                  pl.BlockSpec((1, GLA_HEAD_V), lambda b, t: (0, 0))],
        out_specs=rows(GLA_DV),
        out_shape=jax.ShapeDtypeStruct((B, S, GLA_DV), BF16),
        scratch_shapes=[pltpu.VMEM((GLA_HEAD_V, GLA_DK), F32)],
        compiler_params=pltpu.CompilerParams(
            dimension_semantics=("arbitrary", "arbitrary"),
            vmem_limit_bytes=VMEM_LIMIT),
        name="gla",
    )(gq, gk, gv, la, sgg, gn)


def _out_kernel(oat_ref, ob_ref, sga_ref, sgb_ref, x_ref, wpa_ref, wpb_ref, wo_ref, gf_ref,
                y_ref, *, final_norm):
    n_sub, rt = oat_ref.shape[1], oat_ref.shape[3]
    sub = lambda j: slice(j * rt, (j + 1) * rt)
    ya = [_dot_tn(oat_ref[0, j], wpa_ref[...]) for j in range(n_sub)]
    yb = [_dot(ob_ref[0, sub(j), :], wpb_ref[...]) for j in range(n_sub)]
    merged = [(sga_ref[0, sub(j), :].astype(F32) * ya[j]
               + sgb_ref[0, sub(j), :].astype(F32) * yb[j]).astype(BF16) for j in range(n_sub)]
    proj = [_dot(merged[j], wo_ref[...]) for j in range(n_sub)]
    for j in range(n_sub):
        r = x_ref[0, sub(j), :] + proj[j]
        if final_norm:
            ms = jnp.mean(r * r, axis=-1, keepdims=True)
            r = r * lax.rsqrt(ms + RMS_EPS) * gf_ref[...]
        y_ref[0, sub(j), :] = r


def _out_stage(oat, ob, sga, sgb, x, wpa, wpb, wo, gf, final_norm):
    B, S, D = x.shape
    n_sub = OUT_STEP_ROWS // ROW_TILE
    nt = S // OUT_STEP_ROWS
    rt = OUT_STEP_ROWS
    const = lambda shape: pl.BlockSpec(shape, lambda b, t: (0,) * len(shape))
    rows = lambda w: pl.BlockSpec((1, rt, w), lambda b, t: (b, t, 0))
    return pl.pallas_call(
        functools.partial(_out_kernel, final_norm=final_norm),
        grid=(B, nt),
        in_specs=[
            pl.BlockSpec((1, n_sub, MOBA_WIDTH, ROW_TILE), lambda b, t: (b, t, 0, 0)),
            rows(GLA_DV), rows(D), rows(D), rows(D),
            const(wpa.shape), const(wpb.shape), const(wo.shape), const((1, D)),
        ],
        out_specs=rows(D),
        out_shape=jax.ShapeDtypeStruct((B, S, D), F32),
        compiler_params=pltpu.CompilerParams(
            dimension_semantics=("arbitrary", "arbitrary"),
            vmem_limit_bytes=VMEM_LIMIT),
        name="out_stage",
    )(oat, ob, sga, sgb, x, wpa, wpb, wo, gf)


def _rope_tables(S):
    inv_freq = 1.0 / (ROPE_THETA ** (jnp.arange(HALF, dtype=F32) / HALF))
    ang = jnp.arange(S, dtype=F32)[:, None] * inv_freq[None, :]
    return jnp.cos(ang), jnp.sin(ang)


def _swap_half_head(a):
    lead = a.shape[:-1]
    a = a.reshape(*lead, N_PAIRS, 2, 2, HALF)
    return jnp.swapaxes(a, -2, -3).reshape(*lead, MOBA_WIDTH)


def kernel(x, norm_in_g, w_in, b_merge, w_gla_fg2, b_gla_fg, gla_norm_g,
           w_proj_a, w_proj_b, w_out, norm_f_g):
    B, S, D = x.shape
    depth = w_in.shape[0]
    nt = S // MOBA_BLOCK
    cos, sin = _rope_tables(S)
    q_scale = MOBA_HEAD_DIM ** -0.5 * math.log2(math.e)
    cq = (jnp.concatenate([cos, cos], axis=1) * q_scale).T
    sq = (jnp.concatenate([-sin, sin], axis=1) * q_scale).T
    ck = jnp.tile(cos, (1, KX // HALF))
    sk = jnp.tile(sin, (1, KX // HALF)) * jnp.where(jnp.arange(KX) < KX_HI, -1.0, 1.0)[None, :]
    pair_mask = jnp.asarray(
        np.arange(PAIR)[None, :] // MOBA_HEAD_DIM == np.arange(2)[:, None], F32)

    for layer in range(depth):
        w = w_in[layer]
        o = np.cumsum([0, MOBA_WIDTH, MOBA_WIDTH, MOBA_WIDTH, MOBA_WIDTH, GLA_DK, GLA_DK,
                       GLA_DV, GLA_DV, GLA_GATE_RANK, D_MODEL, D_MODEL])
        wq, wk, wv, wmg, wgq, wgk, wgv, wgg, wfg, wga, wgb = [
            w[:, o[i]:o[i + 1]] for i in range(11)]
        wn = jnp.concatenate([_swap_half_head(wk), wgq, wgk, wgv, wgg, wga, wgb],
                             axis=1).astype(BF16)
        wt = jnp.concatenate([wq, wv, wmg, wfg], axis=1).T.astype(BF16)
        wfg2 = w_gla_fg2[layer].astype(BF16)

        (qt, vt, smgt, k, kmean, gq, gk, gv, sgg, la, sga, sgb) = _in_proj(
            x, norm_in_g[layer][None, :], wn, wt, cq, sq, ck, sk, wfg2,
            b_gla_fg[layer][None, :], b_merge[layer])

        km = _swap_half_head(kmean.reshape(B, nt, MOBA_WIDTH))
        km = km.reshape(B, nt, N_PAIRS, PAIR).transpose(0, 2, 1, 3)
        km = (km[:, :, None, :, :] * pair_mask[None, None, :, None, :]).reshape(
            B, N_PAIRS, 2 * nt, PAIR)
        km_hi = km.astype(BF16)
        km_lo = (km - km_hi.astype(F32)).astype(BF16)

        oat = _moba(qt, k, vt, smgt, km_hi, km_lo)
        ob = _gla(gq, gk, gv, la, sgg, gla_norm_g[layer][None, :])
        x = _out_stage(oat, ob, sga, sgb, x, w_proj_a[layer].astype(BF16),
                       w_proj_b[layer].astype(BF16), w_out[layer].astype(BF16),
                       norm_f_g[None, :], final_norm=(layer == depth - 1))
    return x
```

```python
import functools
import math

import jax
import jax.numpy as jnp
import numpy as np
from jax import lax
from jax.experimental import pallas as pl
from jax.experimental.pallas import tpu as pltpu

F32 = jnp.float32
BF16 = jnp.bfloat16

D_MODEL = 1024
MOBA_HEADS = 8
MOBA_HEAD_DIM = 64
MOBA_WIDTH = MOBA_HEADS * MOBA_HEAD_DIM
MOBA_BLOCK = 256
MOBA_TOPK = 3
ROPE_THETA = 10000.0
GLA_HEADS = 4
GLA_DK = 256
GLA_DV = 512
GLA_HEAD_K = 64
GLA_HEAD_V = 128
GLA_GATE_RANK = 16
GLA_GATE_NORM = 16.0
GLA_CHUNK = 64
RMS_EPS = 1e-6
NEG_INF = -1e30

LANES = 128
BF16_ROWS = 16
ROW_TILE = MOBA_BLOCK
PAIR = 2 * MOBA_HEAD_DIM
N_PAIRS = MOBA_HEADS // 2
GLA_STEP_ROWS = 512
OUT_STEP_ROWS = 512
IN_STEP_ROWS = 512
QK_AHEAD = 4
MOBA_Q_TILE = 512
LOOP_BLOCKS = 2
VMEM_LIMIT = 56 * 1024 * 1024

HALF = MOBA_HEAD_DIM // 2
KX = LANES
KX_LO, KX_OH, KX_HI = 0, HALF, 2 * HALF
N_BLOCKS_MAX = BF16_ROWS
VX = MOBA_HEAD_DIM + BF16_ROWS

_C_K = 0
_C_GQ = _C_K + MOBA_WIDTH
_C_GK = _C_GQ + GLA_DK
_C_GV = _C_GK + GLA_DK
_C_GG = _C_GV + GLA_DV
_C_GA = _C_GG + GLA_DV
_C_GB = _C_GA + D_MODEL
_C_END = _C_GB + D_MODEL
_R_Q = 0
_R_V = _R_Q + MOBA_WIDTH
_R_MG = _R_V + MOBA_WIDTH
_R_FG = _R_MG + MOBA_WIDTH
_R_END = _R_FG + GLA_GATE_RANK


def _dot(a, b):
    return jnp.dot(a, b, preferred_element_type=F32)


def _dot_nt(a, b):
    return lax.dot_general(a, b, (((1,), (1,)), ((), ())), preferred_element_type=F32)


def _dot_tn(a, b):
    return lax.dot_general(a, b, (((0,), (0,)), ((), ())), preferred_element_type=F32)


def _sigmoid(x):
    return 1.0 / (1.0 + jnp.exp(-x))


def _silu(x):
    return x * _sigmoid(x)


def _in_proj_kernel(x_ref, g_ref, wn_ref, wt_ref, cq_ref, sq_ref, ck_ref, sk_ref,
                    wfg2_ref, bfg_ref, bm_ref,
                    qt_ref, vt_ref, smgt_ref, k_ref, kmean_ref, gq_ref, gk_ref, gv_ref,
                    sgg_ref, la_ref, sga_ref, sgb_ref):
    t = pl.program_id(1)
    rt = x_ref.shape[1]
    nb = MOBA_BLOCK
    n_sub = rt // nb
    hd = MOBA_HEAD_DIM
    blk_cols = lambda j: slice(j * nb, (j + 1) * nb)
    x = x_ref[0]
    ms = jnp.mean(x * x, axis=-1, keepdims=True)
    hb = (x * lax.rsqrt(ms + RMS_EPS) * g_ref[...]).astype(BF16)

    pt = _dot_nt(wt_ref[...], hb)
    cq = cq_ref[...]
    sq = sq_ref[...]
    ones_rows = jnp.where(lax.broadcasted_iota(jnp.int32, (BF16_ROWS, nb), 0) == 0,
                          1.0, 0.0).astype(BF16)
    for h in range(MOBA_HEADS):
        blk = pt[_R_Q + h * hd:_R_Q + (h + 1) * hd]
        swapped = jnp.concatenate([blk[HALF:], blk[:HALF]], axis=0)
        q_rot = (blk * cq + swapped * sq).astype(BF16)
        v_rows = pt[_R_V + h * hd:_R_V + (h + 1) * hd].astype(BF16)
        for j in range(n_sub):
            qt_ref[0, j, h * hd:(h + 1) * hd, :] = q_rot[:, blk_cols(j)]
            vt_ref[0, j, h * VX:h * VX + hd, :] = v_rows[:, blk_cols(j)]
            vt_ref[0, j, h * VX + hd:(h + 1) * VX, :] = ones_rows
    smg = _silu(pt[_R_MG:_R_FG]).astype(BF16)
    for j in range(n_sub):
        smgt_ref[0, j] = smg[:, blk_cols(j)]

    pn = _dot(hb, wn_ref[...])
    ck = ck_ref[...]
    sk = sk_ref[...]
    lane = lax.broadcasted_iota(jnp.int32, (rt, KX), 1)
    row_block = t * n_sub + lax.broadcasted_iota(jnp.int32, (rt, KX), 0) // nb
    block_onehot = jnp.where(lane == KX_OH + row_block, 1.0, 0.0)
    own_lanes = (lane % (2 * HALF)) < HALF
    for p in range(N_PAIRS):
        blk = pn[:, _C_K + p * PAIR:_C_K + (p + 1) * PAIR]
        kr = blk * ck + pltpu.roll(blk, 2 * HALF, axis=1) * sk
        for j in range(n_sub):
            kmean_ref[0, j, :, p * PAIR:(p + 1) * PAIR] = jnp.mean(
                kr[blk_cols(j)], axis=0, keepdims=True)
        for hh, src in ((0, kr), (1, pltpu.roll(kr, KX - HALF, axis=1))):
            h = 2 * p + hh
            k_ref[0, :, h * KX:(h + 1) * KX] = (
                jnp.where(own_lanes, src, 0.0) + block_onehot).astype(BF16)

    gq_ref[0] = (pn[:, _C_GQ:_C_GK] * (GLA_HEAD_K ** -0.5)).astype(BF16)
    gk_ref[0] = pn[:, _C_GK:_C_GV].astype(BF16)
    gv_ref[0] = pn[:, _C_GV:_C_GG].astype(BF16)
    sgg_ref[0] = _silu(pn[:, _C_GG:_C_GA]).astype(BF16)

    z = _dot_tn(pt[_R_FG:_R_END].astype(BF16), wfg2_ref[...]) + bfg_ref[...]
    log_sig = jnp.minimum(z, 0.0) - jnp.log1p(jnp.exp(-jnp.abs(z)))
    la_ref[0] = log_sig * (1.0 / GLA_GATE_NORM)

    sga_ref[0] = _sigmoid(pn[:, _C_GA:_C_GB] + bm_ref[0:1, :]).astype(BF16)
    sgb_ref[0] = _sigmoid(pn[:, _C_GB:_C_END] + bm_ref[1:2, :]).astype(BF16)


def _in_proj(x, g, wn, wt, cq, sq, ck, sk, wfg2, bfg, bm):
    B, S, D = x.shape
    nt = S // ROW_TILE
    rt = IN_STEP_ROWS
    n_sub = rt // ROW_TILE
    const = lambda shape: pl.BlockSpec(shape, lambda b, t: (0,) * len(shape))
    t_blocked = lambda r: pl.BlockSpec((1, n_sub, r, ROW_TILE), lambda b, t: (b, t, 0, 0))
    rows = lambda w: pl.BlockSpec((1, rt, w), lambda b, t: (b, t, 0))
    t_shape = lambda r: jax.ShapeDtypeStruct((B, nt, r, ROW_TILE), BF16)
    rshape = lambda w, dt=BF16: jax.ShapeDtypeStruct((B, S, w), dt)
    return pl.pallas_call(
        _in_proj_kernel,
        grid=(B, S // rt),
        in_specs=[
            rows(D), const((1, D)), const(wn.shape), const(wt.shape),
            pl.BlockSpec((MOBA_HEAD_DIM, rt), lambda b, t: (0, t)),
            pl.BlockSpec((MOBA_HEAD_DIM, rt), lambda b, t: (0, t)),
            pl.BlockSpec((rt, KX), lambda b, t: (t, 0)),
            pl.BlockSpec((rt, KX), lambda b, t: (t, 0)),
            const(wfg2.shape), const((1, GLA_DK)), const((2, D)),
        ],
        out_specs=[
            t_blocked(MOBA_WIDTH), t_blocked(MOBA_HEADS * VX), t_blocked(MOBA_WIDTH),
            rows(MOBA_HEADS * KX),
            pl.BlockSpec((1, n_sub, 1, MOBA_WIDTH), lambda b, t: (b, t, 0, 0)),
            rows(GLA_DK), rows(GLA_DK), rows(GLA_DV), rows(GLA_DV), rows(GLA_DK),
            rows(D), rows(D),
        ],
        out_shape=[
            t_shape(MOBA_WIDTH), t_shape(MOBA_HEADS * VX), t_shape(MOBA_WIDTH),
            rshape(MOBA_HEADS * KX),
            jax.ShapeDtypeStruct((B, nt, 1, MOBA_WIDTH), F32),
            rshape(GLA_DK), rshape(GLA_DK), rshape(GLA_DV), rshape(GLA_DV),
            rshape(GLA_DK, F32), rshape(D), rshape(D),
        ],
        compiler_params=pltpu.CompilerParams(
            dimension_semantics=("arbitrary", "arbitrary"),
            vmem_limit_bytes=VMEM_LIMIT),
        name="in_proj",
    )(x, g, wn, wt, cq, sq, ck, sk, wfg2, bfg, bm)


def _moba_kernel(qt_ref, k_ref, vt_ref, smgt_ref, kmh_ref, kml_ref, o_ref,
                 qx_ref, s_ref, smax_ref, m_ref, acc_ref):
    nb = MOBA_BLOCK
    hd = MOBA_HEAD_DIM
    n_sub = qt_ref.shape[1]
    qw = n_sub * nb
    n_blocks = k_ref.shape[1] // nb
    assert n_blocks <= N_BLOCKS_MAX
    first_own = pl.program_id(1) * n_sub

    blk_id = lax.broadcasted_iota(jnp.int32, (n_blocks, nb), 0).astype(F32)
    qx_ref[...] = jnp.zeros_like(qx_ref)

    def select_blocks(p):
        for j in range(n_sub):
            own = (first_own + j).astype(F32)
            cols = slice(j * nb, (j + 1) * nb)
            qt = qt_ref[0, j, p * PAIR:(p + 1) * PAIR, :]
            gates = _dot(kmh_ref[0, p], qt) + _dot(kml_ref[0, p], qt)
            for hh in range(2):
                h = 2 * p + hh
                past = blk_id < own
                gh = jnp.where(past, gates[hh * n_blocks:(hh + 1) * n_blocks], NEG_INF)
                picked = jnp.zeros((n_blocks, nb), jnp.bool_)
                for _ in range(MOBA_TOPK):
                    best = jnp.max(gh, axis=0, keepdims=True)
                    first = jnp.min(jnp.where(gh == best, blk_id, n_blocks), axis=0, keepdims=True)
                    hit = blk_id == first
                    picked = picked | hit
                    gh = jnp.where(hit, -jnp.inf, gh)
                keep = (blk_id == own) | (past & picked)
                qx_ref[h, KX_LO:KX_LO + HALF, cols] = qt[hh * hd:hh * hd + HALF]
                qx_ref[h, KX_OH:KX_OH + n_blocks, cols] = (
                    jnp.where(keep, 0.0, NEG_INF).astype(BF16))
                qx_ref[h, KX_HI:KX_HI + HALF, cols] = qt[hh * hd + HALF:(hh + 1) * hd]

    def produce_scores(n, h):
        kb = k_ref[0, pl.ds(pl.multiple_of(n * nb, nb), nb), h * KX:(h + 1) * KX]
        s = _dot(kb, qx_ref[h])
        s_ref[h] = s
        smax_ref[h] = jnp.max(s, axis=0, keepdims=True)

    causal = (lax.broadcasted_iota(jnp.int32, (nb, nb), 0)
              <= lax.broadcasted_iota(jnp.int32, (nb, nb), 1))

    def attend(blocks, diagonal, next_block):
        jobs = [(blocks[0], h) for h in range(QK_AHEAD, MOBA_HEADS)]
        jobs += [(n, h) for n in blocks[1:] for h in range(MOBA_HEADS)]
        if next_block is not None:
            jobs += [(next_block, h) for h in range(QK_AHEAD)]

        def issue():
            if jobs:
                produce_scores(*jobs.pop(0))

        issue()
        issue()
        for n, diag in zip(blocks, diagonal):
            for h in range(MOBA_HEADS):
                s = s_ref[h]
                smax = smax_ref[h]
                if diag is not None:
                    own_cols = slice(diag * nb, (diag + 1) * nb)
                    tri = jnp.where(causal, s[:, own_cols], NEG_INF)
                    parts = [s[:, :own_cols.start], tri, s[:, own_cols.stop:]]
                    maxes = [smax[:, :own_cols.start], jnp.max(tri, axis=0, keepdims=True),
                             smax[:, own_cols.stop:]]
                    s = jnp.concatenate([x for x in parts if x.shape[1]], axis=1)
                    smax = jnp.concatenate([x for x in maxes if x.shape[1]], axis=1)
                vt = vt_ref[0, n, h * VX:(h + 1) * VX, :]
                m_old = m_ref[h]
                m_new = jnp.maximum(m_old, smax)
                m_ref[h] = m_new
                acc_ref[h] = (jnp.exp2(m_old - m_new) * acc_ref[h]
                              + _dot(vt, jnp.exp2(s - m_new).astype(BF16)))
                issue()

    m_ref[...] = jnp.full(m_ref.shape, NEG_INF, F32)
    acc_ref[...] = jnp.zeros_like(acc_ref)
    for p in range(N_PAIRS):
        select_blocks(p)
        if 2 * p + 2 == QK_AHEAD:
            for h in range(QK_AHEAD):
                produce_scores(0, h)

    assert n_sub % LOOP_BLOCKS == 0

    def body(i, c):
        n = i * LOOP_BLOCKS
        attend([n + d for d in range(LOOP_BLOCKS)], [None] * LOOP_BLOCKS, n + LOOP_BLOCKS)
        return c

    lax.fori_loop(0, first_own // LOOP_BLOCKS, body, 0)
    attend([first_own + j for j in range(n_sub)], list(range(n_sub)), None)
    for h in range(MOBA_HEADS):
        o = acc_ref[h, 0:hd, :] / acc_ref[h, hd:hd + 1, :]
        for j in range(n_sub):
            gate = smgt_ref[0, j, h * hd:(h + 1) * hd, :].astype(F32)
            o_ref[0, j, h * hd:(h + 1) * hd, :] = (o[:, j * nb:(j + 1) * nb] * gate).astype(BF16)


def _moba(qt, k, vt, smgt, kmh, kml):
    B, nt, W, rt = qt.shape
    S = k.shape[1]
    n_sub = MOBA_Q_TILE // rt
    qw = MOBA_Q_TILE
    tile = pl.BlockSpec((1, n_sub, W, rt), lambda b, t: (b, t, 0, 0))
    km = pl.BlockSpec((1, N_PAIRS, 2 * nt, PAIR), lambda b, t: (b, 0, 0, 0))
    return pl.pallas_call(
        _moba_kernel,
        grid=(B, nt // n_sub),
        in_specs=[
            tile,
            pl.BlockSpec((1, S, MOBA_HEADS * KX), lambda b, t: (b, 0, 0)),
            pl.BlockSpec((1, nt, MOBA_HEADS * VX, rt), lambda b, t: (b, 0, 0, 0)),
            tile, km, km,
        ],
        out_specs=tile,
        out_shape=jax.ShapeDtypeStruct((B, nt, W, rt), BF16),
        scratch_shapes=[
            pltpu.VMEM((MOBA_HEADS, KX, qw), BF16),
            pltpu.VMEM((MOBA_HEADS, MOBA_BLOCK, qw), F32),
            pltpu.VMEM((MOBA_HEADS, 1, qw), F32),
            pltpu.VMEM((MOBA_HEADS, 1, qw), F32),
            pltpu.VMEM((MOBA_HEADS, VX, qw), F32),
        ],
        compiler_params=pltpu.CompilerParams(
            dimension_semantics=("arbitrary", "arbitrary"),
            vmem_limit_bytes=VMEM_LIMIT),
        name="moba",
    )(qt, k, vt, smgt, kmh, kml)


def _gla_kernel(q_ref, k_ref, v_ref, la_ref, sgg_ref, gn_ref, o_ref, st_ref):
    C = GLA_CHUNK
    hk, hv = GLA_HEAD_K, GLA_HEAD_V

    @pl.when(pl.program_id(1) == 0)
    def _():
        st_ref[...] = jnp.zeros_like(st_ref)

    nc = GLA_STEP_ROWS // C
    H = GLA_HEADS
    tril_b = (lax.broadcasted_iota(jnp.int32, (C, C), 0)
              >= lax.broadcasted_iota(jnp.int32, (C, C), 1)).astype(BF16)
    tril_stack = (lax.broadcasted_iota(jnp.int32, (H * C, C), 0) % C
                  >= lax.broadcasted_iota(jnp.int32, (H * C, C), 1))
    lane_head = lax.broadcasted_iota(jnp.int32, (C, GLA_DK), 1) // hk
    zero_b = jnp.zeros((C, GLA_DK), BF16)
    gn = gn_ref[...]

    def stack_heads(x):
        return jnp.concatenate([jnp.where(lane_head == h, x, zero_b) for h in range(H)], axis=0)

    chunk = lambda c: slice(c * C, (c + 1) * C)
    g = la_ref[0]
    g_hi = g.astype(BF16)
    g_lo = (g - g_hi.astype(F32)).astype(BF16)
    b = [_dot(tril_b, g_hi[chunk(c)]) + _dot(tril_b, g_lo[chunk(c)]) for c in range(nc)]

    qe_s, qb_s, ke_b, kl_s, decay = [], [], [], [], []
    for c in range(nc):
        b_mid = b[c][C // 2 - 1:C // 2]
        b_last = b[c][C - 1:C]
        qe = q_ref[0, chunk(c), :].astype(F32) * jnp.exp(b[c] - b_mid)
        ke = k_ref[0, chunk(c), :].astype(F32) * jnp.exp(b_mid - b[c])
        qe_s.append(stack_heads(qe.astype(BF16)))
        qb_s.append(stack_heads((qe * jnp.exp(b_mid)).astype(BF16)))
        kl_s.append(stack_heads((ke * jnp.exp(b_last - b_mid)).astype(BF16)))
        ke_b.append(ke.astype(BF16))
        decay.append(jnp.exp(b_last))

    attn = [jnp.where(tril_stack, _dot_nt(qe_s[c], ke_b[c]), 0.0).astype(BF16) for c in range(nc)]
    intra = [jnp.concatenate(
        [_dot(attn[c][h * C:(h + 1) * C], v_ref[0, chunk(c), h * hv:(h + 1) * hv])
         for h in range(H)], axis=1) for c in range(nc)]

    v_s = [jnp.concatenate([v_ref[0, chunk(c), h * hv:(h + 1) * hv] for h in range(H)], axis=0)
           for c in range(nc)]
    upd = [_dot_tn(v_s[c], kl_s[c]) for c in range(nc)]

    st = st_ref[...]
    inter = []
    for c in range(nc):
        o_s = _dot_nt(qb_s[c], st.astype(BF16))
        inter.append(jnp.concatenate([o_s[h * C:(h + 1) * C] for h in range(H)], axis=1))
        st = st * decay[c] + upd[c]
    st_ref[...] = st

    for c in range(nc):
        o = inter[c] + intra[c]
        for h in range(H):
            oh = o[:, h * hv:(h + 1) * hv]
            ms = jnp.mean(oh * oh, axis=-1, keepdims=True)
            y = oh * lax.rsqrt(ms + RMS_EPS) * gn
            gate = sgg_ref[0, chunk(c), h * hv:(h + 1) * hv].astype(F32)
            o_ref[0, chunk(c), h * hv:(h + 1) * hv] = (y * gate).astype(BF16)


def _gla(gq, gk, gv, la, sgg, gn):
    B, S, _ = gq.shape
    rows = lambda w: pl.BlockSpec((1, GLA_STEP_ROWS, w), lambda b, t: (b, t, 0))
    return pl.pallas_call(
        _gla_kernel,
        grid=(B, S // GLA_STEP_ROWS),
        in_specs=[rows(GLA_DK), rows(GLA_DK), rows(GLA_DV), rows(GLA_DK), rows(GLA_DV),
                  pl.BlockSpec((1, GLA_HEAD_V), lambda b, t: (0, 0))],
        out_specs=rows(GLA_DV),
        out_shape=jax.ShapeDtypeStruct((B, S, GLA_DV), BF16),
        scratch_shapes=[pltpu.VMEM((GLA_HEAD_V, GLA_DK), F32)],
        compiler_params=pltpu.CompilerParams(
            dimension_semantics=("arbitrary", "arbitrary"),
            vmem_limit_bytes=VMEM_LIMIT),
        name="gla",
    )(gq, gk, gv, la, sgg, gn)


def _out_kernel(oat_ref, ob_ref, sga_ref, sgb_ref, x_ref, wpa_ref, wpb_ref, wo_ref, gf_ref,
                y_ref, *, final_norm):
    n_sub, rt = oat_ref.shape[1], oat_ref.shape[3]
    sub = lambda j: slice(j * rt, (j + 1) * rt)
    ya = [_dot_tn(oat_ref[0, j], wpa_ref[...]) for j in range(n_sub)]
    yb = [_dot(ob_ref[0, sub(j), :], wpb_ref[...]) for j in range(n_sub)]
    merged = [(sga_ref[0, sub(j), :].astype(F32) * ya[j]
               + sgb_ref[0, sub(j), :].astype(F32) * yb[j]).astype(BF16) for j in range(n_sub)]
    proj = [_dot(merged[j], wo_ref[...]) for j in range(n_sub)]
    for j in range(n_sub):
        r = x_ref[0, sub(j), :] + proj[j]
        if final_norm:
            ms = jnp.mean(r * r, axis=-1, keepdims=True)
            r = r * lax.rsqrt(ms + RMS_EPS) * gf_ref[...]
        y_ref[0, sub(j), :] = r


def _out_stage(oat, ob, sga, sgb, x, wpa, wpb, wo, gf, final_norm):
    B, S, D = x.shape
    n_sub = OUT_STEP_ROWS // ROW_TILE
    nt = S // OUT_STEP_ROWS
    rt = OUT_STEP_ROWS
    const = lambda shape: pl.BlockSpec(shape, lambda b, t: (0,) * len(shape))
    rows = lambda w: pl.BlockSpec((1, rt, w), lambda b, t: (b, t, 0))
    return pl.pallas_call(
        functools.partial(_out_kernel, final_norm=final_norm),
        grid=(B, nt),
        in_specs=[
            pl.BlockSpec((1, n_sub, MOBA_WIDTH, ROW_TILE), lambda b, t: (b, t, 0, 0)),
            rows(GLA_DV), rows(D), rows(D), rows(D),
            const(wpa.shape), const(wpb.shape), const(wo.shape), const((1, D)),
        ],
        out_specs=rows(D),
        out_shape=jax.ShapeDtypeStruct((B, S, D), F32),
        compiler_params=pltpu.CompilerParams(
            dimension_semantics=("arbitrary", "arbitrary"),
            vmem_limit_bytes=VMEM_LIMIT),
        name="out_stage",
    )(oat, ob, sga, sgb, x, wpa, wpb, wo, gf)


def _rope_tables(S):
    inv_freq = 1.0 / (ROPE_THETA ** (jnp.arange(HALF, dtype=F32) / HALF))
    ang = jnp.arange(S, dtype=F32)[:, None] * inv_freq[None, :]
    return jnp.cos(ang), jnp.sin(ang)


def _swap_half_head(a):
    lead = a.shape[:-1]
    a = a.reshape(*lead, N_PAIRS, 2, 2, HALF)
    return jnp.swapaxes(a, -2, -3).reshape(*lead, MOBA_WIDTH)


def kernel(x, norm_in_g, w_in, b_merge, w_gla_fg2, b_gla_fg, gla_norm_g,
           w_proj_a, w_proj_b, w_out, norm_f_g):
    B, S, D = x.shape
    depth = w_in.shape[0]
    nt = S // MOBA_BLOCK
    cos, sin = _rope_tables(S)
    q_scale = MOBA_HEAD_DIM ** -0.5 * math.log2(math.e)
    cq = (jnp.concatenate([cos, cos], axis=1) * q_scale).T
    sq = (jnp.concatenate([-sin, sin], axis=1) * q_scale).T
    ck = jnp.tile(cos, (1, KX // HALF))
    sk = jnp.tile(sin, (1, KX // HALF)) * jnp.where(jnp.arange(KX) < KX_HI, -1.0, 1.0)[None, :]
    pair_mask = jnp.asarray(
        np.arange(PAIR)[None, :] // MOBA_HEAD_DIM == np.arange(2)[:, None], F32)

    for layer in range(depth):
        w = w_in[layer]
        o = np.cumsum([0, MOBA_WIDTH, MOBA_WIDTH, MOBA_WIDTH, MOBA_WIDTH, GLA_DK, GLA_DK,
                       GLA_DV, GLA_DV, GLA_GATE_RANK, D_MODEL, D_MODEL])
        wq, wk, wv, wmg, wgq, wgk, wgv, wgg, wfg, wga, wgb = [
            w[:, o[i]:o[i + 1]] for i in range(11)]
        wn = jnp.concatenate([_swap_half_head(wk), wgq, wgk, wgv, wgg, wga, wgb],
                             axis=1).astype(BF16)
        wt = jnp.concatenate([wq, wv, wmg, wfg], axis=1).T.astype(BF16)
        wfg2 = w_gla_fg2[layer].astype(BF16)

        (qt, vt, smgt, k, kmean, gq, gk, gv, sgg, la, sga, sgb) = _in_proj(
            x, norm_in_g[layer][None, :], wn, wt, cq, sq, ck, sk, wfg2,
            b_gla_fg[layer][None, :], b_merge[layer])

        km = _swap_half_head(kmean.reshape(B, nt, MOBA_WIDTH))
        km = km.reshape(B, nt, N_PAIRS, PAIR).transpose(0, 2, 1, 3)
        km = (km[:, :, None, :, :] * pair_mask[None, None, :, None, :]).reshape(
            B, N_PAIRS, 2 * nt, PAIR)
        km_hi = km.astype(BF16)
        km_lo = (km - km_hi.astype(F32)).astype(BF16)

        oat = _moba(qt, k, vt, smgt, km_hi, km_lo)
        ob = _gla(gq, gk, gv, la, sgg, gla_norm_g[layer][None, :])
        x = _out_stage(oat, ob, sga, sgb, x, w_proj_a[layer].astype(BF16),
                       w_proj_b[layer].astype(BF16), w_out[layer].astype(BF16),
                       norm_f_g[None, :], final_norm=(layer == depth - 1))
    return x
```

```python
import functools
import math

import jax
import jax.numpy as jnp
import numpy as np
from jax import lax
from jax.experimental import pallas as pl
from jax.experimental.pallas import tpu as pltpu

F32 = jnp.float32
BF16 = jnp.bfloat16

D_MODEL = 1024
MOBA_HEADS = 8
MOBA_HEAD_DIM = 64
MOBA_WIDTH = MOBA_HEADS * MOBA_HEAD_DIM
MOBA_BLOCK = 256
MOBA_TOPK = 3
ROPE_THETA = 10000.0
GLA_HEADS = 4
GLA_DK = 256
GLA_DV = 512
GLA_HEAD_K = 64
GLA_HEAD_V = 128
GLA_GATE_RANK = 16
GLA_GATE_NORM = 16.0
GLA_CHUNK = 64
RMS_EPS = 1e-6
NEG_INF = -1e30

LANES = 128
BF16_ROWS = 16
ROW_TILE = MOBA_BLOCK
PAIR = 2 * MOBA_HEAD_DIM
N_PAIRS = MOBA_HEADS // 2
TAIL_STEP_ROWS = 512
IN_STEP_ROWS = 512
QK_AHEAD = 4
MOBA_Q_TILE = 512
LOOP_BLOCKS = 2
VMEM_LIMIT = 56 * 1024 * 1024

HALF = MOBA_HEAD_DIM // 2
KX = LANES
KX_LO, KX_OH, KX_HI = 0, HALF, 2 * HALF
N_BLOCKS_MAX = BF16_ROWS
VX = MOBA_HEAD_DIM + BF16_ROWS

_C_K = 0
_C_GQ = _C_K + MOBA_WIDTH
_C_GK = _C_GQ + GLA_DK
_C_GV = _C_GK + GLA_DK
_C_GG = _C_GV + GLA_DV
_C_GA = _C_GG + GLA_DV
_C_GB = _C_GA + D_MODEL
_C_END = _C_GB + D_MODEL
_R_Q = 0
_R_V = _R_Q + MOBA_WIDTH
_R_MG = _R_V + MOBA_WIDTH
_R_FG = _R_MG + MOBA_WIDTH
_R_END = _R_FG + GLA_GATE_RANK


def _dot(a, b):
    return jnp.dot(a, b, preferred_element_type=F32)


def _dot_nt(a, b):
    return lax.dot_general(a, b, (((1,), (1,)), ((), ())), preferred_element_type=F32)


def _dot_tn(a, b):
    return lax.dot_general(a, b, (((0,), (0,)), ((), ())), preferred_element_type=F32)


def _sigmoid(x):
    return 1.0 / (1.0 + jnp.exp(-x))


def _silu(x):
    return x * _sigmoid(x)


def _in_proj_kernel(x_ref, g_ref, wn_ref, wt_ref, cq_ref, sq_ref, ck_ref, sk_ref,
                    wfg2_ref, bfg_ref, bm_ref,
                    qt_ref, vt_ref, smgt_ref, k_ref, kmean_ref, gq_ref, gk_ref, gv_ref,
                    sgg_ref, la_ref, sga_ref, sgb_ref):
    t = pl.program_id(1)
    rt = x_ref.shape[1]
    nb = MOBA_BLOCK
    n_sub = rt // nb
    hd = MOBA_HEAD_DIM
    blk_cols = lambda j: slice(j * nb, (j + 1) * nb)
    x = x_ref[0]
    ms = jnp.mean(x * x, axis=-1, keepdims=True)
    hb = (x * lax.rsqrt(ms + RMS_EPS) * g_ref[...]).astype(BF16)

    pt = _dot_nt(wt_ref[...], hb)
    cq = cq_ref[...]
    sq = sq_ref[...]
    ones_rows = jnp.where(lax.broadcasted_iota(jnp.int32, (BF16_ROWS, nb), 0) == 0,
                          1.0, 0.0).astype(BF16)
    for h in range(MOBA_HEADS):
        blk = pt[_R_Q + h * hd:_R_Q + (h + 1) * hd]
        swapped = jnp.concatenate([blk[HALF:], blk[:HALF]], axis=0)
        q_rot = (blk * cq + swapped * sq).astype(BF16)
        v_rows = pt[_R_V + h * hd:_R_V + (h + 1) * hd].astype(BF16)
        for j in range(n_sub):
            qt_ref[0, j, h * hd:(h + 1) * hd, :] = q_rot[:, blk_cols(j)]
            vt_ref[0, j, h * VX:h * VX + hd, :] = v_rows[:, blk_cols(j)]
            vt_ref[0, j, h * VX + hd:(h + 1) * VX, :] = ones_rows
    smg = _silu(pt[_R_MG:_R_FG]).astype(BF16)
    for j in range(n_sub):
        smgt_ref[0, j] = smg[:, blk_cols(j)]

    pn = _dot(hb, wn_ref[...])
    ck = ck_ref[...]
    sk = sk_ref[...]
    lane = lax.broadcasted_iota(jnp.int32, (rt, KX), 1)
    row_block = t * n_sub + lax.broadcasted_iota(jnp.int32, (rt, KX), 0) // nb
    block_onehot = jnp.where(lane == KX_OH + row_block, 1.0, 0.0)
    own_lanes = (lane % (2 * HALF)) < HALF
    for p in range(N_PAIRS):
        blk = pn[:, _C_K + p * PAIR:_C_K + (p + 1) * PAIR]
        kr = blk * ck + pltpu.roll(blk, 2 * HALF, axis=1) * sk
        for j in range(n_sub):
            kmean_ref[0, j, :, p * PAIR:(p + 1) * PAIR] = jnp.mean(
                kr[blk_cols(j)], axis=0, keepdims=True)
        for hh, src in ((0, kr), (1, pltpu.roll(kr, KX - HALF, axis=1))):
            h = 2 * p + hh
            k_ref[0, :, h * KX:(h + 1) * KX] = (
                jnp.where(own_lanes, src, 0.0) + block_onehot).astype(BF16)

    gq_ref[0] = (pn[:, _C_GQ:_C_GK] * (GLA_HEAD_K ** -0.5)).astype(BF16)
    gk_ref[0] = pn[:, _C_GK:_C_GV].astype(BF16)
    gv_ref[0] = pn[:, _C_GV:_C_GG].astype(BF16)
    sgg_ref[0] = _silu(pn[:, _C_GG:_C_GA]).astype(BF16)

    z = _dot_tn(pt[_R_FG:_R_END].astype(BF16), wfg2_ref[...]) + bfg_ref[...]
    log_sig = jnp.minimum(z, 0.0) - jnp.log1p(jnp.exp(-jnp.abs(z)))
    la_ref[0] = log_sig * (1.0 / GLA_GATE_NORM)

    sga_ref[0] = _sigmoid(pn[:, _C_GA:_C_GB] + bm_ref[0:1, :]).astype(BF16)
    sgb_ref[0] = _sigmoid(pn[:, _C_GB:_C_END] + bm_ref[1:2, :]).astype(BF16)


def _in_proj(x, g, wn, wt, cq, sq, ck, sk, wfg2, bfg, bm):
    B, S, D = x.shape
    nt = S // ROW_TILE
    rt = IN_STEP_ROWS
    n_sub = rt // ROW_TILE
    const = lambda shape: pl.BlockSpec(shape, lambda b, t: (0,) * len(shape))
    t_blocked = lambda r: pl.BlockSpec((1, n_sub, r, ROW_TILE), lambda b, t: (b, t, 0, 0))
    rows = lambda w: pl.BlockSpec((1, rt, w), lambda b, t: (b, t, 0))
    t_shape = lambda r: jax.ShapeDtypeStruct((B, nt, r, ROW_TILE), BF16)
    rshape = lambda w, dt=BF16: jax.ShapeDtypeStruct((B, S, w), dt)
    return pl.pallas_call(
        _in_proj_kernel,
        grid=(B, S // rt),
        in_specs=[
            rows(D), const((1, D)), const(wn.shape), const(wt.shape),
            pl.BlockSpec((MOBA_HEAD_DIM, rt), lambda b, t: (0, t)),
            pl.BlockSpec((MOBA_HEAD_DIM, rt), lambda b, t: (0, t)),
            pl.BlockSpec((rt, KX), lambda b, t: (t, 0)),
            pl.BlockSpec((rt, KX), lambda b, t: (t, 0)),
            const(wfg2.shape), const((1, GLA_DK)), const((2, D)),
        ],
        out_specs=[
            t_blocked(MOBA_WIDTH), t_blocked(MOBA_HEADS * VX), t_blocked(MOBA_WIDTH),
            rows(MOBA_HEADS * KX),
            pl.BlockSpec((1, n_sub, 1, MOBA_WIDTH), lambda b, t: (b, t, 0, 0)),
            rows(GLA_DK), rows(GLA_DK), rows(GLA_DV), rows(GLA_DV), rows(GLA_DK),
            rows(D), rows(D),
        ],
        out_shape=[
            t_shape(MOBA_WIDTH), t_shape(MOBA_HEADS * VX), t_shape(MOBA_WIDTH),
            rshape(MOBA_HEADS * KX),
            jax.ShapeDtypeStruct((B, nt, 1, MOBA_WIDTH), F32),
            rshape(GLA_DK), rshape(GLA_DK), rshape(GLA_DV), rshape(GLA_DV),
            rshape(GLA_DK, F32), rshape(D), rshape(D),
        ],
        compiler_params=pltpu.CompilerParams(
            dimension_semantics=("arbitrary", "arbitrary"),
            vmem_limit_bytes=VMEM_LIMIT),
        name="in_proj",
    )(x, g, wn, wt, cq, sq, ck, sk, wfg2, bfg, bm)


def _moba_kernel(qt_ref, k_ref, vt_ref, smgt_ref, kmh_ref, kml_ref, o_ref,
                 qx_ref, s_ref, smax_ref, m_ref, acc_ref):
    nb = MOBA_BLOCK
    hd = MOBA_HEAD_DIM
    n_sub = qt_ref.shape[1]
    qw = n_sub * nb
    n_blocks = k_ref.shape[1] // nb
    assert n_blocks <= N_BLOCKS_MAX
    first_own = pl.program_id(1) * n_sub

    blk_id = lax.broadcasted_iota(jnp.int32, (n_blocks, nb), 0).astype(F32)
    qx_ref[...] = jnp.zeros_like(qx_ref)

    def select_blocks(p):
        for j in range(n_sub):
            own = (first_own + j).astype(F32)
            cols = slice(j * nb, (j + 1) * nb)
            qt = qt_ref[0, j, p * PAIR:(p + 1) * PAIR, :]
            gates = _dot(kmh_ref[0, p], qt) + _dot(kml_ref[0, p], qt)
            for hh in range(2):
                h = 2 * p + hh
                past = blk_id < own
                gh = jnp.where(past, gates[hh * n_blocks:(hh + 1) * n_blocks], NEG_INF)
                picked = jnp.zeros((n_blocks, nb), jnp.bool_)
                for _ in range(MOBA_TOPK):
                    best = jnp.max(gh, axis=0, keepdims=True)
                    first = jnp.min(jnp.where(gh == best, blk_id, n_blocks), axis=0, keepdims=True)
                    hit = blk_id == first
                    picked = picked | hit
                    gh = jnp.where(hit, -jnp.inf, gh)
                keep = (blk_id == own) | (past & picked)
                qx_ref[h, KX_LO:KX_LO + HALF, cols] = qt[hh * hd:hh * hd + HALF]
                qx_ref[h, KX_OH:KX_OH + n_blocks, cols] = (
                    jnp.where(keep, 0.0, NEG_INF).astype(BF16))
                qx_ref[h, KX_HI:KX_HI + HALF, cols] = qt[hh * hd + HALF:(hh + 1) * hd]

    def produce_scores(n, h):
        kb = k_ref[0, pl.ds(pl.multiple_of(n * nb, nb), nb), h * KX:(h + 1) * KX]
        s = _dot(kb, qx_ref[h])
        s_ref[h] = s
        smax_ref[h] = jnp.max(s, axis=0, keepdims=True)

    causal = (lax.broadcasted_iota(jnp.int32, (nb, nb), 0)
              <= lax.broadcasted_iota(jnp.int32, (nb, nb), 1))

    def attend(blocks, diagonal, next_block):
        jobs = [(blocks[0], h) for h in range(QK_AHEAD, MOBA_HEADS)]
        jobs += [(n, h) for n in blocks[1:] for h in range(MOBA_HEADS)]
        if next_block is not None:
            jobs += [(next_block, h) for h in range(QK_AHEAD)]

        def issue():
            if jobs:
                produce_scores(*jobs.pop(0))

        issue()
        issue()
        for n, diag in zip(blocks, diagonal):
            for h in range(MOBA_HEADS):
                s = s_ref[h]
                smax = smax_ref[h]
                if diag is not None:
                    own_cols = slice(diag * nb, (diag + 1) * nb)
                    tri = jnp.where(causal, s[:, own_cols], NEG_INF)
                    parts = [s[:, :own_cols.start], tri, s[:, own_cols.stop:]]
                    maxes = [smax[:, :own_cols.start], jnp.max(tri, axis=0, keepdims=True),
                             smax[:, own_cols.stop:]]
                    s = jnp.concatenate([x for x in parts if x.shape[1]], axis=1)
                    smax = jnp.concatenate([x for x in maxes if x.shape[1]], axis=1)
                vt = vt_ref[0, n, h * VX:(h + 1) * VX, :]
                m_old = m_ref[h]
                m_new = jnp.maximum(m_old, smax)
                m_ref[h] = m_new
                acc_ref[h] = (jnp.exp2(m_old - m_new) * acc_ref[h]
                              + _dot(vt, jnp.exp2(s - m_new).astype(BF16)))
                issue()

    m_ref[...] = jnp.full(m_ref.shape, NEG_INF, F32)
    acc_ref[...] = jnp.zeros_like(acc_ref)
    for p in range(N_PAIRS):
        select_blocks(p)
        if 2 * p + 2 == QK_AHEAD:
            for h in range(QK_AHEAD):
                produce_scores(0, h)

    assert n_sub % LOOP_BLOCKS == 0

    def body(i, c):
        n = i * LOOP_BLOCKS
        attend([n + d for d in range(LOOP_BLOCKS)], [None] * LOOP_BLOCKS, n + LOOP_BLOCKS)
        return c

    lax.fori_loop(0, first_own // LOOP_BLOCKS, body, 0)
    attend([first_own + j for j in range(n_sub)], list(range(n_sub)), None)
    for h in range(MOBA_HEADS):
        o = acc_ref[h, 0:hd, :] / acc_ref[h, hd:hd + 1, :]
        for j in range(n_sub):
            gate = smgt_ref[0, j, h * hd:(h + 1) * hd, :].astype(F32)
            o_ref[0, j, h * hd:(h + 1) * hd, :] = (o[:, j * nb:(j + 1) * nb] * gate).astype(BF16)


def _moba(qt, k, vt, smgt, kmh, kml):
    B, nt, W, rt = qt.shape
    S = k.shape[1]
    n_sub = MOBA_Q_TILE // rt
    qw = MOBA_Q_TILE
    tile = pl.BlockSpec((1, n_sub, W, rt), lambda b, t: (b, t, 0, 0))
    km = pl.BlockSpec((1, N_PAIRS, 2 * nt, PAIR), lambda b, t: (b, 0, 0, 0))
    return pl.pallas_call(
        _moba_kernel,
        grid=(B, nt // n_sub),
        in_specs=[
            tile,
            pl.BlockSpec((1, S, MOBA_HEADS * KX), lambda b, t: (b, 0, 0)),
            pl.BlockSpec((1, nt, MOBA_HEADS * VX, rt), lambda b, t: (b, 0, 0, 0)),
            tile, km, km,
        ],
        out_specs=tile,
        out_shape=jax.ShapeDtypeStruct((B, nt, W, rt), BF16),
        scratch_shapes=[
            pltpu.VMEM((MOBA_HEADS, KX, qw), BF16),
            pltpu.VMEM((MOBA_HEADS, MOBA_BLOCK, qw), F32),
            pltpu.VMEM((MOBA_HEADS, 1, qw), F32),
            pltpu.VMEM((MOBA_HEADS, 1, qw), F32),
            pltpu.VMEM((MOBA_HEADS, VX, qw), F32),
        ],
        compiler_params=pltpu.CompilerParams(
            dimension_semantics=("arbitrary", "arbitrary"),
            vmem_limit_bytes=VMEM_LIMIT),
        name="moba",
    )(qt, k, vt, smgt, kmh, kml)


def _gla_out_kernel(q_ref, k_ref, v_ref, la_ref, sgg_ref, gn_ref,
                    oat_ref, sga_ref, sgb_ref, x_ref, wpa_ref, wpb_ref, wo_ref, gf_ref,
                    y_ref, st_ref, *, final_norm):
    C = GLA_CHUNK
    hk, hv = GLA_HEAD_K, GLA_HEAD_V
    n_sub, rt = oat_ref.shape[1], oat_ref.shape[3]
    sub = lambda j: slice(j * rt, (j + 1) * rt)

    @pl.when(pl.program_id(1) == 0)
    def _():
        st_ref[...] = jnp.zeros_like(st_ref)

    nc = TAIL_STEP_ROWS // C
    H = GLA_HEADS
    tril_b = (lax.broadcasted_iota(jnp.int32, (C, C), 0)
              >= lax.broadcasted_iota(jnp.int32, (C, C), 1)).astype(BF16)
    tril_stack = (lax.broadcasted_iota(jnp.int32, (H * C, C), 0) % C
                  >= lax.broadcasted_iota(jnp.int32, (H * C, C), 1))
    lane_head = lax.broadcasted_iota(jnp.int32, (C, GLA_DK), 1) // hk
    zero_b = jnp.zeros((C, GLA_DK), BF16)
    gn = gn_ref[...]

    def stack_heads(x):
        return jnp.concatenate([jnp.where(lane_head == h, x, zero_b) for h in range(H)], axis=0)

    chunk = lambda c: slice(c * C, (c + 1) * C)
    g = la_ref[0]
    g_hi = g.astype(BF16)
    g_lo = (g - g_hi.astype(F32)).astype(BF16)
    b = [_dot(tril_b, g_hi[chunk(c)]) + _dot(tril_b, g_lo[chunk(c)]) for c in range(nc)]

    ya = [_dot_tn(oat_ref[0, j], wpa_ref[...]) for j in range(n_sub)]

    qe_s, qb_s, ke_b, kl_s, decay = [], [], [], [], []
    for c in range(nc):
        b_mid = b[c][C // 2 - 1:C // 2]
        b_last = b[c][C - 1:C]
        qe = q_ref[0, chunk(c), :].astype(F32) * jnp.exp(b[c] - b_mid)
        ke = k_ref[0, chunk(c), :].astype(F32) * jnp.exp(b_mid - b[c])
        qe_s.append(stack_heads(qe.astype(BF16)))
        qb_s.append(stack_heads((qe * jnp.exp(b_mid)).astype(BF16)))
        kl_s.append(stack_heads((ke * jnp.exp(b_last - b_mid)).astype(BF16)))
        ke_b.append(ke.astype(BF16))
        decay.append(jnp.exp(b_last))

    attn = [jnp.where(tril_stack, _dot_nt(qe_s[c], ke_b[c]), 0.0).astype(BF16) for c in range(nc)]
    intra = [jnp.concatenate(
        [_dot(attn[c][h * C:(h + 1) * C], v_ref[0, chunk(c), h * hv:(h + 1) * hv])
         for h in range(H)], axis=1) for c in range(nc)]

    v_s = [jnp.concatenate([v_ref[0, chunk(c), h * hv:(h + 1) * hv] for h in range(H)], axis=0)
           for c in range(nc)]
    upd = [_dot_tn(v_s[c], kl_s[c]) for c in range(nc)]

    st = st_ref[...]
    inter = []
    for c in range(nc):
        o_s = _dot_nt(qb_s[c], st.astype(BF16))
        inter.append(jnp.concatenate([o_s[h * C:(h + 1) * C] for h in range(H)], axis=1))
        st = st * decay[c] + upd[c]
    st_ref[...] = st

    ob = []
    for c in range(nc):
        o = inter[c] + intra[c]
        heads = []
        for h in range(H):
            oh = o[:, h * hv:(h + 1) * hv]
            ms = jnp.mean(oh * oh, axis=-1, keepdims=True)
            y = oh * lax.rsqrt(ms + RMS_EPS) * gn
            gate = sgg_ref[0, chunk(c), h * hv:(h + 1) * hv].astype(F32)
            heads.append((y * gate).astype(BF16))
        ob.append(jnp.concatenate(heads, axis=1))

    per_sub = rt // C
    yb = [_dot(jnp.concatenate(ob[j * per_sub:(j + 1) * per_sub], axis=0), wpb_ref[...])
          for j in range(n_sub)]
    merged = [(sga_ref[0, sub(j), :].astype(F32) * ya[j]
               + sgb_ref[0, sub(j), :].astype(F32) * yb[j]).astype(BF16) for j in range(n_sub)]
    proj = [_dot(merged[j], wo_ref[...]) for j in range(n_sub)]
    for j in range(n_sub):
        r = x_ref[0, sub(j), :] + proj[j]
        if final_norm:
            ms = jnp.mean(r * r, axis=-1, keepdims=True)
            r = r * lax.rsqrt(ms + RMS_EPS) * gf_ref[...]
        y_ref[0, sub(j), :] = r


def _gla_out(gq, gk, gv, la, sgg, gn, oat, sga, sgb, x, wpa, wpb, wo, gf, final_norm):
    B, S, D = x.shape
    rt = TAIL_STEP_ROWS
    n_sub = rt // ROW_TILE
    const = lambda shape: pl.BlockSpec(shape, lambda b, t: (0,) * len(shape))
    rows = lambda w: pl.BlockSpec((1, rt, w), lambda b, t: (b, t, 0))
    return pl.pallas_call(
        functools.partial(_gla_out_kernel, final_norm=final_norm),
        grid=(B, S // rt),
        in_specs=[
            rows(GLA_DK), rows(GLA_DK), rows(GLA_DV), rows(GLA_DK), rows(GLA_DV),
            const((1, GLA_HEAD_V)),
            pl.BlockSpec((1, n_sub, MOBA_WIDTH, ROW_TILE), lambda b, t: (b, t, 0, 0)),
            rows(D), rows(D), rows(D),
            const(wpa.shape), const(wpb.shape), const(wo.shape), const((1, D)),
        ],
        out_specs=rows(D),
        out_shape=jax.ShapeDtypeStruct((B, S, D), F32),
        scratch_shapes=[pltpu.VMEM((GLA_HEAD_V, GLA_DK), F32)],
        compiler_params=pltpu.CompilerParams(
            dimension_semantics=("arbitrary", "arbitrary"),
            vmem_limit_bytes=VMEM_LIMIT),
        name="gla_out",
    )(gq, gk, gv, la, sgg, gn, oat, sga, sgb, x, wpa, wpb, wo, gf)


def _rope_tables(S):
    inv_freq = 1.0 / (ROPE_THETA ** (jnp.arange(HALF, dtype=F32) / HALF))
    ang = jnp.arange(S, dtype=F32)[:, None] * inv_freq[None, :]
    return jnp.cos(ang), jnp.sin(ang)


def _swap_half_head(a):
    lead = a.shape[:-1]
    a = a.reshape(*lead, N_PAIRS, 2, 2, HALF)
    return jnp.swapaxes(a, -2, -3).reshape(*lead, MOBA_WIDTH)


def kernel(x, norm_in_g, w_in, b_merge, w_gla_fg2, b_gla_fg, gla_norm_g,
           w_proj_a, w_proj_b, w_out, norm_f_g):
    B, S, D = x.shape
    depth = w_in.shape[0]
    nt = S // MOBA_BLOCK
    cos, sin = _rope_tables(S)
    q_scale = MOBA_HEAD_DIM ** -0.5 * math.log2(math.e)
    cq = (jnp.concatenate([cos, cos], axis=1) * q_scale).T
    sq = (jnp.concatenate([-sin, sin], axis=1) * q_scale).T
    ck = jnp.tile(cos, (1, KX // HALF))
    sk = jnp.tile(sin, (1, KX // HALF)) * jnp.where(jnp.arange(KX) < KX_HI, -1.0, 1.0)[None, :]
    pair_mask = jnp.asarray(
        np.arange(PAIR)[None, :] // MOBA_HEAD_DIM == np.arange(2)[:, None], F32)

    for layer in range(depth):
        w = w_in[layer]
        o = np.cumsum([0, MOBA_WIDTH, MOBA_WIDTH, MOBA_WIDTH, MOBA_WIDTH, GLA_DK, GLA_DK,
                       GLA_DV, GLA_DV, GLA_GATE_RANK, D_MODEL, D_MODEL])
        wq, wk, wv, wmg, wgq, wgk, wgv, wgg, wfg, wga, wgb = [
            w[:, o[i]:o[i + 1]] for i in range(11)]
        wn = jnp.concatenate([_swap_half_head(wk), wgq, wgk, wgv, wgg, wga, wgb],
                             axis=1).astype(BF16)
        wt = jnp.concatenate([wq, wv, wmg, wfg], axis=1).T.astype(BF16)
        wfg2 = w_gla_fg2[layer].astype(BF16)

        (qt, vt, smgt, k, kmean, gq, gk, gv, sgg, la, sga, sgb) = _in_proj(
            x, norm_in_g[layer][None, :], wn, wt, cq, sq, ck, sk, wfg2,
            b_gla_fg[layer][None, :], b_merge[layer])

        km = _swap_half_head(kmean.reshape(B, nt, MOBA_WIDTH))
        km = km.reshape(B, nt, N_PAIRS, PAIR).transpose(0, 2, 1, 3)
        km = (km[:, :, None, :, :] * pair_mask[None, None, :, None, :]).reshape(
            B, N_PAIRS, 2 * nt, PAIR)
        km_hi = km.astype(BF16)
        km_lo = (km - km_hi.astype(F32)).astype(BF16)

        oat = _moba(qt, k, vt, smgt, km_hi, km_lo)
        x = _gla_out(gq, gk, gv, la, sgg, gla_norm_g[layer][None, :], oat, sga, sgb, x,
                     w_proj_a[layer].astype(BF16), w_proj_b[layer].astype(BF16),
                     w_out[layer].astype(BF16), norm_f_g[None, :],
                     final_norm=(layer == depth - 1))
    return x
```

```python
import functools
import math

import jax
import jax.numpy as jnp
import numpy as np
from jax import lax
from jax.experimental import pallas as pl
from jax.experimental.pallas import tpu as pltpu

F32 = jnp.float32
BF16 = jnp.bfloat16

D_MODEL = 1024
MOBA_HEADS = 8
MOBA_HEAD_DIM = 64
MOBA_WIDTH = MOBA_HEADS * MOBA_HEAD_DIM
MOBA_BLOCK = 256
MOBA_TOPK = 3
ROPE_THETA = 10000.0
GLA_HEADS = 4
GLA_DK = 256
GLA_DV = 512
GLA_HEAD_K = 64
GLA_HEAD_V = 128
GLA_GATE_RANK = 16
GLA_GATE_NORM = 16.0
GLA_CHUNK = 64
RMS_EPS = 1e-6
NEG_INF = -1e30

LANES = 128
BF16_ROWS = 16
ROW_TILE = MOBA_BLOCK
PAIR = 2 * MOBA_HEAD_DIM
N_PAIRS = MOBA_HEADS // 2
TAIL_STEP_ROWS = 512
IN_STEP_ROWS = 512
QK_AHEAD = 4
MOBA_Q_TILE = 512
LOOP_BLOCKS = 2
VMEM_LIMIT = 56 * 1024 * 1024

HALF = MOBA_HEAD_DIM // 2
KX = LANES
KX_LO, KX_OH, KX_HI = 0, HALF, 2 * HALF
N_BLOCKS_MAX = BF16_ROWS
VX = MOBA_HEAD_DIM + BF16_ROWS

_C_K = 0
_C_GQ = _C_K + MOBA_WIDTH
_C_GK = _C_GQ + GLA_DK
_C_GV = _C_GK + GLA_DK
_C_GG = _C_GV + GLA_DV
_C_GA = _C_GG + GLA_DV
_C_GB = _C_GA + D_MODEL
_C_END = _C_GB + D_MODEL
_R_Q = 0
_R_V = _R_Q + MOBA_WIDTH
_R_MG = _R_V + MOBA_WIDTH
_R_FG = _R_MG + MOBA_WIDTH
_R_END = _R_FG + GLA_GATE_RANK


def _dot(a, b):
    return jnp.dot(a, b, preferred_element_type=F32)


def _dot_nt(a, b):
    return lax.dot_general(a, b, (((1,), (1,)), ((), ())), preferred_element_type=F32)


def _dot_tn(a, b):
    return lax.dot_general(a, b, (((0,), (0,)), ((), ())), preferred_element_type=F32)


def _sigmoid(x):
    return 1.0 / (1.0 + jnp.exp(-x))


def _silu(x):
    return x * _sigmoid(x)


def _select_blocks(gates, first_own, n_blocks, nb):
    H = MOBA_HEADS
    qw = gates.shape[1]
    col_own = (first_own + lax.broadcasted_iota(jnp.int32, (1, qw), 1) // nb).astype(F32)
    past = [col_own > n for n in range(n_blocks)]
    g = [jnp.where(past[n], gates[n * H:(n + 1) * H], NEG_INF) for n in range(n_blocks)]
    picked = [jnp.zeros((H, qw), jnp.bool_) for _ in range(n_blocks)]
    for _ in range(MOBA_TOPK):
        best = functools.reduce(jnp.maximum, g)
        first = functools.reduce(jnp.minimum, [
            jnp.where(g[n] == best, float(n), float(n_blocks)) for n in range(n_blocks)])
        for n in range(n_blocks):
            hit = first == float(n)
            picked[n] = picked[n] | hit
            g[n] = jnp.where(hit, -jnp.inf, g[n])
    keep = jnp.concatenate(
        [jnp.where((col_own == n) | (past[n] & picked[n]), 1.0, 0.0) for n in range(n_blocks)],
        axis=0).astype(BF16)
    r_out = lax.broadcasted_iota(jnp.int32, (H * n_blocks, n_blocks * H), 0)
    r_in = lax.broadcasted_iota(jnp.int32, (H * n_blocks, n_blocks * H), 1)
    perm = ((r_out // n_blocks == r_in % H) & (r_out % n_blocks == r_in // H)).astype(BF16)
    return jnp.where(_dot(perm, keep) > 0.5, 0.0, NEG_INF).astype(BF16)


def _in_proj_kernel(x_ref, g_ref, wn_ref, wt_ref, cq_ref, sq_ref, ck_ref, sk_ref,
                    wfg2_ref, bfg_ref, bm_ref,
                    qx_ref, vt_ref, smgt_ref, k_ref, gq_ref, gk_ref, gv_ref,
                    sgg_ref, la_ref, sga_ref, sgb_ref, km_ref, *, n_blocks):
    t = pl.program_id(1)
    rt = x_ref.shape[1]
    nb = MOBA_BLOCK
    n_sub = rt // nb
    hd = MOBA_HEAD_DIM
    H = MOBA_HEADS
    assert n_blocks <= N_BLOCKS_MAX
    blk_cols = lambda j: slice(j * nb, (j + 1) * nb)

    @pl.when(t == 0)
    def _():
        km_ref[...] = jnp.zeros_like(km_ref)

    x = x_ref[0]
    ms = jnp.mean(x * x, axis=-1, keepdims=True)
    hb = (x * lax.rsqrt(ms + RMS_EPS) * g_ref[...]).astype(BF16)

    pt = _dot_nt(wt_ref[...], hb)
    cq = cq_ref[...]
    sq = sq_ref[...]
    ones_rows = jnp.where(lax.broadcasted_iota(jnp.int32, (BF16_ROWS, nb), 0) == 0,
                          1.0, 0.0).astype(BF16)
    q_rot = []
    for h in range(H):
        blk = pt[_R_Q + h * hd:_R_Q + (h + 1) * hd]
        swapped = jnp.concatenate([blk[HALF:], blk[:HALF]], axis=0)
        q_rot.append((blk * cq + swapped * sq).astype(BF16))
        v_rows = pt[_R_V + h * hd:_R_V + (h + 1) * hd].astype(BF16)
        for j in range(n_sub):
            vt_ref[0, j, h * VX:h * VX + hd, :] = v_rows[:, blk_cols(j)]
            vt_ref[0, j, h * VX + hd:(h + 1) * VX, :] = ones_rows

    smg = _silu(pt[_R_MG:_R_FG]).astype(BF16)
    for j in range(n_sub):
        smgt_ref[0, j] = smg[:, blk_cols(j)]

    pk = _dot(hb, wn_ref[:, _C_K:_C_GQ])
    pg = _dot(hb, wn_ref[:, _C_GQ:_C_GA])
    ck = ck_ref[...]
    sk = sk_ref[...]
    lane = lax.broadcasted_iota(jnp.int32, (rt, KX), 1)
    row_block = t * n_sub + lax.broadcasted_iota(jnp.int32, (rt, KX), 0) // nb
    block_onehot = jnp.where(lane == KX_OH + row_block, 1.0, 0.0)
    own_lanes = (lane % (2 * HALF)) < HALF
    k_means = [[] for _ in range(n_sub)]
    for p in range(N_PAIRS):
        blk = pk[:, p * PAIR:(p + 1) * PAIR]
        kr = blk * ck + pltpu.roll(blk, 2 * HALF, axis=1) * sk
        for j in range(n_sub):
            k_means[j].append(jnp.mean(kr[blk_cols(j)], axis=0, keepdims=True))
        for hh, src in ((0, kr), (1, pltpu.roll(kr, KX - HALF, axis=1))):
            h = 2 * p + hh
            k_ref[0, :, h * KX:(h + 1) * KX] = (
                jnp.where(own_lanes, src, 0.0) + block_onehot).astype(BF16)
    km_lane = lax.broadcasted_iota(jnp.int32, (H, MOBA_WIDTH), 1)
    km_head = 2 * (km_lane // PAIR) + (km_lane // HALF) % 2
    head_lanes = km_head == lax.broadcasted_iota(jnp.int32, (H, MOBA_WIDTH), 0)
    for j in range(n_sub):
        mean_row = jnp.concatenate(k_means[j], axis=1)
        km_ref[pl.ds(pl.multiple_of((t * n_sub + j) * H, H), H), :] = jnp.where(
            head_lanes, mean_row, 0.0)

    q_pair = jnp.concatenate(
        [q_rot[2 * p + hh][part * HALF:(part + 1) * HALF]
         for p in range(N_PAIRS) for part in range(2) for hh in range(2)], axis=0)
    km = km_ref[...]
    km_hi = km.astype(BF16)
    km_lo = (km - km_hi.astype(F32)).astype(BF16)
    gates = _dot(km_hi, q_pair) + _dot(km_lo, q_pair)

    pm = _dot(hb, wn_ref[:, _C_GA:_C_END])

    mask_rows = _select_blocks(gates, t * n_sub, n_blocks, nb)
    pad = jnp.zeros((KX_HI - KX_OH - n_blocks, rt), BF16)
    tail = jnp.zeros((KX - KX_HI - HALF, rt), BF16)
    for h in range(H):
        qx_ref[0, h] = jnp.concatenate(
            [q_rot[h][:HALF], mask_rows[h * n_blocks:(h + 1) * n_blocks], pad,
             q_rot[h][HALF:], tail], axis=0)

    gq_ref[0] = (pg[:, _C_GQ - _C_GQ:_C_GK - _C_GQ] * (GLA_HEAD_K ** -0.5)).astype(BF16)
    gk_ref[0] = pg[:, _C_GK - _C_GQ:_C_GV - _C_GQ].astype(BF16)
    gv_ref[0] = pg[:, _C_GV - _C_GQ:_C_GG - _C_GQ].astype(BF16)
    sgg_ref[0] = _silu(pg[:, _C_GG - _C_GQ:_C_GA - _C_GQ]).astype(BF16)

    z = _dot_tn(pt[_R_FG:_R_END].astype(BF16), wfg2_ref[...]) + bfg_ref[...]
    log_sig = jnp.minimum(z, 0.0) - jnp.log1p(jnp.exp(-jnp.abs(z)))
    la_ref[0] = log_sig * (1.0 / GLA_GATE_NORM)

    sga_ref[0] = _sigmoid(pm[:, :D_MODEL] + bm_ref[0:1, :]).astype(BF16)
    sgb_ref[0] = _sigmoid(pm[:, D_MODEL:] + bm_ref[1:2, :]).astype(BF16)


def _in_proj(x, g, wn, wt, cq, sq, ck, sk, wfg2, bfg, bm):
    B, S, D = x.shape
    nt = S // ROW_TILE
    rt = IN_STEP_ROWS
    n_sub = rt // ROW_TILE
    const = lambda shape: pl.BlockSpec(shape, lambda b, t: (0,) * len(shape))
    t_blocked = lambda r: pl.BlockSpec((1, n_sub, r, ROW_TILE), lambda b, t: (b, t, 0, 0))
    rows = lambda w: pl.BlockSpec((1, rt, w), lambda b, t: (b, t, 0))
    t_shape = lambda r: jax.ShapeDtypeStruct((B, nt, r, ROW_TILE), BF16)
    rshape = lambda w, dt=BF16: jax.ShapeDtypeStruct((B, S, w), dt)
    return pl.pallas_call(
        functools.partial(_in_proj_kernel, n_blocks=nt),
        grid=(B, S // rt),
        in_specs=[
            rows(D), const((1, D)), const(wn.shape), const(wt.shape),
            pl.BlockSpec((MOBA_HEAD_DIM, rt), lambda b, t: (0, t)),
            pl.BlockSpec((MOBA_HEAD_DIM, rt), lambda b, t: (0, t)),
            pl.BlockSpec((rt, KX), lambda b, t: (t, 0)),
            pl.BlockSpec((rt, KX), lambda b, t: (t, 0)),
            const(wfg2.shape), const((1, GLA_DK)), const((2, D)),
        ],
        out_specs=[
            pl.BlockSpec((1, MOBA_HEADS, KX, rt), lambda b, t: (b, 0, 0, t)),
            t_blocked(MOBA_HEADS * VX), t_blocked(MOBA_WIDTH),
            rows(MOBA_HEADS * KX),
            rows(GLA_DK), rows(GLA_DK), rows(GLA_DV), rows(GLA_DV), rows(GLA_DK),
            rows(D), rows(D),
        ],
        out_shape=[
            jax.ShapeDtypeStruct((B, MOBA_HEADS, KX, S), BF16),
            t_shape(MOBA_HEADS * VX), t_shape(MOBA_WIDTH),
            rshape(MOBA_HEADS * KX),
            rshape(GLA_DK), rshape(GLA_DK), rshape(GLA_DV), rshape(GLA_DV),
            rshape(GLA_DK, F32), rshape(D), rshape(D),
        ],
        scratch_shapes=[pltpu.VMEM((nt * MOBA_HEADS, MOBA_WIDTH), F32)],
        compiler_params=pltpu.CompilerParams(
            dimension_semantics=("arbitrary", "arbitrary"),
            vmem_limit_bytes=VMEM_LIMIT),
        name="in_proj",
    )(x, g, wn, wt, cq, sq, ck, sk, wfg2, bfg, bm)


def _moba_kernel(qx_ref, k_ref, vt_ref, smgt_ref, o_ref, s_ref, smax_ref, m_ref, acc_ref):
    nb = MOBA_BLOCK
    hd = MOBA_HEAD_DIM
    n_sub = smgt_ref.shape[1]
    first_own = pl.program_id(1) * n_sub

    def produce_scores(n, h):
        kb = k_ref[0, pl.ds(pl.multiple_of(n * nb, nb), nb), h * KX:(h + 1) * KX]
        s = _dot(kb, qx_ref[0, h])
        s_ref[h] = s
        smax_ref[h] = jnp.max(s, axis=0, keepdims=True)

    causal = (lax.broadcasted_iota(jnp.int32, (nb, nb), 0)
              <= lax.broadcasted_iota(jnp.int32, (nb, nb), 1))

    def attend(blocks, diagonal, next_block):
        jobs = [(blocks[0], h) for h in range(QK_AHEAD, MOBA_HEADS)]
        jobs += [(n, h) for n in blocks[1:] for h in range(MOBA_HEADS)]
        if next_block is not None:
            jobs += [(next_block, h) for h in range(QK_AHEAD)]

        def issue():
            if jobs:
                produce_scores(*jobs.pop(0))

        issue()
        issue()
        for n, diag in zip(blocks, diagonal):
            for h in range(MOBA_HEADS):
                s = s_ref[h]
                smax = smax_ref[h]
                if diag is not None:
                    own_cols = slice(diag * nb, (diag + 1) * nb)
                    tri = jnp.where(causal, s[:, own_cols], NEG_INF)
                    parts = [s[:, :own_cols.start], tri, s[:, own_cols.stop:]]
                    maxes = [smax[:, :own_cols.start], jnp.max(tri, axis=0, keepdims=True),
                             smax[:, own_cols.stop:]]
                    s = jnp.concatenate([x for x in parts if x.shape[1]], axis=1)
                    smax = jnp.concatenate([x for x in maxes if x.shape[1]], axis=1)
                vt = vt_ref[0, n, h * VX:(h + 1) * VX, :]
                m_old = m_ref[h]
                m_new = jnp.maximum(m_old, smax)
                m_ref[h] = m_new
                acc_ref[h] = (jnp.exp2(m_old - m_new) * acc_ref[h]
                              + _dot(vt, jnp.exp2(s - m_new).astype(BF16)))
                issue()

    m_ref[...] = jnp.full(m_ref.shape, NEG_INF, F32)
    acc_ref[...] = jnp.zeros_like(acc_ref)
    for h in range(QK_AHEAD):
        produce_scores(0, h)

    assert n_sub % LOOP_BLOCKS == 0

    def body(i, c):
        n = i * LOOP_BLOCKS
        attend([n + d for d in range(LOOP_BLOCKS)], [None] * LOOP_BLOCKS, n + LOOP_BLOCKS)
        return c

    lax.fori_loop(0, first_own // LOOP_BLOCKS, body, 0)
    attend([first_own + j for j in range(n_sub)], list(range(n_sub)), None)
    for h in range(MOBA_HEADS):
        o = acc_ref[h, 0:hd, :] / acc_ref[h, hd:hd + 1, :]
        for j in range(n_sub):
            gate = smgt_ref[0, j, h * hd:(h + 1) * hd, :].astype(F32)
            o_ref[0, j, h * hd:(h + 1) * hd, :] = (o[:, j * nb:(j + 1) * nb] * gate).astype(BF16)


def _moba(qx, k, vt, smgt):
    B, nt, W, rt = smgt.shape
    S = k.shape[1]
    n_sub = MOBA_Q_TILE // rt
    qw = MOBA_Q_TILE
    tile = pl.BlockSpec((1, n_sub, W, rt), lambda b, t: (b, t, 0, 0))
    return pl.pallas_call(
        _moba_kernel,
        grid=(B, nt // n_sub),
        in_specs=[
            pl.BlockSpec((1, MOBA_HEADS, KX, qw), lambda b, t: (b, 0, 0, t)),
            pl.BlockSpec((1, S, MOBA_HEADS * KX), lambda b, t: (b, 0, 0)),
            pl.BlockSpec((1, nt, MOBA_HEADS * VX, rt), lambda b, t: (b, 0, 0, 0)),
            tile,
        ],
        out_specs=tile,
        out_shape=jax.ShapeDtypeStruct((B, nt, W, rt), BF16),
        scratch_shapes=[
            pltpu.VMEM((MOBA_HEADS, MOBA_BLOCK, qw), F32),
            pltpu.VMEM((MOBA_HEADS, 1, qw), F32),
            pltpu.VMEM((MOBA_HEADS, 1, qw), F32),
            pltpu.VMEM((MOBA_HEADS, VX, qw), F32),
        ],
        compiler_params=pltpu.CompilerParams(
            dimension_semantics=("arbitrary", "arbitrary"),
            vmem_limit_bytes=VMEM_LIMIT),
        name="moba",
    )(qx, k, vt, smgt)


def _gla_out_kernel(q_ref, k_ref, v_ref, la_ref, sgg_ref, gn_ref,
                    oat_ref, sga_ref, sgb_ref, x_ref, wpa_ref, wpb_ref, wo_ref, gf_ref,
                    y_ref, st_ref, *, final_norm):
    C = GLA_CHUNK
    hk, hv = GLA_HEAD_K, GLA_HEAD_V
    n_sub, rt = oat_ref.shape[1], oat_ref.shape[3]
    sub = lambda j: slice(j * rt, (j + 1) * rt)

    @pl.when(pl.program_id(1) == 0)
    def _():
        st_ref[...] = jnp.zeros_like(st_ref)

    nc = TAIL_STEP_ROWS // C
    H = GLA_HEADS
    tril_b = (lax.broadcasted_iota(jnp.int32, (C, C), 0)
              >= lax.broadcasted_iota(jnp.int32, (C, C), 1)).astype(BF16)
    tril_stack = (lax.broadcasted_iota(jnp.int32, (H * C, C), 0) % C
                  >= lax.broadcasted_iota(jnp.int32, (H * C, C), 1))
    lane_head = lax.broadcasted_iota(jnp.int32, (C, GLA_DK), 1) // hk
    zero_b = jnp.zeros((C, GLA_DK), BF16)
    gn = gn_ref[...]

    def stack_heads(x):
        return jnp.concatenate([jnp.where(lane_head == h, x, zero_b) for h in range(H)], axis=0)

    chunk = lambda c: slice(c * C, (c + 1) * C)
    g = la_ref[0]
    g_hi = g.astype(BF16)
    g_lo = (g - g_hi.astype(F32)).astype(BF16)
    b = [_dot(tril_b, g_hi[chunk(c)]) + _dot(tril_b, g_lo[chunk(c)]) for c in range(nc)]

    ya = [_dot_tn(oat_ref[0, j], wpa_ref[...]) for j in range(n_sub)]

    qe_s, qb_s, ke_b, kl_s, decay = [], [], [], [], []
    for c in range(nc):
        b_mid = b[c][C // 2 - 1:C // 2]
        b_last = b[c][C - 1:C]
        qe = q_ref[0, chunk(c), :].astype(F32) * jnp.exp(b[c] - b_mid)
        ke = k_ref[0, chunk(c), :].astype(F32) * jnp.exp(b_mid - b[c])
        qe_s.append(stack_heads(qe.astype(BF16)))
        qb_s.append(stack_heads((qe * jnp.exp(b_mid)).astype(BF16)))
        kl_s.append(stack_heads((ke * jnp.exp(b_last - b_mid)).astype(BF16)))
        ke_b.append(ke.astype(BF16))
        decay.append(jnp.exp(b_last))

    attn = [jnp.where(tril_stack, _dot_nt(qe_s[c], ke_b[c]), 0.0).astype(BF16) for c in range(nc)]
    intra = [jnp.concatenate(
        [_dot(attn[c][h * C:(h + 1) * C], v_ref[0, chunk(c), h * hv:(h + 1) * hv])
         for h in range(H)], axis=1) for c in range(nc)]

    v_s = [jnp.concatenate([v_ref[0, chunk(c), h * hv:(h + 1) * hv] for h in range(H)], axis=0)
           for c in range(nc)]
    upd = [_dot_tn(v_s[c], kl_s[c]) for c in range(nc)]

    st = st_ref[...]
    inter = []
    for c in range(nc):
        o_s = _dot_nt(qb_s[c], st.astype(BF16))
        inter.append(jnp.concatenate([o_s[h * C:(h + 1) * C] for h in range(H)], axis=1))
        st = st * decay[c] + upd[c]
    st_ref[...] = st

    ob = []
    for c in range(nc):
        o = inter[c] + intra[c]
        heads = []
        for h in range(H):
            oh = o[:, h * hv:(h + 1) * hv]
            ms = jnp.mean(oh * oh, axis=-1, keepdims=True)
            y = oh * lax.rsqrt(ms + RMS_EPS) * gn
            gate = sgg_ref[0, chunk(c), h * hv:(h + 1) * hv].astype(F32)
            heads.append((y * gate).astype(BF16))
        ob.append(jnp.concatenate(heads, axis=1))

    per_sub = rt // C
    yb = [_dot(jnp.concatenate(ob[j * per_sub:(j + 1) * per_sub], axis=0), wpb_ref[...])
          for j in range(n_sub)]
    merged = [(sga_ref[0, sub(j), :].astype(F32) * ya[j]
               + sgb_ref[0, sub(j), :].astype(F32) * yb[j]).astype(BF16) for j in range(n_sub)]
    proj = [_dot(merged[j], wo_ref[...]) for j in range(n_sub)]
    for j in range(n_sub):
        r = x_ref[0, sub(j), :] + proj[j]
        if final_norm:
            ms = jnp.mean(r * r, axis=-1, keepdims=True)
            r = r * lax.rsqrt(ms + RMS_EPS) * gf_ref[...]
        y_ref[0, sub(j), :] = r


def _gla_out(gq, gk, gv, la, sgg, gn, oat, sga, sgb, x, wpa, wpb, wo, gf, final_norm):
    B, S, D = x.shape
    rt = TAIL_STEP_ROWS
    n_sub = rt // ROW_TILE
    const = lambda shape: pl.BlockSpec(shape, lambda b, t: (0,) * len(shape))
    rows = lambda w: pl.BlockSpec((1, rt, w), lambda b, t: (b, t, 0))
    return pl.pallas_call(
        functools.partial(_gla_out_kernel, final_norm=final_norm),
        grid=(B, S // rt),
        in_specs=[
            rows(GLA_DK), rows(GLA_DK), rows(GLA_DV), rows(GLA_DK), rows(GLA_DV),
            const((1, GLA_HEAD_V)),
            pl.BlockSpec((1, n_sub, MOBA_WIDTH, ROW_TILE), lambda b, t: (b, t, 0, 0)),
            rows(D), rows(D), rows(D),
            const(wpa.shape), const(wpb.shape), const(wo.shape), const((1, D)),
        ],
        out_specs=rows(D),
        out_shape=jax.ShapeDtypeStruct((B, S, D), F32),
        scratch_shapes=[pltpu.VMEM((GLA_HEAD_V, GLA_DK), F32)],
        compiler_params=pltpu.CompilerParams(
            dimension_semantics=("arbitrary", "arbitrary"),
            vmem_limit_bytes=VMEM_LIMIT),
        name="gla_out",
    )(gq, gk, gv, la, sgg, gn, oat, sga, sgb, x, wpa, wpb, wo, gf)


def _rope_tables(S):
    inv_freq = 1.0 / (ROPE_THETA ** (jnp.arange(HALF, dtype=F32) / HALF))
    ang = jnp.arange(S, dtype=F32)[:, None] * inv_freq[None, :]
    return jnp.cos(ang), jnp.sin(ang)


def _swap_half_head(a):
    lead = a.shape[:-1]
    a = a.reshape(*lead, N_PAIRS, 2, 2, HALF)
    return jnp.swapaxes(a, -2, -3).reshape(*lead, MOBA_WIDTH)


def kernel(x, norm_in_g, w_in, b_merge, w_gla_fg2, b_gla_fg, gla_norm_g,
           w_proj_a, w_proj_b, w_out, norm_f_g):
    B, S, D = x.shape
    depth = w_in.shape[0]
    nt = S // MOBA_BLOCK
    cos, sin = _rope_tables(S)
    q_scale = MOBA_HEAD_DIM ** -0.5 * math.log2(math.e)
    cq = (jnp.concatenate([cos, cos], axis=1) * q_scale).T
    sq = (jnp.concatenate([-sin, sin], axis=1) * q_scale).T
    ck = jnp.tile(cos, (1, KX // HALF))
    sk = jnp.tile(sin, (1, KX // HALF)) * jnp.where(jnp.arange(KX) < KX_HI, -1.0, 1.0)[None, :]
    head_eye = jnp.eye(MOBA_HEADS, dtype=F32)

    for layer in range(depth):
        w = w_in[layer]
        o = np.cumsum([0, MOBA_WIDTH, MOBA_WIDTH, MOBA_WIDTH, MOBA_WIDTH, GLA_DK, GLA_DK,
                       GLA_DV, GLA_DV, GLA_GATE_RANK, D_MODEL, D_MODEL])
        wq, wk, wv, wmg, wgq, wgk, wgv, wgg, wfg, wga, wgb = [
            w[:, o[i]:o[i + 1]] for i in range(11)]
        wn = jnp.concatenate([_swap_half_head(wk), wgq, wgk, wgv, wgg, wga, wgb],
                             axis=1).astype(BF16)
        wt = jnp.concatenate([wq, wv, wmg, wfg], axis=1).T.astype(BF16)
        wfg2 = w_gla_fg2[layer].astype(BF16)

        (qx, vt, smgt, k, gq, gk, gv, sgg, la, sga, sgb) = _in_proj(
            x, norm_in_g[layer][None, :], wn, wt, cq, sq, ck, sk, wfg2,
            b_gla_fg[layer][None, :], b_merge[layer])
        oat = _moba(qx, k, vt, smgt)
        x = _gla_out(gq, gk, gv, la, sgg, gla_norm_g[layer][None, :], oat, sga, sgb, x,
                     w_proj_a[layer].astype(BF16), w_proj_b[layer].astype(BF16),
                     w_out[layer].astype(BF16), norm_f_g[None, :],
                     final_norm=(layer == depth - 1))
    return x
```

```python
import functools
import math

import jax
import jax.numpy as jnp
import numpy as np
from jax import lax
from jax.experimental import pallas as pl
from jax.experimental.pallas import tpu as pltpu

F32 = jnp.float32
BF16 = jnp.bfloat16

D_MODEL = 1024
MOBA_HEADS = 8
MOBA_HEAD_DIM = 64
MOBA_WIDTH = MOBA_HEADS * MOBA_HEAD_DIM
MOBA_BLOCK = 256
MOBA_TOPK = 3
ROPE_THETA = 10000.0
GLA_HEADS = 4
GLA_DK = 256
GLA_DV = 512
GLA_HEAD_K = 64
GLA_HEAD_V = 128
GLA_GATE_RANK = 16
GLA_GATE_NORM = 16.0
GLA_CHUNK = 64
RMS_EPS = 1e-6
NEG_INF = -1e30

LANES = 128
BF16_ROWS = 16
ROW_TILE = MOBA_BLOCK
PAIR = 2 * MOBA_HEAD_DIM
N_PAIRS = MOBA_HEADS // 2
TAIL_STEP_ROWS = 512
IN_STEP_ROWS = 512
QK_AHEAD = 4
MOBA_Q_TILE = 512
LOOP_BLOCKS = 2
VMEM_LIMIT = 56 * 1024 * 1024

HALF = MOBA_HEAD_DIM // 2
KX = LANES
KX_LO, KX_OH, KX_HI = 0, HALF, 2 * HALF
N_BLOCKS_MAX = BF16_ROWS
VX = MOBA_HEAD_DIM + BF16_ROWS

_C_K = 0
_C_GQ = _C_K + MOBA_WIDTH
_C_GK = _C_GQ + GLA_DK
_C_GV = _C_GK + GLA_DK
_C_GG = _C_GV + GLA_DV
_C_GA = _C_GG + GLA_DV
_C_GB = _C_GA + D_MODEL
_C_END = _C_GB + D_MODEL
_R_Q = 0
_R_V = _R_Q + MOBA_WIDTH
_R_MG = _R_V + MOBA_WIDTH
_R_FG = _R_MG + MOBA_WIDTH
_R_END = _R_FG + GLA_GATE_RANK


def _dot(a, b):
    return jnp.dot(a, b, preferred_element_type=F32)


def _dot_nt(a, b):
    return lax.dot_general(a, b, (((1,), (1,)), ((), ())), preferred_element_type=F32)


def _dot_tn(a, b):
    return lax.dot_general(a, b, (((0,), (0,)), ((), ())), preferred_element_type=F32)


def _sigmoid(x):
    return 1.0 / (1.0 + jnp.exp(-x))


def _silu(x):
    return x * _sigmoid(x)


def _select_blocks(gates, first_own, n_blocks, nb):
    H = MOBA_HEADS
    qw = gates.shape[1]
    col_own = (first_own + lax.broadcasted_iota(jnp.int32, (1, qw), 1) // nb).astype(F32)
    past = [col_own > n for n in range(n_blocks)]
    g = [jnp.where(past[n], gates[n * H:(n + 1) * H], NEG_INF) for n in range(n_blocks)]
    picked = [jnp.zeros((H, qw), jnp.bool_) for _ in range(n_blocks)]
    for _ in range(MOBA_TOPK):
        best = functools.reduce(jnp.maximum, g)
        first = functools.reduce(jnp.minimum, [
            jnp.where(g[n] == best, float(n), float(n_blocks)) for n in range(n_blocks)])
        for n in range(n_blocks):
            hit = first == float(n)
            picked[n] = picked[n] | hit
            g[n] = jnp.where(hit, -jnp.inf, g[n])
    keep = jnp.concatenate(
        [jnp.where((col_own == n) | (past[n] & picked[n]), 1.0, 0.0) for n in range(n_blocks)],
        axis=0).astype(BF16)
    r_out = lax.broadcasted_iota(jnp.int32, (H * n_blocks, n_blocks * H), 0)
    r_in = lax.broadcasted_iota(jnp.int32, (H * n_blocks, n_blocks * H), 1)
    perm = ((r_out // n_blocks == r_in % H) & (r_out % n_blocks == r_in // H)).astype(BF16)
    return jnp.where(_dot(perm, keep) > 0.5, 0.0, NEG_INF).astype(BF16)


def _in_proj_kernel(x_ref, g_ref, wn_ref, wt_ref, cq_ref, sq_ref, ck_ref, sk_ref,
                    wfg2_ref, bfg_ref, bm_ref,
                    qx_ref, vt_ref, smgt_ref, k_ref, gq_ref, gk_ref, gv_ref,
                    sgg_ref, la_ref, sga_ref, sgb_ref, km_ref, *, n_blocks):
    t = pl.program_id(1)
    rt = x_ref.shape[1]
    nb = MOBA_BLOCK
    n_sub = rt // nb
    hd = MOBA_HEAD_DIM
    H = MOBA_HEADS
    assert n_blocks <= N_BLOCKS_MAX
    blk_cols = lambda j: slice(j * nb, (j + 1) * nb)

    @pl.when(t == 0)
    def _():
        km_ref[...] = jnp.zeros_like(km_ref)

    x = x_ref[0]
    ms = jnp.mean(x * x, axis=-1, keepdims=True)
    hb = (x * lax.rsqrt(ms + RMS_EPS) * g_ref[...]).astype(BF16)

    pt = _dot_nt(wt_ref[...], hb)
    cq = cq_ref[...]
    sq = sq_ref[...]
    ones_rows = jnp.where(lax.broadcasted_iota(jnp.int32, (BF16_ROWS, nb), 0) == 0,
                          1.0, 0.0).astype(BF16)
    q_rot = []
    for h in range(H):
        blk = pt[_R_Q + h * hd:_R_Q + (h + 1) * hd]
        swapped = jnp.concatenate([blk[HALF:], blk[:HALF]], axis=0)
        q_rot.append((blk * cq + swapped * sq).astype(BF16))
        v_rows = pt[_R_V + h * hd:_R_V + (h + 1) * hd].astype(BF16)
        for j in range(n_sub):
            vt_ref[0, j, h * VX:h * VX + hd, :] = v_rows[:, blk_cols(j)]
            vt_ref[0, j, h * VX + hd:(h + 1) * VX, :] = ones_rows

    smg = _silu(pt[_R_MG:_R_FG]).astype(BF16)
    for j in range(n_sub):
        smgt_ref[0, j] = smg[:, blk_cols(j)]

    pk = _dot(hb, wn_ref[:, _C_K:_C_GQ])
    pg = _dot(hb, wn_ref[:, _C_GQ:_C_GA])
    ck = ck_ref[...]
    sk = sk_ref[...]
    lane = lax.broadcasted_iota(jnp.int32, (rt, KX), 1)
    row_block = t * n_sub + lax.broadcasted_iota(jnp.int32, (rt, KX), 0) // nb
    block_onehot = jnp.where(lane == KX_OH + row_block, 1.0, 0.0)
    own_lanes = (lane % (2 * HALF)) < HALF
    k_means = [[] for _ in range(n_sub)]
    for p in range(N_PAIRS):
        blk = pk[:, p * PAIR:(p + 1) * PAIR]
        kr = blk * ck + pltpu.roll(blk, 2 * HALF, axis=1) * sk
        for j in range(n_sub):
            k_means[j].append(jnp.mean(kr[blk_cols(j)], axis=0, keepdims=True))
        for hh, src in ((0, kr), (1, pltpu.roll(kr, KX - HALF, axis=1))):
            h = 2 * p + hh
            k_ref[0, :, h * KX:(h + 1) * KX] = (
                jnp.where(own_lanes, src, 0.0) + block_onehot).astype(BF16)
    km_lane = lax.broadcasted_iota(jnp.int32, (H, MOBA_WIDTH), 1)
    km_head = 2 * (km_lane // PAIR) + (km_lane // HALF) % 2
    head_lanes = km_head == lax.broadcasted_iota(jnp.int32, (H, MOBA_WIDTH), 0)
    for j in range(n_sub):
        mean_row = jnp.concatenate(k_means[j], axis=1)
        km_ref[pl.ds(pl.multiple_of((t * n_sub + j) * H, H), H), :] = jnp.where(
            head_lanes, mean_row, 0.0)

    q_pair = jnp.concatenate(
        [q_rot[2 * p + hh][part * HALF:(part + 1) * HALF]
         for p in range(N_PAIRS) for part in range(2) for hh in range(2)], axis=0)
    km = km_ref[...]
    km_hi = km.astype(BF16)
    km_lo = (km - km_hi.astype(F32)).astype(BF16)
    gates = _dot(km_hi, q_pair) + _dot(km_lo, q_pair)

    pm = _dot(hb, wn_ref[:, _C_GA:_C_END])

    mask_rows = _select_blocks(gates, t * n_sub, n_blocks, nb)
    pad = jnp.zeros((KX_HI - KX_OH - n_blocks, rt), BF16)
    tail = jnp.zeros((KX - KX_HI - HALF, rt), BF16)
    for h in range(H):
        qx_ref[0, h] = jnp.concatenate(
            [q_rot[h][:HALF], mask_rows[h * n_blocks:(h + 1) * n_blocks], pad,
             q_rot[h][HALF:], tail], axis=0)

    gq_ref[0] = (pg[:, _C_GQ - _C_GQ:_C_GK - _C_GQ] * (GLA_HEAD_K ** -0.5)).astype(BF16)
    gk_ref[0] = pg[:, _C_GK - _C_GQ:_C_GV - _C_GQ].astype(BF16)
    gv_ref[0] = pg[:, _C_GV - _C_GQ:_C_GG - _C_GQ].astype(BF16)
    sgg_ref[0] = _silu(pg[:, _C_GG - _C_GQ:_C_GA - _C_GQ]).astype(BF16)

    z = _dot_tn(pt[_R_FG:_R_END].astype(BF16), wfg2_ref[...]) + bfg_ref[...]
    log_sig = jnp.minimum(z, 0.0) - jnp.log1p(jnp.exp(-jnp.abs(z)))
    la_ref[0] = log_sig * (1.0 / GLA_GATE_NORM)

    sga_ref[0] = _sigmoid(pm[:, :D_MODEL] + bm_ref[0:1, :]).astype(BF16)
    sgb_ref[0] = _sigmoid(pm[:, D_MODEL:] + bm_ref[1:2, :]).astype(BF16)


def _in_proj(x, g, wn, wt, cq, sq, ck, sk, wfg2, bfg, bm):
    B, S, D = x.shape
    nt = S // ROW_TILE
    rt = IN_STEP_ROWS
    n_sub = rt // ROW_TILE
    const = lambda shape: pl.BlockSpec(shape, lambda b, t: (0,) * len(shape))
    t_blocked = lambda r: pl.BlockSpec((1, n_sub, r, ROW_TILE), lambda b, t: (b, t, 0, 0))
    rows = lambda w: pl.BlockSpec((1, rt, w), lambda b, t: (b, t, 0))
    t_shape = lambda r: jax.ShapeDtypeStruct((B, nt, r, ROW_TILE), BF16)
    rshape = lambda w, dt=BF16: jax.ShapeDtypeStruct((B, S, w), dt)
    return pl.pallas_call(
        functools.partial(_in_proj_kernel, n_blocks=nt),
        grid=(B, S // rt),
        in_specs=[
            rows(D), const((1, D)), const(wn.shape), const(wt.shape),
            pl.BlockSpec((MOBA_HEAD_DIM, rt), lambda b, t: (0, t)),
            pl.BlockSpec((MOBA_HEAD_DIM, rt), lambda b, t: (0, t)),
            pl.BlockSpec((rt, KX), lambda b, t: (t, 0)),
            pl.BlockSpec((rt, KX), lambda b, t: (t, 0)),
            const(wfg2.shape), const((1, GLA_DK)), const((2, D)),
        ],
        out_specs=[
            pl.BlockSpec((1, MOBA_HEADS, KX, rt), lambda b, t: (b, 0, 0, t)),
            t_blocked(MOBA_HEADS * VX), t_blocked(MOBA_WIDTH),
            rows(MOBA_HEADS * KX),
            rows(GLA_DK), rows(GLA_DK), rows(GLA_DV), rows(GLA_DV), rows(GLA_DK),
            rows(D), rows(D),
        ],
        out_shape=[
            jax.ShapeDtypeStruct((B, MOBA_HEADS, KX, S), BF16),
            t_shape(MOBA_HEADS * VX), t_shape(MOBA_WIDTH),
            rshape(MOBA_HEADS * KX),
            rshape(GLA_DK), rshape(GLA_DK), rshape(GLA_DV), rshape(GLA_DV),
            rshape(GLA_DK, F32), rshape(D), rshape(D),
        ],
        scratch_shapes=[pltpu.VMEM((nt * MOBA_HEADS, MOBA_WIDTH), F32)],
        compiler_params=pltpu.CompilerParams(
            dimension_semantics=("arbitrary", "arbitrary"),
            vmem_limit_bytes=VMEM_LIMIT),
        name="in_proj",
    )(x, g, wn, wt, cq, sq, ck, sk, wfg2, bfg, bm)


def _moba_kernel(qx_ref, k_ref, vt_ref, smgt_ref, o_ref, s_ref, smax_ref, m_ref, acc_ref):
    nb = MOBA_BLOCK
    hd = MOBA_HEAD_DIM
    n_sub = smgt_ref.shape[1]
    first_own = pl.program_id(1) * n_sub

    def produce_scores(n, h, col0=0):
        kb = k_ref[0, pl.ds(pl.multiple_of(n * nb, nb), nb), h * KX:(h + 1) * KX]
        s = _dot(kb, qx_ref[0, h, :, col0:])
        s_ref[h, :, col0:] = s
        smax_ref[h, :, col0:] = jnp.max(s, axis=0, keepdims=True)

    causal = (lax.broadcasted_iota(jnp.int32, (nb, nb), 0)
              <= lax.broadcasted_iota(jnp.int32, (nb, nb), 1))

    def attend(blocks, diagonal, next_block):
        col0 = [0 if d is None else d * nb for d in diagonal]
        jobs = [(blocks[0], h, col0[0]) for h in range(QK_AHEAD, MOBA_HEADS)]
        jobs += [(n, h, c0) for n, c0 in zip(blocks[1:], col0[1:]) for h in range(MOBA_HEADS)]
        if next_block is not None:
            jobs += [(next_block, h, 0) for h in range(QK_AHEAD)]

        def issue():
            if jobs:
                produce_scores(*jobs.pop(0))

        issue()
        issue()
        for n, diag, c0 in zip(blocks, diagonal, col0):
            for h in range(MOBA_HEADS):
                s = s_ref[h, :, c0:]
                smax = smax_ref[h, :, c0:]
                if diag is not None:
                    tri = jnp.where(causal, s[:, :nb], NEG_INF)
                    tri_max = jnp.max(tri, axis=0, keepdims=True)
                    if s.shape[1] > nb:
                        s = jnp.concatenate([tri, s[:, nb:]], axis=1)
                        smax = jnp.concatenate([tri_max, smax[:, nb:]], axis=1)
                    else:
                        s, smax = tri, tri_max
                vt = vt_ref[0, n, h * VX:(h + 1) * VX, :]
                m_old = m_ref[h, :, c0:]
                m_new = jnp.maximum(m_old, smax)
                m_ref[h, :, c0:] = m_new
                acc_ref[h, :, c0:] = (jnp.exp2(m_old - m_new) * acc_ref[h, :, c0:]
                                      + _dot(vt, jnp.exp2(s - m_new).astype(BF16)))
                issue()

    m_ref[...] = jnp.full(m_ref.shape, NEG_INF, F32)
    acc_ref[...] = jnp.zeros_like(acc_ref)
    for h in range(QK_AHEAD):
        produce_scores(0, h)

    assert n_sub % LOOP_BLOCKS == 0

    def body(i, c):
        n = i * LOOP_BLOCKS
        attend([n + d for d in range(LOOP_BLOCKS)], [None] * LOOP_BLOCKS, n + LOOP_BLOCKS)
        return c

    lax.fori_loop(0, first_own // LOOP_BLOCKS, body, 0)
    attend([first_own + j for j in range(n_sub)], list(range(n_sub)), None)
    for h in range(MOBA_HEADS):
        o = acc_ref[h, 0:hd, :] / acc_ref[h, hd:hd + 1, :]
        for j in range(n_sub):
            gate = smgt_ref[0, j, h * hd:(h + 1) * hd, :].astype(F32)
            o_ref[0, j, h * hd:(h + 1) * hd, :] = (o[:, j * nb:(j + 1) * nb] * gate).astype(BF16)


def _moba(qx, k, vt, smgt):
    B, nt, W, rt = smgt.shape
    S = k.shape[1]
    n_sub = MOBA_Q_TILE // rt
    qw = MOBA_Q_TILE
    tile = pl.BlockSpec((1, n_sub, W, rt), lambda b, t: (b, t, 0, 0))
    return pl.pallas_call(
        _moba_kernel,
        grid=(B, nt // n_sub),
        in_specs=[
            pl.BlockSpec((1, MOBA_HEADS, KX, qw), lambda b, t: (b, 0, 0, t)),
            pl.BlockSpec((1, S, MOBA_HEADS * KX), lambda b, t: (b, 0, 0)),
            pl.BlockSpec((1, nt, MOBA_HEADS * VX, rt), lambda b, t: (b, 0, 0, 0)),
            tile,
        ],
        out_specs=tile,
        out_shape=jax.ShapeDtypeStruct((B, nt, W, rt), BF16),
        scratch_shapes=[
            pltpu.VMEM((MOBA_HEADS, MOBA_BLOCK, qw), F32),
            pltpu.VMEM((MOBA_HEADS, 1, qw), F32),
            pltpu.VMEM((MOBA_HEADS, 1, qw), F32),
            pltpu.VMEM((MOBA_HEADS, VX, qw), F32),
        ],
        compiler_params=pltpu.CompilerParams(
            dimension_semantics=("arbitrary", "arbitrary"),
            vmem_limit_bytes=VMEM_LIMIT),
        name="moba",
    )(qx, k, vt, smgt)


def _gla_out_kernel(q_ref, k_ref, v_ref, la_ref, sgg_ref, gn_ref,
                    oat_ref, sga_ref, sgb_ref, x_ref, wpa_ref, wpb_ref, wo_ref, gf_ref,
                    y_ref, st_ref, *, final_norm):
    C = GLA_CHUNK
    hk, hv = GLA_HEAD_K, GLA_HEAD_V
    n_sub, rt = oat_ref.shape[1], oat_ref.shape[3]
    sub = lambda j: slice(j * rt, (j + 1) * rt)

    @pl.when(pl.program_id(1) == 0)
    def _():
        st_ref[...] = jnp.zeros_like(st_ref)

    nc = TAIL_STEP_ROWS // C
    H = GLA_HEADS
    tril_b = (lax.broadcasted_iota(jnp.int32, (C, C), 0)
              >= lax.broadcasted_iota(jnp.int32, (C, C), 1)).astype(BF16)
    tril_stack = (lax.broadcasted_iota(jnp.int32, (H * C, C), 0) % C
                  >= lax.broadcasted_iota(jnp.int32, (H * C, C), 1))
    lane_head = lax.broadcasted_iota(jnp.int32, (C, GLA_DK), 1) // hk
    zero_b = jnp.zeros((C, GLA_DK), BF16)
    gn = gn_ref[...]

    def stack_heads(x):
        return jnp.concatenate([jnp.where(lane_head == h, x, zero_b) for h in range(H)], axis=0)

    chunk = lambda c: slice(c * C, (c + 1) * C)
    g = la_ref[0]
    g_hi = g.astype(BF16)
    g_lo = (g - g_hi.astype(F32)).astype(BF16)
    b = [_dot(tril_b, g_hi[chunk(c)]) + _dot(tril_b, g_lo[chunk(c)]) for c in range(nc)]

    ya = [_dot_tn(oat_ref[0, j], wpa_ref[...]) for j in range(n_sub)]

    qe_s, qb_s, ke_b, kl_s, decay = [], [], [], [], []
    for c in range(nc):
        b_mid = b[c][C // 2 - 1:C // 2]
        b_last = b[c][C - 1:C]
        qe = q_ref[0, chunk(c), :].astype(F32) * jnp.exp(b[c] - b_mid)
        ke = k_ref[0, chunk(c), :].astype(F32) * jnp.exp(b_mid - b[c])
        qe_s.append(stack_heads(qe.astype(BF16)))
        qb_s.append(stack_heads((qe * jnp.exp(b_mid)).astype(BF16)))
        kl_s.append(stack_heads((ke * jnp.exp(b_last - b_mid)).astype(BF16)))
        ke_b.append(ke.astype(BF16))
        decay.append(jnp.exp(b_last))

    attn = [jnp.where(tril_stack, _dot_nt(qe_s[c], ke_b[c]), 0.0).astype(BF16) for c in range(nc)]
    intra = [jnp.concatenate(
        [_dot(attn[c][h * C:(h + 1) * C], v_ref[0, chunk(c), h * hv:(h + 1) * hv])
         for h in range(H)], axis=1) for c in range(nc)]

    v_s = [jnp.concatenate([v_ref[0, chunk(c), h * hv:(h + 1) * hv] for h in range(H)], axis=0)
           for c in range(nc)]
    upd = [_dot_tn(v_s[c], kl_s[c]) for c in range(nc)]

    st = st_ref[...]
    inter = []
    for c in range(nc):
        o_s = _dot_nt(qb_s[c], st.astype(BF16))
        inter.append(jnp.concatenate([o_s[h * C:(h + 1) * C] for h in range(H)], axis=1))
        st = st * decay[c] + upd[c]
    st_ref[...] = st

    ob = []
    for c in range(nc):
        o = inter[c] + intra[c]
        heads = []
        for h in range(H):
            oh = o[:, h * hv:(h + 1) * hv]
            ms = jnp.mean(oh * oh, axis=-1, keepdims=True)
            y = oh * lax.rsqrt(ms + RMS_EPS) * gn
            gate = sgg_ref[0, chunk(c), h * hv:(h + 1) * hv].astype(F32)
            heads.append((y * gate).astype(BF16))
        ob.append(jnp.concatenate(heads, axis=1))

    per_sub = rt // C
    yb = [_dot(jnp.concatenate(ob[j * per_sub:(j + 1) * per_sub], axis=0), wpb_ref[...])
          for j in range(n_sub)]
    merged = [(sga_ref[0, sub(j), :].astype(F32) * ya[j]
               + sgb_ref[0, sub(j), :].astype(F32) * yb[j]).astype(BF16) for j in range(n_sub)]
    proj = [_dot(merged[j], wo_ref[...]) for j in range(n_sub)]
    for j in range(n_sub):
        r = x_ref[0, sub(j), :] + proj[j]
        if final_norm:
            ms = jnp.mean(r * r, axis=-1, keepdims=True)
            r = r * lax.rsqrt(ms + RMS_EPS) * gf_ref[...]
        y_ref[0, sub(j), :] = r


def _gla_out(gq, gk, gv, la, sgg, gn, oat, sga, sgb, x, wpa, wpb, wo, gf, final_norm):
    B, S, D = x.shape
    rt = TAIL_STEP_ROWS
    n_sub = rt // ROW_TILE
    const = lambda shape: pl.BlockSpec(shape, lambda b, t: (0,) * len(shape))
    rows = lambda w: pl.BlockSpec((1, rt, w), lambda b, t: (b, t, 0))
    return pl.pallas_call(
        functools.partial(_gla_out_kernel, final_norm=final_norm),
        grid=(B, S // rt),
        in_specs=[
            rows(GLA_DK), rows(GLA_DK), rows(GLA_DV), rows(GLA_DK), rows(GLA_DV),
            const((1, GLA_HEAD_V)),
            pl.BlockSpec((1, n_sub, MOBA_WIDTH, ROW_TILE), lambda b, t: (b, t, 0, 0)),
            rows(D), rows(D), rows(D),
            const(wpa.shape), const(wpb.shape), const(wo.shape), const((1, D)),
        ],
        out_specs=rows(D),
        out_shape=jax.ShapeDtypeStruct((B, S, D), F32),
        scratch_shapes=[pltpu.VMEM((GLA_HEAD_V, GLA_DK), F32)],
        compiler_params=pltpu.CompilerParams(
            dimension_semantics=("arbitrary", "arbitrary"),
            vmem_limit_bytes=VMEM_LIMIT),
        name="gla_out",
    )(gq, gk, gv, la, sgg, gn, oat, sga, sgb, x, wpa, wpb, wo, gf)


def _rope_tables(S):
    inv_freq = 1.0 / (ROPE_THETA ** (jnp.arange(HALF, dtype=F32) / HALF))
    ang = jnp.arange(S, dtype=F32)[:, None] * inv_freq[None, :]
    return jnp.cos(ang), jnp.sin(ang)


def _swap_half_head(a):
    lead = a.shape[:-1]
    a = a.reshape(*lead, N_PAIRS, 2, 2, HALF)
    return jnp.swapaxes(a, -2, -3).reshape(*lead, MOBA_WIDTH)


def kernel(x, norm_in_g, w_in, b_merge, w_gla_fg2, b_gla_fg, gla_norm_g,
           w_proj_a, w_proj_b, w_out, norm_f_g):
    B, S, D = x.shape
    depth = w_in.shape[0]
    nt = S // MOBA_BLOCK
    cos, sin = _rope_tables(S)
    q_scale = MOBA_HEAD_DIM ** -0.5 * math.log2(math.e)
    cq = (jnp.concatenate([cos, cos], axis=1) * q_scale).T
    sq = (jnp.concatenate([-sin, sin], axis=1) * q_scale).T
    ck = jnp.tile(cos, (1, KX // HALF))
    sk = jnp.tile(sin, (1, KX // HALF)) * jnp.where(jnp.arange(KX) < KX_HI, -1.0, 1.0)[None, :]

    for layer in range(depth):
        w = w_in[layer]
        o = np.cumsum([0, MOBA_WIDTH, MOBA_WIDTH, MOBA_WIDTH, MOBA_WIDTH, GLA_DK, GLA_DK,
                       GLA_DV, GLA_DV, GLA_GATE_RANK, D_MODEL, D_MODEL])
        wq, wk, wv, wmg, wgq, wgk, wgv, wgg, wfg, wga, wgb = [
            w[:, o[i]:o[i + 1]] for i in range(11)]
        wn = jnp.concatenate([_swap_half_head(wk), wgq, wgk, wgv, wgg, wga, wgb],
                             axis=1).astype(BF16)
        wt = jnp.concatenate([wq, wv, wmg, wfg], axis=1).T.astype(BF16)
        wfg2 = w_gla_fg2[layer].astype(BF16)

        (qx, vt, smgt, k, gq, gk, gv, sgg, la, sga, sgb) = _in_proj(
            x, norm_in_g[layer][None, :], wn, wt, cq, sq, ck, sk, wfg2,
            b_gla_fg[layer][None, :], b_merge[layer])
        oat = _moba(qx, k, vt, smgt)
        x = _gla_out(gq, gk, gv, la, sgg, gla_norm_g[layer][None, :], oat, sga, sgb, x,
                     w_proj_a[layer].astype(BF16), w_proj_b[layer].astype(BF16),
                     w_out[layer].astype(BF16), norm_f_g[None, :],
                     final_norm=(layer == depth - 1))
    return x
```

```python
import functools
import math

import jax
import jax.numpy as jnp
import numpy as np
from jax import lax
from jax.experimental import pallas as pl
from jax.experimental.pallas import tpu as pltpu

F32 = jnp.float32
BF16 = jnp.bfloat16

D_MODEL = 1024
MOBA_HEADS = 8
MOBA_HEAD_DIM = 64
MOBA_WIDTH = MOBA_HEADS * MOBA_HEAD_DIM
MOBA_BLOCK = 256
MOBA_TOPK = 3
ROPE_THETA = 10000.0
GLA_HEADS = 4
GLA_DK = 256
GLA_DV = 512
GLA_HEAD_K = 64
GLA_HEAD_V = 128
GLA_GATE_RANK = 16
GLA_GATE_NORM = 16.0
GLA_CHUNK = 64
RMS_EPS = 1e-6
NEG_INF = -1e30

LANES = 128
BF16_ROWS = 16
ROW_TILE = MOBA_BLOCK
PAIR = 2 * MOBA_HEAD_DIM
N_PAIRS = MOBA_HEADS // 2
TAIL_STEP_ROWS = 512
IN_STEP_ROWS = 512
QK_AHEAD = 4
QK_LEAD = 1
MOBA_Q_TILE = 512
LOOP_BLOCKS = 2
VMEM_LIMIT = 56 * 1024 * 1024

HALF = MOBA_HEAD_DIM // 2
KX = LANES
KX_LO, KX_OH, KX_HI = 0, HALF, 2 * HALF
N_BLOCKS_MAX = BF16_ROWS
VX = MOBA_HEAD_DIM + BF16_ROWS

_C_K = 0
_C_GQ = _C_K + MOBA_WIDTH
_C_GK = _C_GQ + GLA_DK
_C_GV = _C_GK + GLA_DK
_C_GG = _C_GV + GLA_DV
_C_GA = _C_GG + GLA_DV
_C_GB = _C_GA + D_MODEL
_C_END = _C_GB + D_MODEL
_R_Q = 0
_R_V = _R_Q + MOBA_WIDTH
_R_MG = _R_V + MOBA_WIDTH
_R_FG = _R_MG + MOBA_WIDTH
_R_END = _R_FG + GLA_GATE_RANK


def _dot(a, b):
    return jnp.dot(a, b, preferred_element_type=F32)


def _dot_nt(a, b):
    return lax.dot_general(a, b, (((1,), (1,)), ((), ())), preferred_element_type=F32)


def _dot_tn(a, b):
    return lax.dot_general(a, b, (((0,), (0,)), ((), ())), preferred_element_type=F32)


def _sigmoid(x):
    return 1.0 / (1.0 + jnp.exp(-x))


def _silu(x):
    return x * _sigmoid(x)


def _select_blocks(gates, first_own, n_blocks, nb):
    H = MOBA_HEADS
    qw = gates.shape[1]
    col_own = (first_own + lax.broadcasted_iota(jnp.int32, (1, qw), 1) // nb).astype(F32)
    past = [col_own > n for n in range(n_blocks)]
    g = [jnp.where(past[n], gates[n * H:(n + 1) * H], NEG_INF) for n in range(n_blocks)]
    picked = [jnp.zeros((H, qw), jnp.bool_) for _ in range(n_blocks)]
    for _ in range(MOBA_TOPK):
        best = functools.reduce(jnp.maximum, g)
        first = functools.reduce(jnp.minimum, [
            jnp.where(g[n] == best, float(n), float(n_blocks)) for n in range(n_blocks)])
        for n in range(n_blocks):
            hit = first == float(n)
            picked[n] = picked[n] | hit
            g[n] = jnp.where(hit, -jnp.inf, g[n])
    keep = jnp.concatenate(
        [jnp.where((col_own == n) | (past[n] & picked[n]), 1.0, 0.0) for n in range(n_blocks)],
        axis=0).astype(BF16)
    r_out = lax.broadcasted_iota(jnp.int32, (H * n_blocks, n_blocks * H), 0)
    r_in = lax.broadcasted_iota(jnp.int32, (H * n_blocks, n_blocks * H), 1)
    perm = ((r_out // n_blocks == r_in % H) & (r_out % n_blocks == r_in // H)).astype(BF16)
    return jnp.where(_dot(perm, keep) > 0.5, 0.0, NEG_INF).astype(BF16)


def _in_proj_kernel(x_ref, g_ref, wn_ref, wt_ref, cq_ref, sq_ref, ck_ref, sk_ref,
                    wfg2_ref, bfg_ref, bm_ref,
                    qx_ref, vt_ref, smgt_ref, k_ref, gq_ref, gk_ref, gv_ref,
                    sgg_ref, la_ref, sga_ref, sgb_ref, km_ref, *, n_blocks):
    t = pl.program_id(1)
    rt = x_ref.shape[1]
    nb = MOBA_BLOCK
    n_sub = rt // nb
    hd = MOBA_HEAD_DIM
    H = MOBA_HEADS
    assert n_blocks <= N_BLOCKS_MAX
    blk_cols = lambda j: slice(j * nb, (j + 1) * nb)

    @pl.when(t == 0)
    def _():
        km_ref[...] = jnp.zeros_like(km_ref)

    x = x_ref[0]
    ms = jnp.mean(x * x, axis=-1, keepdims=True)
    hb = (x * lax.rsqrt(ms + RMS_EPS) * g_ref[...]).astype(BF16)

    pt = _dot_nt(wt_ref[...], hb)
    cq = cq_ref[...]
    sq = sq_ref[...]
    ones_rows = jnp.where(lax.broadcasted_iota(jnp.int32, (BF16_ROWS, nb), 0) == 0,
                          1.0, 0.0).astype(BF16)
    q_rot = []
    for h in range(H):
        blk = pt[_R_Q + h * hd:_R_Q + (h + 1) * hd]
        swapped = jnp.concatenate([blk[HALF:], blk[:HALF]], axis=0)
        q_rot.append((blk * cq + swapped * sq).astype(BF16))
        v_rows = pt[_R_V + h * hd:_R_V + (h + 1) * hd].astype(BF16)
        for j in range(n_sub):
            vt_ref[0, j, h * VX:h * VX + hd, :] = v_rows[:, blk_cols(j)]
            vt_ref[0, j, h * VX + hd:(h + 1) * VX, :] = ones_rows

    smg = _silu(pt[_R_MG:_R_FG]).astype(BF16)
    for j in range(n_sub):
        smgt_ref[0, j] = smg[:, blk_cols(j)]

    pk = _dot(hb, wn_ref[:, _C_K:_C_GQ])
    pg = _dot(hb, wn_ref[:, _C_GQ:_C_GA])
    ck = ck_ref[...]
    sk = sk_ref[...]
    lane = lax.broadcasted_iota(jnp.int32, (rt, KX), 1)
    row_block = t * n_sub + lax.broadcasted_iota(jnp.int32, (rt, KX), 0) // nb
    block_onehot = jnp.where(lane == KX_OH + row_block, 1.0, 0.0)
    own_lanes = (lane % (2 * HALF)) < HALF
    k_means = [[] for _ in range(n_sub)]
    for p in range(N_PAIRS):
        blk = pk[:, p * PAIR:(p + 1) * PAIR]
        kr = blk * ck + pltpu.roll(blk, 2 * HALF, axis=1) * sk
        for j in range(n_sub):
            k_means[j].append(jnp.mean(kr[blk_cols(j)], axis=0, keepdims=True))
        for hh, src in ((0, kr), (1, pltpu.roll(kr, KX - HALF, axis=1))):
            h = 2 * p + hh
            k_ref[0, :, h * KX:(h + 1) * KX] = (
                jnp.where(own_lanes, src, 0.0) + block_onehot).astype(BF16)
    km_lane = lax.broadcasted_iota(jnp.int32, (H, MOBA_WIDTH), 1)
    km_head = 2 * (km_lane // PAIR) + (km_lane // HALF) % 2
    head_lanes = km_head == lax.broadcasted_iota(jnp.int32, (H, MOBA_WIDTH), 0)
    for j in range(n_sub):
        mean_row = jnp.concatenate(k_means[j], axis=1)
        km_ref[pl.ds(pl.multiple_of((t * n_sub + j) * H, H), H), :] = jnp.where(
            head_lanes, mean_row, 0.0)

    q_pair = jnp.concatenate(
        [q_rot[2 * p + hh][part * HALF:(part + 1) * HALF]
         for p in range(N_PAIRS) for part in range(2) for hh in range(2)], axis=0)
    km = km_ref[...]
    km_hi = km.astype(BF16)
    km_lo = (km - km_hi.astype(F32)).astype(BF16)
    gates = _dot(km_hi, q_pair) + _dot(km_lo, q_pair)

    pm = _dot(hb, wn_ref[:, _C_GA:_C_END])

    mask_rows = _select_blocks(gates, t * n_sub, n_blocks, nb)
    pad = jnp.zeros((KX_HI - KX_OH - n_blocks, rt), BF16)
    tail = jnp.zeros((KX - KX_HI - HALF, rt), BF16)
    for h in range(H):
        qx_ref[0, h] = jnp.concatenate(
            [q_rot[h][:HALF], mask_rows[h * n_blocks:(h + 1) * n_blocks], pad,
             q_rot[h][HALF:], tail], axis=0)

    gq_ref[0] = (pg[:, _C_GQ - _C_GQ:_C_GK - _C_GQ] * (GLA_HEAD_K ** -0.5)).astype(BF16)
    gk_ref[0] = pg[:, _C_GK - _C_GQ:_C_GV - _C_GQ].astype(BF16)
    gv_ref[0] = pg[:, _C_GV - _C_GQ:_C_GG - _C_GQ].astype(BF16)
    sgg_ref[0] = _silu(pg[:, _C_GG - _C_GQ:_C_GA - _C_GQ]).astype(BF16)

    z = _dot_tn(pt[_R_FG:_R_END].astype(BF16), wfg2_ref[...]) + bfg_ref[...]
    log_sig = jnp.minimum(z, 0.0) - jnp.log1p(jnp.exp(-jnp.abs(z)))
    la_ref[0] = log_sig * (1.0 / GLA_GATE_NORM)

    sga_ref[0] = _sigmoid(pm[:, :D_MODEL] + bm_ref[0:1, :]).astype(BF16)
    sgb_ref[0] = _sigmoid(pm[:, D_MODEL:] + bm_ref[1:2, :]).astype(BF16)


def _in_proj(x, g, wn, wt, cq, sq, ck, sk, wfg2, bfg, bm):
    B, S, D = x.shape
    nt = S // ROW_TILE
    rt = IN_STEP_ROWS
    n_sub = rt // ROW_TILE
    const = lambda shape: pl.BlockSpec(shape, lambda b, t: (0,) * len(shape))
    t_blocked = lambda r: pl.BlockSpec((1, n_sub, r, ROW_TILE), lambda b, t: (b, t, 0, 0))
    rows = lambda w: pl.BlockSpec((1, rt, w), lambda b, t: (b, t, 0))
    t_shape = lambda r: jax.ShapeDtypeStruct((B, nt, r, ROW_TILE), BF16)
    rshape = lambda w, dt=BF16: jax.ShapeDtypeStruct((B, S, w), dt)
    return pl.pallas_call(
        functools.partial(_in_proj_kernel, n_blocks=nt),
        grid=(B, S // rt),
        in_specs=[
            rows(D), const((1, D)), const(wn.shape), const(wt.shape),
            pl.BlockSpec((MOBA_HEAD_DIM, rt), lambda b, t: (0, t)),
            pl.BlockSpec((MOBA_HEAD_DIM, rt), lambda b, t: (0, t)),
            pl.BlockSpec((rt, KX), lambda b, t: (t, 0)),
            pl.BlockSpec((rt, KX), lambda b, t: (t, 0)),
            const(wfg2.shape), const((1, GLA_DK)), const((2, D)),
        ],
        out_specs=[
            pl.BlockSpec((1, MOBA_HEADS, KX, rt), lambda b, t: (b, 0, 0, t)),
            t_blocked(MOBA_HEADS * VX), t_blocked(MOBA_WIDTH),
            rows(MOBA_HEADS * KX),
            rows(GLA_DK), rows(GLA_DK), rows(GLA_DV), rows(GLA_DV), rows(GLA_DK),
            rows(D), rows(D),
        ],
        out_shape=[
            jax.ShapeDtypeStruct((B, MOBA_HEADS, KX, S), BF16),
            t_shape(MOBA_HEADS * VX), t_shape(MOBA_WIDTH),
            rshape(MOBA_HEADS * KX),
            rshape(GLA_DK), rshape(GLA_DK), rshape(GLA_DV), rshape(GLA_DV),
            rshape(GLA_DK, F32), rshape(D), rshape(D),
        ],
        scratch_shapes=[pltpu.VMEM((nt * MOBA_HEADS, MOBA_WIDTH), F32)],
        compiler_params=pltpu.CompilerParams(
            dimension_semantics=("arbitrary", "arbitrary"),
            vmem_limit_bytes=VMEM_LIMIT),
        name="in_proj",
    )(x, g, wn, wt, cq, sq, ck, sk, wfg2, bfg, bm)


def _moba_kernel(qx_ref, k_ref, vt_ref, smgt_ref, o_ref, s_ref, smax_ref, m_ref, acc_ref):
    nb = MOBA_BLOCK
    hd = MOBA_HEAD_DIM
    n_sub = smgt_ref.shape[1]
    first_own = pl.program_id(1) * n_sub

    def produce_scores(n, h, col0=0):
        kb = k_ref[0, pl.ds(pl.multiple_of(n * nb, nb), nb), h * KX:(h + 1) * KX]
        s = _dot(kb, qx_ref[0, h, :, col0:])
        s_ref[h, :, col0:] = s
        smax_ref[h, :, col0:] = jnp.max(s, axis=0, keepdims=True)

    causal = (lax.broadcasted_iota(jnp.int32, (nb, nb), 0)
              <= lax.broadcasted_iota(jnp.int32, (nb, nb), 1))

    def attend(blocks, diagonal, next_block):
        assert QK_LEAD <= MOBA_HEADS - QK_AHEAD
        col0 = [0 if d is None else d * nb for d in diagonal]
        jobs = [(blocks[0], h, col0[0]) for h in range(QK_AHEAD, MOBA_HEADS)]
        jobs += [(n, h, c0) for n, c0 in zip(blocks[1:], col0[1:]) for h in range(MOBA_HEADS)]
        if next_block is not None:
            jobs += [(next_block, h, 0) for h in range(QK_AHEAD)]

        def issue():
            if jobs:
                produce_scores(*jobs.pop(0))

        for _ in range(QK_LEAD):
            issue()
        for n, diag, c0 in zip(blocks, diagonal, col0):
            for h in range(MOBA_HEADS):
                s = s_ref[h, :, c0:]
                smax = smax_ref[h, :, c0:]
                if diag is not None:
                    tri = jnp.where(causal, s[:, :nb], NEG_INF)
                    tri_max = jnp.max(tri, axis=0, keepdims=True)
                    if s.shape[1] > nb:
                        s = jnp.concatenate([tri, s[:, nb:]], axis=1)
                        smax = jnp.concatenate([tri_max, smax[:, nb:]], axis=1)
                    else:
                        s, smax = tri, tri_max
                vt = vt_ref[0, n, h * VX:(h + 1) * VX, :]
                m_old = m_ref[h, :, c0:]
                m_new = jnp.maximum(m_old, smax)
                m_ref[h, :, c0:] = m_new
                acc_ref[h, :, c0:] = (jnp.exp2(m_old - m_new) * acc_ref[h, :, c0:]
                                      + _dot(vt, jnp.exp2(s - m_new).astype(BF16)))
                issue()

    m_ref[...] = jnp.full(m_ref.shape, NEG_INF, F32)
    acc_ref[...] = jnp.zeros_like(acc_ref)
    for h in range(QK_AHEAD):
        produce_scores(0, h)

    assert n_sub % LOOP_BLOCKS == 0

    def body(i, c):
        n = i * LOOP_BLOCKS
        attend([n + d for d in range(LOOP_BLOCKS)], [None] * LOOP_BLOCKS, n + LOOP_BLOCKS)
        return c

    lax.fori_loop(0, first_own // LOOP_BLOCKS, body, 0)
    attend([first_own + j for j in range(n_sub)], list(range(n_sub)), None)
    for h in range(MOBA_HEADS):
        o = acc_ref[h, 0:hd, :] / acc_ref[h, hd:hd + 1, :]
        for j in range(n_sub):
            gate = smgt_ref[0, j, h * hd:(h + 1) * hd, :].astype(F32)
            o_ref[0, j, h * hd:(h + 1) * hd, :] = (o[:, j * nb:(j + 1) * nb] * gate).astype(BF16)


def _moba(qx, k, vt, smgt):
    B, nt, W, rt = smgt.shape
    S = k.shape[1]
    n_sub = MOBA_Q_TILE // rt
    qw = MOBA_Q_TILE
    tile = pl.BlockSpec((1, n_sub, W, rt), lambda b, t: (b, t, 0, 0))
    return pl.pallas_call(
        _moba_kernel,
        grid=(B, nt // n_sub),
        in_specs=[
            pl.BlockSpec((1, MOBA_HEADS, KX, qw), lambda b, t: (b, 0, 0, t)),
            pl.BlockSpec((1, S, MOBA_HEADS * KX), lambda b, t: (b, 0, 0)),
            pl.BlockSpec((1, nt, MOBA_HEADS * VX, rt), lambda b, t: (b, 0, 0, 0)),
            tile,
        ],
        out_specs=tile,
        out_shape=jax.ShapeDtypeStruct((B, nt, W, rt), BF16),
        scratch_shapes=[
            pltpu.VMEM((MOBA_HEADS, MOBA_BLOCK, qw), F32),
            pltpu.VMEM((MOBA_HEADS, 1, qw), F32),
            pltpu.VMEM((MOBA_HEADS, 1, qw), F32),
            pltpu.VMEM((MOBA_HEADS, VX, qw), F32),
        ],
        compiler_params=pltpu.CompilerParams(
            dimension_semantics=("arbitrary", "arbitrary"),
            vmem_limit_bytes=VMEM_LIMIT),
        name="moba",
    )(qx, k, vt, smgt)


def _gla_out_kernel(q_ref, k_ref, v_ref, la_ref, sgg_ref, gn_ref,
                    oat_ref, sga_ref, sgb_ref, x_ref, wpa_ref, wpb_ref, wo_ref, gf_ref,
                    y_ref, st_ref, *, final_norm):
    C = GLA_CHUNK
    hk, hv = GLA_HEAD_K, GLA_HEAD_V
    n_sub, rt = oat_ref.shape[1], oat_ref.shape[3]
    sub = lambda j: slice(j * rt, (j + 1) * rt)

    @pl.when(pl.program_id(1) == 0)
    def _():
        st_ref[...] = jnp.zeros_like(st_ref)

    nc = TAIL_STEP_ROWS // C
    H = GLA_HEADS
    tril_b = (lax.broadcasted_iota(jnp.int32, (C, C), 0)
              >= lax.broadcasted_iota(jnp.int32, (C, C), 1)).astype(BF16)
    tril_stack = (lax.broadcasted_iota(jnp.int32, (H * C, C), 0) % C
                  >= lax.broadcasted_iota(jnp.int32, (H * C, C), 1))
    lane_head = lax.broadcasted_iota(jnp.int32, (C, GLA_DK), 1) // hk
    zero_b = jnp.zeros((C, GLA_DK), BF16)
    gn = gn_ref[...]

    def stack_heads(x):
        return jnp.concatenate([jnp.where(lane_head == h, x, zero_b) for h in range(H)], axis=0)

    chunk = lambda c: slice(c * C, (c + 1) * C)
    g = la_ref[0]
    g_hi = g.astype(BF16)
    g_lo = (g - g_hi.astype(F32)).astype(BF16)
    b = [_dot(tril_b, g_hi[chunk(c)]) + _dot(tril_b, g_lo[chunk(c)]) for c in range(nc)]

    ya = [_dot_tn(oat_ref[0, j], wpa_ref[...]) for j in range(n_sub)]

    qe_s, qb_s, ke_b, kl_s, decay = [], [], [], [], []
    for c in range(nc):
        b_mid = b[c][C // 2 - 1:C // 2]
        b_last = b[c][C - 1:C]
        qe = q_ref[0, chunk(c), :].astype(F32) * jnp.exp(b[c] - b_mid)
        ke = k_ref[0, chunk(c), :].astype(F32) * jnp.exp(b_mid - b[c])
        qe_s.append(stack_heads(qe.astype(BF16)))
        qb_s.append(stack_heads((qe * jnp.exp(b_mid)).astype(BF16)))
        kl_s.append(stack_heads((ke * jnp.exp(b_last - b_mid)).astype(BF16)))
        ke_b.append(ke.astype(BF16))
        decay.append(jnp.exp(b_last))

    attn = [jnp.where(tril_stack, _dot_nt(qe_s[c], ke_b[c]), 0.0).astype(BF16) for c in range(nc)]
    intra = [jnp.concatenate(
        [_dot(attn[c][h * C:(h + 1) * C], v_ref[0, chunk(c), h * hv:(h + 1) * hv])
         for h in range(H)], axis=1) for c in range(nc)]

    v_s = [jnp.concatenate([v_ref[0, chunk(c), h * hv:(h + 1) * hv] for h in range(H)], axis=0)
           for c in range(nc)]
    upd = [_dot_tn(v_s[c], kl_s[c]) for c in range(nc)]

    st = st_ref[...]
    inter = []
    for c in range(nc):
        o_s = _dot_nt(qb_s[c], st.astype(BF16))
        inter.append(jnp.concatenate([o_s[h * C:(h + 1) * C] for h in range(H)], axis=1))
        st = st * decay[c] + upd[c]
    st_ref[...] = st

    ob = []
    for c in range(nc):
        o = inter[c] + intra[c]
        heads = []
        for h in range(H):
            oh = o[:, h * hv:(h + 1) * hv]
            ms = jnp.mean(oh * oh, axis=-1, keepdims=True)
            y = oh * lax.rsqrt(ms + RMS_EPS) * gn
            gate = sgg_ref[0, chunk(c), h * hv:(h + 1) * hv].astype(F32)
            heads.append((y * gate).astype(BF16))
        ob.append(jnp.concatenate(heads, axis=1))

    per_sub = rt // C
    yb = [_dot(jnp.concatenate(ob[j * per_sub:(j + 1) * per_sub], axis=0), wpb_ref[...])
          for j in range(n_sub)]
    merged = [(sga_ref[0, sub(j), :].astype(F32) * ya[j]
               + sgb_ref[0, sub(j), :].astype(F32) * yb[j]).astype(BF16) for j in range(n_sub)]
    proj = [_dot(merged[j], wo_ref[...]) for j in range(n_sub)]
    for j in range(n_sub):
        r = x_ref[0, sub(j), :] + proj[j]
        if final_norm:
            ms = jnp.mean(r * r, axis=-1, keepdims=True)
            r = r * lax.rsqrt(ms + RMS_EPS) * gf_ref[...]
        y_ref[0, sub(j), :] = r


def _gla_out(gq, gk, gv, la, sgg, gn, oat, sga, sgb, x, wpa, wpb, wo, gf, final_norm):
    B, S, D = x.shape
    rt = TAIL_STEP_ROWS
    n_sub = rt // ROW_TILE
    const = lambda shape: pl.BlockSpec(shape, lambda b, t: (0,) * len(shape))
    rows = lambda w: pl.BlockSpec((1, rt, w), lambda b, t: (b, t, 0))
    return pl.pallas_call(
        functools.partial(_gla_out_kernel, final_norm=final_norm),
        grid=(B, S // rt),
        in_specs=[
            rows(GLA_DK), rows(GLA_DK), rows(GLA_DV), rows(GLA_DK), rows(GLA_DV),
            const((1, GLA_HEAD_V)),
            pl.BlockSpec((1, n_sub, MOBA_WIDTH, ROW_TILE), lambda b, t: (b, t, 0, 0)),
            rows(D), rows(D), rows(D),
            const(wpa.shape), const(wpb.shape), const(wo.shape), const((1, D)),
        ],
        out_specs=rows(D),
        out_shape=jax.ShapeDtypeStruct((B, S, D), F32),
        scratch_shapes=[pltpu.VMEM((GLA_HEAD_V, GLA_DK), F32)],
        compiler_params=pltpu.CompilerParams(
            dimension_semantics=("arbitrary", "arbitrary"),
            vmem_limit_bytes=VMEM_LIMIT),
        name="gla_out",
    )(gq, gk, gv, la, sgg, gn, oat, sga, sgb, x, wpa, wpb, wo, gf)


def _rope_tables(S):
    inv_freq = 1.0 / (ROPE_THETA ** (jnp.arange(HALF, dtype=F32) / HALF))
    ang = jnp.arange(S, dtype=F32)[:, None] * inv_freq[None, :]
    return jnp.cos(ang), jnp.sin(ang)


def _swap_half_head(a):
    lead = a.shape[:-1]
    a = a.reshape(*lead, N_PAIRS, 2, 2, HALF)
    return jnp.swapaxes(a, -2, -3).reshape(*lead, MOBA_WIDTH)


def kernel(x, norm_in_g, w_in, b_merge, w_gla_fg2, b_gla_fg, gla_norm_g,
           w_proj_a, w_proj_b, w_out, norm_f_g):
    B, S, D = x.shape
    depth = w_in.shape[0]
    nt = S // MOBA_BLOCK
    cos, sin = _rope_tables(S)
    q_scale = MOBA_HEAD_DIM ** -0.5 * math.log2(math.e)
    cq = (jnp.concatenate([cos, cos], axis=1) * q_scale).T
    sq = (jnp.concatenate([-sin, sin], axis=1) * q_scale).T
    ck = jnp.tile(cos, (1, KX // HALF))
    sk = jnp.tile(sin, (1, KX // HALF)) * jnp.where(jnp.arange(KX) < KX_HI, -1.0, 1.0)[None, :]

    for layer in range(depth):
        w = w_in[layer]
        o = np.cumsum([0, MOBA_WIDTH, MOBA_WIDTH, MOBA_WIDTH, MOBA_WIDTH, GLA_DK, GLA_DK,
                       GLA_DV, GLA_DV, GLA_GATE_RANK, D_MODEL, D_MODEL])
        wq, wk, wv, wmg, wgq, wgk, wgv, wgg, wfg, wga, wgb = [
            w[:, o[i]:o[i + 1]] for i in range(11)]
        wn = jnp.concatenate([_swap_half_head(wk), wgq, wgk, wgv, wgg, wga, wgb],
                             axis=1).astype(BF16)
        wt = jnp.concatenate([wq, wv, wmg, wfg], axis=1).T.astype(BF16)
        wfg2 = w_gla_fg2[layer].astype(BF16)

        (qx, vt, smgt, k, gq, gk, gv, sgg, la, sga, sgb) = _in_proj(
            x, norm_in_g[layer][None, :], wn, wt, cq, sq, ck, sk, wfg2,
            b_gla_fg[layer][None, :], b_merge[layer])
        oat = _moba(qx, k, vt, smgt)
        x = _gla_out(gq, gk, gv, la, sgg, gla_norm_g[layer][None, :], oat, sga, sgb, x,
                     w_proj_a[layer].astype(BF16), w_proj_b[layer].astype(BF16),
                     w_out[layer].astype(BF16), norm_f_g[None, :],
                     final_norm=(layer == depth - 1))
    return x
```

```python
import functools
import math

import jax
import jax.numpy as jnp
import numpy as np
from jax import lax
from jax.experimental import pallas as pl
from jax.experimental.pallas import tpu as pltpu

F32 = jnp.float32
BF16 = jnp.bfloat16

D_MODEL = 1024
MOBA_HEADS = 8
MOBA_HEAD_DIM = 64
MOBA_WIDTH = MOBA_HEADS * MOBA_HEAD_DIM
MOBA_BLOCK = 256
MOBA_TOPK = 3
ROPE_THETA = 10000.0
GLA_HEADS = 4
GLA_DK = 256
GLA_DV = 512
GLA_HEAD_K = 64
GLA_HEAD_V = 128
GLA_GATE_RANK = 16
GLA_GATE_NORM = 16.0
GLA_CHUNK = 64
RMS_EPS = 1e-6
NEG_INF = -1e30

LANES = 128
BF16_ROWS = 16
ROW_TILE = MOBA_BLOCK
PAIR = 2 * MOBA_HEAD_DIM
N_PAIRS = MOBA_HEADS // 2
TAIL_STEP_ROWS = 512
IN_STEP_ROWS = 512
QK_AHEAD = 2
QK_LEAD = 1
MOBA_Q_TILE = 512
LOOP_BLOCKS = 2
VMEM_LIMIT = 56 * 1024 * 1024

HALF = MOBA_HEAD_DIM // 2
KX = LANES
KX_LO, KX_OH, KX_HI = 0, HALF, 2 * HALF
N_BLOCKS_MAX = BF16_ROWS
VX = MOBA_HEAD_DIM + BF16_ROWS

_C_K = 0
_C_GQ = _C_K + MOBA_WIDTH
_C_GK = _C_GQ + GLA_DK
_C_GV = _C_GK + GLA_DK
_C_GG = _C_GV + GLA_DV
_C_GA = _C_GG + GLA_DV
_C_GB = _C_GA + D_MODEL
_C_END = _C_GB + D_MODEL
_R_Q = 0
_R_V = _R_Q + MOBA_WIDTH
_R_MG = _R_V + MOBA_WIDTH
_R_FG = _R_MG + MOBA_WIDTH
_R_END = _R_FG + GLA_GATE_RANK


def _dot(a, b):
    return jnp.dot(a, b, preferred_element_type=F32)


def _dot_nt(a, b):
    return lax.dot_general(a, b, (((1,), (1,)), ((), ())), preferred_element_type=F32)


def _dot_tn(a, b):
    return lax.dot_general(a, b, (((0,), (0,)), ((), ())), preferred_element_type=F32)


def _sigmoid(x):
    return 1.0 / (1.0 + jnp.exp(-x))


def _silu(x):
    return x * _sigmoid(x)


def _select_blocks(gates, first_own, n_blocks, nb):
    H = MOBA_HEADS
    qw = gates.shape[1]
    col_own = (first_own + lax.broadcasted_iota(jnp.int32, (1, qw), 1) // nb).astype(F32)
    past = [col_own > n for n in range(n_blocks)]
    g = [jnp.where(past[n], gates[n * H:(n + 1) * H], NEG_INF) for n in range(n_blocks)]
    picked = [jnp.zeros((H, qw), jnp.bool_) for _ in range(n_blocks)]
    for _ in range(MOBA_TOPK):
        best = functools.reduce(jnp.maximum, g)
        first = functools.reduce(jnp.minimum, [
            jnp.where(g[n] == best, float(n), float(n_blocks)) for n in range(n_blocks)])
        for n in range(n_blocks):
            hit = first == float(n)
            picked[n] = picked[n] | hit
            g[n] = jnp.where(hit, -jnp.inf, g[n])
    keep = jnp.concatenate(
        [jnp.where((col_own == n) | (past[n] & picked[n]), 1.0, 0.0) for n in range(n_blocks)],
        axis=0).astype(BF16)
    r_out = lax.broadcasted_iota(jnp.int32, (H * n_blocks, n_blocks * H), 0)
    r_in = lax.broadcasted_iota(jnp.int32, (H * n_blocks, n_blocks * H), 1)
    perm = ((r_out // n_blocks == r_in % H) & (r_out % n_blocks == r_in // H)).astype(BF16)
    return jnp.where(_dot(perm, keep) > 0.5, 0.0, NEG_INF).astype(BF16)


def _in_proj_kernel(x_ref, g_ref, wn_ref, wt_ref, cq_ref, sq_ref, ck_ref, sk_ref,
                    wfg2_ref, bfg_ref, bm_ref,
                    qx_ref, vt_ref, smgt_ref, k_ref, gq_ref, gk_ref, gv_ref,
                    sgg_ref, la_ref, sga_ref, sgb_ref, km_ref, *, n_blocks):
    t = pl.program_id(1)
    rt = x_ref.shape[1]
    nb = MOBA_BLOCK
    n_sub = rt // nb
    hd = MOBA_HEAD_DIM
    H = MOBA_HEADS
    assert n_blocks <= N_BLOCKS_MAX
    blk_cols = lambda j: slice(j * nb, (j + 1) * nb)

    @pl.when(t == 0)
    def _():
        km_ref[...] = jnp.zeros_like(km_ref)

    x = x_ref[0]
    ms = jnp.mean(x * x, axis=-1, keepdims=True)
    hb = (x * lax.rsqrt(ms + RMS_EPS) * g_ref[...]).astype(BF16)

    pt = _dot_nt(wt_ref[...], hb)
    cq = cq_ref[...]
    sq = sq_ref[...]
    ones_rows = jnp.where(lax.broadcasted_iota(jnp.int32, (BF16_ROWS, nb), 0) == 0,
                          1.0, 0.0).astype(BF16)
    q_rot = []
    for h in range(H):
        blk = pt[_R_Q + h * hd:_R_Q + (h + 1) * hd]
        swapped = jnp.concatenate([blk[HALF:], blk[:HALF]], axis=0)
        q_rot.append((blk * cq + swapped * sq).astype(BF16))
        v_rows = pt[_R_V + h * hd:_R_V + (h + 1) * hd].astype(BF16)
        for j in range(n_sub):
            vt_ref[0, j, h * VX:h * VX + hd, :] = v_rows[:, blk_cols(j)]
            vt_ref[0, j, h * VX + hd:(h + 1) * VX, :] = ones_rows

    smg = _silu(pt[_R_MG:_R_FG]).astype(BF16)
    for j in range(n_sub):
        smgt_ref[0, j] = smg[:, blk_cols(j)]

    pk = _dot(hb, wn_ref[:, _C_K:_C_GQ])
    pg = _dot(hb, wn_ref[:, _C_GQ:_C_GA])
    ck = ck_ref[...]
    sk = sk_ref[...]
    lane = lax.broadcasted_iota(jnp.int32, (rt, KX), 1)
    row_block = t * n_sub + lax.broadcasted_iota(jnp.int32, (rt, KX), 0) // nb
    block_onehot = jnp.where(lane == KX_OH + row_block, 1.0, 0.0)
    own_lanes = (lane % (2 * HALF)) < HALF
    k_means = [[] for _ in range(n_sub)]
    for p in range(N_PAIRS):
        blk = pk[:, p * PAIR:(p + 1) * PAIR]
        kr = blk * ck + pltpu.roll(blk, 2 * HALF, axis=1) * sk
        for j in range(n_sub):
            k_means[j].append(jnp.mean(kr[blk_cols(j)], axis=0, keepdims=True))
        for hh, src in ((0, kr), (1, pltpu.roll(kr, KX - HALF, axis=1))):
            h = 2 * p + hh
            k_ref[0, :, h * KX:(h + 1) * KX] = (
                jnp.where(own_lanes, src, 0.0) + block_onehot).astype(BF16)
    km_lane = lax.broadcasted_iota(jnp.int32, (H, MOBA_WIDTH), 1)
    km_head = 2 * (km_lane // PAIR) + (km_lane // HALF) % 2
    head_lanes = km_head == lax.broadcasted_iota(jnp.int32, (H, MOBA_WIDTH), 0)
    for j in range(n_sub):
        mean_row = jnp.concatenate(k_means[j], axis=1)
        km_ref[pl.ds(pl.multiple_of((t * n_sub + j) * H, H), H), :] = jnp.where(
            head_lanes, mean_row, 0.0)

    q_pair = jnp.concatenate(
        [q_rot[2 * p + hh][part * HALF:(part + 1) * HALF]
         for p in range(N_PAIRS) for part in range(2) for hh in range(2)], axis=0)
    km = km_ref[...]
    km_hi = km.astype(BF16)
    km_lo = (km - km_hi.astype(F32)).astype(BF16)
    gates = _dot(km_hi, q_pair) + _dot(km_lo, q_pair)

    pm = _dot(hb, wn_ref[:, _C_GA:_C_END])

    mask_rows = _select_blocks(gates, t * n_sub, n_blocks, nb)
    pad = jnp.zeros((KX_HI - KX_OH - n_blocks, rt), BF16)
    tail = jnp.zeros((KX - KX_HI - HALF, rt), BF16)
    for h in range(H):
        qx_ref[0, h] = jnp.concatenate(
            [q_rot[h][:HALF], mask_rows[h * n_blocks:(h + 1) * n_blocks], pad,
             q_rot[h][HALF:], tail], axis=0)

    gq_ref[0] = (pg[:, _C_GQ - _C_GQ:_C_GK - _C_GQ] * (GLA_HEAD_K ** -0.5)).astype(BF16)
    gk_ref[0] = pg[:, _C_GK - _C_GQ:_C_GV - _C_GQ].astype(BF16)
    gv_ref[0] = pg[:, _C_GV - _C_GQ:_C_GG - _C_GQ].astype(BF16)
    sgg_ref[0] = _silu(pg[:, _C_GG - _C_GQ:_C_GA - _C_GQ]).astype(BF16)

    z = _dot_tn(pt[_R_FG:_R_END].astype(BF16), wfg2_ref[...]) + bfg_ref[...]
    log_sig = jnp.minimum(z, 0.0) - jnp.log1p(jnp.exp(-jnp.abs(z)))
    la_ref[0] = log_sig * (1.0 / GLA_GATE_NORM)

    sga_ref[0] = _sigmoid(pm[:, :D_MODEL] + bm_ref[0:1, :]).astype(BF16)
    sgb_ref[0] = _sigmoid(pm[:, D_MODEL:] + bm_ref[1:2, :]).astype(BF16)


def _in_proj(x, g, wn, wt, cq, sq, ck, sk, wfg2, bfg, bm):
    B, S, D = x.shape
    nt = S // ROW_TILE
    rt = IN_STEP_ROWS
    n_sub = rt // ROW_TILE
    const = lambda shape: pl.BlockSpec(shape, lambda b, t: (0,) * len(shape))
    t_blocked = lambda r: pl.BlockSpec((1, n_sub, r, ROW_TILE), lambda b, t: (b, t, 0, 0))
    rows = lambda w: pl.BlockSpec((1, rt, w), lambda b, t: (b, t, 0))
    t_shape = lambda r: jax.ShapeDtypeStruct((B, nt, r, ROW_TILE), BF16)
    rshape = lambda w, dt=BF16: jax.ShapeDtypeStruct((B, S, w), dt)
    return pl.pallas_call(
        functools.partial(_in_proj_kernel, n_blocks=nt),
        grid=(B, S // rt),
        in_specs=[
            rows(D), const((1, D)), const(wn.shape), const(wt.shape),
            pl.BlockSpec((MOBA_HEAD_DIM, rt), lambda b, t: (0, t)),
            pl.BlockSpec((MOBA_HEAD_DIM, rt), lambda b, t: (0, t)),
            pl.BlockSpec((rt, KX), lambda b, t: (t, 0)),
            pl.BlockSpec((rt, KX), lambda b, t: (t, 0)),
            const(wfg2.shape), const((1, GLA_DK)), const((2, D)),
        ],
        out_specs=[
            pl.BlockSpec((1, MOBA_HEADS, KX, rt), lambda b, t: (b, 0, 0, t)),
            t_blocked(MOBA_HEADS * VX), t_blocked(MOBA_WIDTH),
            rows(MOBA_HEADS * KX),
            rows(GLA_DK), rows(GLA_DK), rows(GLA_DV), rows(GLA_DV), rows(GLA_DK),
            rows(D), rows(D),
        ],
        out_shape=[
            jax.ShapeDtypeStruct((B, MOBA_HEADS, KX, S), BF16),
            t_shape(MOBA_HEADS * VX), t_shape(MOBA_WIDTH),
            rshape(MOBA_HEADS * KX),
            rshape(GLA_DK), rshape(GLA_DK), rshape(GLA_DV), rshape(GLA_DV),
            rshape(GLA_DK, F32), rshape(D), rshape(D),
        ],
        scratch_shapes=[pltpu.VMEM((nt * MOBA_HEADS, MOBA_WIDTH), F32)],
        compiler_params=pltpu.CompilerParams(
            dimension_semantics=("arbitrary", "arbitrary"),
            vmem_limit_bytes=VMEM_LIMIT),
        name="in_proj",
    )(x, g, wn, wt, cq, sq, ck, sk, wfg2, bfg, bm)


def _moba_kernel(qx_ref, k_ref, vt_ref, smgt_ref, o_ref, s_ref, smax_ref, m_ref, acc_ref):
    nb = MOBA_BLOCK
    hd = MOBA_HEAD_DIM
    n_sub = smgt_ref.shape[1]
    first_own = pl.program_id(1) * n_sub

    def produce_scores(n, h, col0=0):
        kb = k_ref[0, pl.ds(pl.multiple_of(n * nb, nb), nb), h * KX:(h + 1) * KX]
        s = _dot(kb, qx_ref[0, h, :, col0:])
        s_ref[h, :, col0:] = s
        smax_ref[h, :, col0:] = jnp.max(s, axis=0, keepdims=True)

    causal = (lax.broadcasted_iota(jnp.int32, (nb, nb), 0)
              <= lax.broadcasted_iota(jnp.int32, (nb, nb), 1))

    def attend(blocks, diagonal, next_block):
        assert QK_LEAD <= MOBA_HEADS - QK_AHEAD
        col0 = [0 if d is None else d * nb for d in diagonal]
        jobs = [(blocks[0], h, col0[0]) for h in range(QK_AHEAD, MOBA_HEADS)]
        jobs += [(n, h, c0) for n, c0 in zip(blocks[1:], col0[1:]) for h in range(MOBA_HEADS)]
        if next_block is not None:
            jobs += [(next_block, h, 0) for h in range(QK_AHEAD)]

        def issue():
            if jobs:
                produce_scores(*jobs.pop(0))

        for _ in range(QK_LEAD):
            issue()
        for n, diag, c0 in zip(blocks, diagonal, col0):
            for h in range(MOBA_HEADS):
                s = s_ref[h, :, c0:]
                smax = smax_ref[h, :, c0:]
                if diag is not None:
                    tri = jnp.where(causal, s[:, :nb], NEG_INF)
                    tri_max = jnp.max(tri, axis=0, keepdims=True)
                    if s.shape[1] > nb:
                        s = jnp.concatenate([tri, s[:, nb:]], axis=1)
                        smax = jnp.concatenate([tri_max, smax[:, nb:]], axis=1)
                    else:
                        s, smax = tri, tri_max
                vt = vt_ref[0, n, h * VX:(h + 1) * VX, :]
                m_old = m_ref[h, :, c0:]
                m_new = jnp.maximum(m_old, smax)
                m_ref[h, :, c0:] = m_new
                acc_ref[h, :, c0:] = (jnp.exp2(m_old - m_new) * acc_ref[h, :, c0:]
                                      + _dot(vt, jnp.exp2(s - m_new).astype(BF16)))
                issue()

    m_ref[...] = jnp.full(m_ref.shape, NEG_INF, F32)
    acc_ref[...] = jnp.zeros_like(acc_ref)
    for h in range(QK_AHEAD):
        produce_scores(0, h)

    assert n_sub % LOOP_BLOCKS == 0

    def body(i, c):
        n = i * LOOP_BLOCKS
        attend([n + d for d in range(LOOP_BLOCKS)], [None] * LOOP_BLOCKS, n + LOOP_BLOCKS)
        return c

    lax.fori_loop(0, first_own // LOOP_BLOCKS, body, 0)
    attend([first_own + j for j in range(n_sub)], list(range(n_sub)), None)
    for h in range(MOBA_HEADS):
        o = acc_ref[h, 0:hd, :] / acc_ref[h, hd:hd + 1, :]
        for j in range(n_sub):
            gate = smgt_ref[0, j, h * hd:(h + 1) * hd, :].astype(F32)
            o_ref[0, j, h * hd:(h + 1) * hd, :] = (o[:, j * nb:(j + 1) * nb] * gate).astype(BF16)


def _moba(qx, k, vt, smgt):
    B, nt, W, rt = smgt.shape
    S = k.shape[1]
    n_sub = MOBA_Q_TILE // rt
    qw = MOBA_Q_TILE
    tile = pl.BlockSpec((1, n_sub, W, rt), lambda b, t: (b, t, 0, 0))
    return pl.pallas_call(
        _moba_kernel,
        grid=(B, nt // n_sub),
        in_specs=[
            pl.BlockSpec((1, MOBA_HEADS, KX, qw), lambda b, t: (b, 0, 0, t)),
            pl.BlockSpec((1, S, MOBA_HEADS * KX), lambda b, t: (b, 0, 0)),
            pl.BlockSpec((1, nt, MOBA_HEADS * VX, rt), lambda b, t: (b, 0, 0, 0)),
            tile,
        ],
        out_specs=tile,
        out_shape=jax.ShapeDtypeStruct((B, nt, W, rt), BF16),
        scratch_shapes=[
            pltpu.VMEM((MOBA_HEADS, MOBA_BLOCK, qw), F32),
            pltpu.VMEM((MOBA_HEADS, 1, qw), F32),
            pltpu.VMEM((MOBA_HEADS, 1, qw), F32),
            pltpu.VMEM((MOBA_HEADS, VX, qw), F32),
        ],
        compiler_params=pltpu.CompilerParams(
            dimension_semantics=("arbitrary", "arbitrary"),
            vmem_limit_bytes=VMEM_LIMIT),
        name="moba",
    )(qx, k, vt, smgt)


def _gla_out_kernel(q_ref, k_ref, v_ref, la_ref, sgg_ref, gn_ref,
                    oat_ref, sga_ref, sgb_ref, x_ref, wpa_ref, wpb_ref, wo_ref, gf_ref,
                    y_ref, st_ref, *, final_norm):
    C = GLA_CHUNK
    hk, hv = GLA_HEAD_K, GLA_HEAD_V
    n_sub, rt = oat_ref.shape[1], oat_ref.shape[3]
    sub = lambda j: slice(j * rt, (j + 1) * rt)

    @pl.when(pl.program_id(1) == 0)
    def _():
        st_ref[...] = jnp.zeros_like(st_ref)

    nc = TAIL_STEP_ROWS // C
    H = GLA_HEADS
    tril_b = (lax.broadcasted_iota(jnp.int32, (C, C), 0)
              >= lax.broadcasted_iota(jnp.int32, (C, C), 1)).astype(BF16)
    tril_stack = (lax.broadcasted_iota(jnp.int32, (H * C, C), 0) % C
                  >= lax.broadcasted_iota(jnp.int32, (H * C, C), 1))
    lane_head = lax.broadcasted_iota(jnp.int32, (C, GLA_DK), 1) // hk
    zero_b = jnp.zeros((C, GLA_DK), BF16)
    gn = gn_ref[...]

    def stack_heads(x):
        return jnp.concatenate([jnp.where(lane_head == h, x, zero_b) for h in range(H)], axis=0)

    chunk = lambda c: slice(c * C, (c + 1) * C)
    g = la_ref[0]
    g_hi = g.astype(BF16)
    g_lo = (g - g_hi.astype(F32)).astype(BF16)
    b = [_dot(tril_b, g_hi[chunk(c)]) + _dot(tril_b, g_lo[chunk(c)]) for c in range(nc)]

    ya = [_dot_tn(oat_ref[0, j], wpa_ref[...]) for j in range(n_sub)]

    qe_s, qb_s, ke_b, kl_s, decay = [], [], [], [], []
    for c in range(nc):
        b_mid = b[c][C // 2 - 1:C // 2]
        b_last = b[c][C - 1:C]
        qe = q_ref[0, chunk(c), :].astype(F32) * jnp.exp(b[c] - b_mid)
        ke = k_ref[0, chunk(c), :].astype(F32) * jnp.exp(b_mid - b[c])
        qe_s.append(stack_heads(qe.astype(BF16)))
        qb_s.append(stack_heads((qe * jnp.exp(b_mid)).astype(BF16)))
        kl_s.append(stack_heads((ke * jnp.exp(b_last - b_mid)).astype(BF16)))
        ke_b.append(ke.astype(BF16))
        decay.append(jnp.exp(b_last))

    attn = [jnp.where(tril_stack, _dot_nt(qe_s[c], ke_b[c]), 0.0).astype(BF16) for c in range(nc)]
    intra = [jnp.concatenate(
        [_dot(attn[c][h * C:(h + 1) * C], v_ref[0, chunk(c), h * hv:(h + 1) * hv])
         for h in range(H)], axis=1) for c in range(nc)]

    v_s = [jnp.concatenate([v_ref[0, chunk(c), h * hv:(h + 1) * hv] for h in range(H)], axis=0)
           for c in range(nc)]
    upd = [_dot_tn(v_s[c], kl_s[c]) for c in range(nc)]

    st = st_ref[...]
    inter = []
    for c in range(nc):
        o_s = _dot_nt(qb_s[c], st.astype(BF16))
        inter.append(jnp.concatenate([o_s[h * C:(h + 1) * C] for h in range(H)], axis=1))
        st = st * decay[c] + upd[c]
    st_ref[...] = st

    ob = []
    for c in range(nc):
        o = inter[c] + intra[c]
        heads = []
        for h in range(H):
            oh = o[:, h * hv:(h + 1) * hv]
            ms = jnp.mean(oh * oh, axis=-1, keepdims=True)
            y = oh * lax.rsqrt(ms + RMS_EPS) * gn
            gate = sgg_ref[0, chunk(c), h * hv:(h + 1) * hv].astype(F32)
            heads.append((y * gate).astype(BF16))
        ob.append(jnp.concatenate(heads, axis=1))

    per_sub = rt // C
    yb = [_dot(jnp.concatenate(ob[j * per_sub:(j + 1) * per_sub], axis=0), wpb_ref[...])
          for j in range(n_sub)]
    merged = [(sga_ref[0, sub(j), :].astype(F32) * ya[j]
               + sgb_ref[0, sub(j), :].astype(F32) * yb[j]).astype(BF16) for j in range(n_sub)]
    proj = [_dot(merged[j], wo_ref[...]) for j in range(n_sub)]
    for j in range(n_sub):
        r = x_ref[0, sub(j), :] + proj[j]
        if final_norm:
            ms = jnp.mean(r * r, axis=-1, keepdims=True)
            r = r * lax.rsqrt(ms + RMS_EPS) * gf_ref[...]
        y_ref[0, sub(j), :] = r


def _gla_out(gq, gk, gv, la, sgg, gn, oat, sga, sgb, x, wpa, wpb, wo, gf, final_norm):
    B, S, D = x.shape
    rt = TAIL_STEP_ROWS
    n_sub = rt // ROW_TILE
    const = lambda shape: pl.BlockSpec(shape, lambda b, t: (0,) * len(shape))
    rows = lambda w: pl.BlockSpec((1, rt, w), lambda b, t: (b, t, 0))
    return pl.pallas_call(
        functools.partial(_gla_out_kernel, final_norm=final_norm),
        grid=(B, S // rt),
        in_specs=[
            rows(GLA_DK), rows(GLA_DK), rows(GLA_DV), rows(GLA_DK), rows(GLA_DV),
            const((1, GLA_HEAD_V)),
            pl.BlockSpec((1, n_sub, MOBA_WIDTH, ROW_TILE), lambda b, t: (b, t, 0, 0)),
            rows(D), rows(D), rows(D),
            const(wpa.shape), const(wpb.shape), const(wo.shape), const((1, D)),
        ],
        out_specs=rows(D),
        out_shape=jax.ShapeDtypeStruct((B, S, D), F32),
        scratch_shapes=[pltpu.VMEM((GLA_HEAD_V, GLA_DK), F32)],
        compiler_params=pltpu.CompilerParams(
            dimension_semantics=("arbitrary", "arbitrary"),
            vmem_limit_bytes=VMEM_LIMIT),
        name="gla_out",
    )(gq, gk, gv, la, sgg, gn, oat, sga, sgb, x, wpa, wpb, wo, gf)


def _rope_tables(S):
    inv_freq = 1.0 / (ROPE_THETA ** (jnp.arange(HALF, dtype=F32) / HALF))
    ang = jnp.arange(S, dtype=F32)[:, None] * inv_freq[None, :]
    return jnp.cos(ang), jnp.sin(ang)


def _swap_half_head(a):
    lead = a.shape[:-1]
    a = a.reshape(*lead, N_PAIRS, 2, 2, HALF)
    return jnp.swapaxes(a, -2, -3).reshape(*lead, MOBA_WIDTH)


def kernel(x, norm_in_g, w_in, b_merge, w_gla_fg2, b_gla_fg, gla_norm_g,
           w_proj_a, w_proj_b, w_out, norm_f_g):
    B, S, D = x.shape
    depth = w_in.shape[0]
    nt = S // MOBA_BLOCK
    cos, sin = _rope_tables(S)
    q_scale = MOBA_HEAD_DIM ** -0.5 * math.log2(math.e)
    cq = (jnp.concatenate([cos, cos], axis=1) * q_scale).T
    sq = (jnp.concatenate([-sin, sin], axis=1) * q_scale).T
    ck = jnp.tile(cos, (1, KX // HALF))
    sk = jnp.tile(sin, (1, KX // HALF)) * jnp.where(jnp.arange(KX) < KX_HI, -1.0, 1.0)[None, :]

    for layer in range(depth):
        w = w_in[layer]
        o = np.cumsum([0, MOBA_WIDTH, MOBA_WIDTH, MOBA_WIDTH, MOBA_WIDTH, GLA_DK, GLA_DK,
                       GLA_DV, GLA_DV, GLA_GATE_RANK, D_MODEL, D_MODEL])
        wq, wk, wv, wmg, wgq, wgk, wgv, wgg, wfg, wga, wgb = [
            w[:, o[i]:o[i + 1]] for i in range(11)]
        wn = jnp.concatenate([_swap_half_head(wk), wgq, wgk, wgv, wgg, wga, wgb],
                             axis=1).astype(BF16)
        wt = jnp.concatenate([wq, wv, wmg, wfg], axis=1).T.astype(BF16)
        wfg2 = w_gla_fg2[layer].astype(BF16)

        (qx, vt, smgt, k, gq, gk, gv, sgg, la, sga, sgb) = _in_proj(
            x, norm_in_g[layer][None, :], wn, wt, cq, sq, ck, sk, wfg2,
            b_gla_fg[layer][None, :], b_merge[layer])
        oat = _moba(qx, k, vt, smgt)
        x = _gla_out(gq, gk, gv, la, sgg, gla_norm_g[layer][None, :], oat, sga, sgb, x,
                     w_proj_a[layer].astype(BF16), w_proj_b[layer].astype(BF16),
                     w_out[layer].astype(BF16), norm_f_g[None, :],
                     final_norm=(layer == depth - 1))
    return x
```

```python
import functools
import math

import jax
import jax.numpy as jnp
import numpy as np
from jax import lax
from jax.experimental import pallas as pl
from jax.experimental.pallas import tpu as pltpu

F32 = jnp.float32
BF16 = jnp.bfloat16

D_MODEL = 1024
MOBA_HEADS = 8
MOBA_HEAD_DIM = 64
MOBA_WIDTH = MOBA_HEADS * MOBA_HEAD_DIM
MOBA_BLOCK = 256
MOBA_TOPK = 3
ROPE_THETA = 10000.0
GLA_HEADS = 4
GLA_DK = 256
GLA_DV = 512
GLA_HEAD_K = 64
GLA_HEAD_V = 128
GLA_GATE_RANK = 16
GLA_GATE_NORM = 16.0
GLA_CHUNK = 64
RMS_EPS = 1e-6
NEG_INF = -1e30

LANES = 128
BF16_ROWS = 16
ROW_TILE = MOBA_BLOCK
PAIR = 2 * MOBA_HEAD_DIM
N_PAIRS = MOBA_HEADS // 2
TAIL_STEP_ROWS = 512
IN_STEP_ROWS = 512
QK_AHEAD = 6
QK_LEAD = 1
MOBA_Q_TILE = 512
LOOP_BLOCKS = 2
VMEM_LIMIT = 56 * 1024 * 1024

HALF = MOBA_HEAD_DIM // 2
KX = LANES
KX_LO, KX_OH, KX_HI = 0, HALF, 2 * HALF
N_BLOCKS_MAX = BF16_ROWS
VX = MOBA_HEAD_DIM + BF16_ROWS

_C_K = 0
_C_GQ = _C_K + MOBA_WIDTH
_C_GK = _C_GQ + GLA_DK
_C_GV = _C_GK + GLA_DK
_C_GG = _C_GV + GLA_DV
_C_GA = _C_GG + GLA_DV
_C_GB = _C_GA + D_MODEL
_C_END = _C_GB + D_MODEL
_R_Q = 0
_R_V = _R_Q + MOBA_WIDTH
_R_MG = _R_V + MOBA_WIDTH
_R_FG = _R_MG + MOBA_WIDTH
_R_END = _R_FG + GLA_GATE_RANK


def _dot(a, b):
    return jnp.dot(a, b, preferred_element_type=F32)


def _dot_nt(a, b):
    return lax.dot_general(a, b, (((1,), (1,)), ((), ())), preferred_element_type=F32)


def _dot_tn(a, b):
    return lax.dot_general(a, b, (((0,), (0,)), ((), ())), preferred_element_type=F32)


def _sigmoid(x):
    return 1.0 / (1.0 + jnp.exp(-x))


def _silu(x):
    return x * _sigmoid(x)


def _select_blocks(gates, first_own, n_blocks, nb):
    H = MOBA_HEADS
    qw = gates.shape[1]
    col_own = (first_own + lax.broadcasted_iota(jnp.int32, (1, qw), 1) // nb).astype(F32)
    past = [col_own > n for n in range(n_blocks)]
    g = [jnp.where(past[n], gates[n * H:(n + 1) * H], NEG_INF) for n in range(n_blocks)]
    picked = [jnp.zeros((H, qw), jnp.bool_) for _ in range(n_blocks)]
    for _ in range(MOBA_TOPK):
        best = functools.reduce(jnp.maximum, g)
        first = functools.reduce(jnp.minimum, [
            jnp.where(g[n] == best, float(n), float(n_blocks)) for n in range(n_blocks)])
        for n in range(n_blocks):
            hit = first == float(n)
            picked[n] = picked[n] | hit
            g[n] = jnp.where(hit, -jnp.inf, g[n])
    keep = jnp.concatenate(
        [jnp.where((col_own == n) | (past[n] & picked[n]), 1.0, 0.0) for n in range(n_blocks)],
        axis=0).astype(BF16)
    r_out = lax.broadcasted_iota(jnp.int32, (H * n_blocks, n_blocks * H), 0)
    r_in = lax.broadcasted_iota(jnp.int32, (H * n_blocks, n_blocks * H), 1)
    perm = ((r_out // n_blocks == r_in % H) & (r_out % n_blocks == r_in // H)).astype(BF16)
    return jnp.where(_dot(perm, keep) > 0.5, 0.0, NEG_INF).astype(BF16)


def _in_proj_kernel(x_ref, g_ref, wn_ref, wt_ref, cq_ref, sq_ref, ck_ref, sk_ref,
                    wfg2_ref, bfg_ref, bm_ref,
                    qx_ref, vt_ref, smgt_ref, k_ref, gq_ref, gk_ref, gv_ref,
                    sgg_ref, la_ref, sga_ref, sgb_ref, km_ref, *, n_blocks):
    t = pl.program_id(1)
    rt = x_ref.shape[1]
    nb = MOBA_BLOCK
    n_sub = rt // nb
    hd = MOBA_HEAD_DIM
    H = MOBA_HEADS
    assert n_blocks <= N_BLOCKS_MAX
    blk_cols = lambda j: slice(j * nb, (j + 1) * nb)

    @pl.when(t == 0)
    def _():
        km_ref[...] = jnp.zeros_like(km_ref)

    x = x_ref[0]
    ms = jnp.mean(x * x, axis=-1, keepdims=True)
    hb = (x * lax.rsqrt(ms + RMS_EPS) * g_ref[...]).astype(BF16)

    pt = _dot_nt(wt_ref[...], hb)
    cq = cq_ref[...]
    sq = sq_ref[...]
    ones_rows = jnp.where(lax.broadcasted_iota(jnp.int32, (BF16_ROWS, nb), 0) == 0,
                          1.0, 0.0).astype(BF16)
    q_rot = []
    for h in range(H):
        blk = pt[_R_Q + h * hd:_R_Q + (h + 1) * hd]
        swapped = jnp.concatenate([blk[HALF:], blk[:HALF]], axis=0)
        q_rot.append((blk * cq + swapped * sq).astype(BF16))
        v_rows = pt[_R_V + h * hd:_R_V + (h + 1) * hd].astype(BF16)
        for j in range(n_sub):
            vt_ref[0, j, h * VX:h * VX + hd, :] = v_rows[:, blk_cols(j)]
            vt_ref[0, j, h * VX + hd:(h + 1) * VX, :] = ones_rows

    smg = _silu(pt[_R_MG:_R_FG]).astype(BF16)
    for j in range(n_sub):
        smgt_ref[0, j] = smg[:, blk_cols(j)]

    pk = _dot(hb, wn_ref[:, _C_K:_C_GQ])
    pg = _dot(hb, wn_ref[:, _C_GQ:_C_GA])
    ck = ck_ref[...]
    sk = sk_ref[...]
    lane = lax.broadcasted_iota(jnp.int32, (rt, KX), 1)
    row_block = t * n_sub + lax.broadcasted_iota(jnp.int32, (rt, KX), 0) // nb
    block_onehot = jnp.where(lane == KX_OH + row_block, 1.0, 0.0)
    own_lanes = (lane % (2 * HALF)) < HALF
    k_means = [[] for _ in range(n_sub)]
    for p in range(N_PAIRS):
        blk = pk[:, p * PAIR:(p + 1) * PAIR]
        kr = blk * ck + pltpu.roll(blk, 2 * HALF, axis=1) * sk
        for j in range(n_sub):
            k_means[j].append(jnp.mean(kr[blk_cols(j)], axis=0, keepdims=True))
        for hh, src in ((0, kr), (1, pltpu.roll(kr, KX - HALF, axis=1))):
            h = 2 * p + hh
            k_ref[0, :, h * KX:(h + 1) * KX] = (
                jnp.where(own_lanes, src, 0.0) + block_onehot).astype(BF16)
    km_lane = lax.broadcasted_iota(jnp.int32, (H, MOBA_WIDTH), 1)
    km_head = 2 * (km_lane // PAIR) + (km_lane // HALF) % 2
    head_lanes = km_head == lax.broadcasted_iota(jnp.int32, (H, MOBA_WIDTH), 0)
    for j in range(n_sub):
        mean_row = jnp.concatenate(k_means[j], axis=1)
        km_ref[pl.ds(pl.multiple_of((t * n_sub + j) * H, H), H), :] = jnp.where(
            head_lanes, mean_row, 0.0)

    q_pair = jnp.concatenate(
        [q_rot[2 * p + hh][part * HALF:(part + 1) * HALF]
         for p in range(N_PAIRS) for part in range(2) for hh in range(2)], axis=0)
    km = km_ref[...]
    km_hi = km.astype(BF16)
    km_lo = (km - km_hi.astype(F32)).astype(BF16)
    gates = _dot(km_hi, q_pair) + _dot(km_lo, q_pair)

    pm = _dot(hb, wn_ref[:, _C_GA:_C_END])

    mask_rows = _select_blocks(gates, t * n_sub, n_blocks, nb)
    pad = jnp.zeros((KX_HI - KX_OH - n_blocks, rt), BF16)
    tail = jnp.zeros((KX - KX_HI - HALF, rt), BF16)
    for h in range(H):
        qx_ref[0, h] = jnp.concatenate(
            [q_rot[h][:HALF], mask_rows[h * n_blocks:(h + 1) * n_blocks], pad,
             q_rot[h][HALF:], tail], axis=0)

    gq_ref[0] = (pg[:, _C_GQ - _C_GQ:_C_GK - _C_GQ] * (GLA_HEAD_K ** -0.5)).astype(BF16)
    gk_ref[0] = pg[:, _C_GK - _C_GQ:_C_GV - _C_GQ].astype(BF16)
    gv_ref[0] = pg[:, _C_GV - _C_GQ:_C_GG - _C_GQ].astype(BF16)
    sgg_ref[0] = _silu(pg[:, _C_GG - _C_GQ:_C_GA - _C_GQ]).astype(BF16)

    z = _dot_tn(pt[_R_FG:_R_END].astype(BF16), wfg2_ref[...]) + bfg_ref[...]
    log_sig = jnp.minimum(z, 0.0) - jnp.log1p(jnp.exp(-jnp.abs(z)))
    la_ref[0] = log_sig * (1.0 / GLA_GATE_NORM)

    sga_ref[0] = _sigmoid(pm[:, :D_MODEL] + bm_ref[0:1, :]).astype(BF16)
    sgb_ref[0] = _sigmoid(pm[:, D_MODEL:] + bm_ref[1:2, :]).astype(BF16)


def _in_proj(x, g, wn, wt, cq, sq, ck, sk, wfg2, bfg, bm):
    B, S, D = x.shape
    nt = S // ROW_TILE
    rt = IN_STEP_ROWS
    n_sub = rt // ROW_TILE
    const = lambda shape: pl.BlockSpec(shape, lambda b, t: (0,) * len(shape))
    t_blocked = lambda r: pl.BlockSpec((1, n_sub, r, ROW_TILE), lambda b, t: (b, t, 0, 0))
    rows = lambda w: pl.BlockSpec((1, rt, w), lambda b, t: (b, t, 0))
    t_shape = lambda r: jax.ShapeDtypeStruct((B, nt, r, ROW_TILE), BF16)
    rshape = lambda w, dt=BF16: jax.ShapeDtypeStruct((B, S, w), dt)
    return pl.pallas_call(
        functools.partial(_in_proj_kernel, n_blocks=nt),
        grid=(B, S // rt),
        in_specs=[
            rows(D), const((1, D)), const(wn.shape), const(wt.shape),
            pl.BlockSpec((MOBA_HEAD_DIM, rt), lambda b, t: (0, t)),
            pl.BlockSpec((MOBA_HEAD_DIM, rt), lambda b, t: (0, t)),
            pl.BlockSpec((rt, KX), lambda b, t: (t, 0)),
            pl.BlockSpec((rt, KX), lambda b, t: (t, 0)),
            const(wfg2.shape), const((1, GLA_DK)), const((2, D)),
        ],
        out_specs=[
            pl.BlockSpec((1, MOBA_HEADS, KX, rt), lambda b, t: (b, 0, 0, t)),
            t_blocked(MOBA_HEADS * VX), t_blocked(MOBA_WIDTH),
            rows(MOBA_HEADS * KX),
            rows(GLA_DK), rows(GLA_DK), rows(GLA_DV), rows(GLA_DV), rows(GLA_DK),
            rows(D), rows(D),
        ],
        out_shape=[
            jax.ShapeDtypeStruct((B, MOBA_HEADS, KX, S), BF16),
            t_shape(MOBA_HEADS * VX), t_shape(MOBA_WIDTH),
            rshape(MOBA_HEADS * KX),
            rshape(GLA_DK), rshape(GLA_DK), rshape(GLA_DV), rshape(GLA_DV),
            rshape(GLA_DK, F32), rshape(D), rshape(D),
        ],
        scratch_shapes=[pltpu.VMEM((nt * MOBA_HEADS, MOBA_WIDTH), F32)],
        compiler_params=pltpu.CompilerParams(
            dimension_semantics=("arbitrary", "arbitrary"),
            vmem_limit_bytes=VMEM_LIMIT),
        name="in_proj",
    )(x, g, wn, wt, cq, sq, ck, sk, wfg2, bfg, bm)


def _moba_kernel(qx_ref, k_ref, vt_ref, smgt_ref, o_ref, s_ref, smax_ref, m_ref, acc_ref):
    nb = MOBA_BLOCK
    hd = MOBA_HEAD_DIM
    n_sub = smgt_ref.shape[1]
    first_own = pl.program_id(1) * n_sub

    def produce_scores(n, h, col0=0):
        kb = k_ref[0, pl.ds(pl.multiple_of(n * nb, nb), nb), h * KX:(h + 1) * KX]
        s = _dot(kb, qx_ref[0, h, :, col0:])
        s_ref[h, :, col0:] = s
        smax_ref[h, :, col0:] = jnp.max(s, axis=0, keepdims=True)

    causal = (lax.broadcasted_iota(jnp.int32, (nb, nb), 0)
              <= lax.broadcasted_iota(jnp.int32, (nb, nb), 1))

    def attend(blocks, diagonal, next_block):
        assert QK_LEAD <= MOBA_HEADS - QK_AHEAD
        col0 = [0 if d is None else d * nb for d in diagonal]
        jobs = [(blocks[0], h, col0[0]) for h in range(QK_AHEAD, MOBA_HEADS)]
        jobs += [(n, h, c0) for n, c0 in zip(blocks[1:], col0[1:]) for h in range(MOBA_HEADS)]
        if next_block is not None:
            jobs += [(next_block, h, 0) for h in range(QK_AHEAD)]

        def issue():
            if jobs:
                produce_scores(*jobs.pop(0))

        for _ in range(QK_LEAD):
            issue()
        for n, diag, c0 in zip(blocks, diagonal, col0):
            for h in range(MOBA_HEADS):
                s = s_ref[h, :, c0:]
                smax = smax_ref[h, :, c0:]
                if diag is not None:
                    tri = jnp.where(causal, s[:, :nb], NEG_INF)
                    tri_max = jnp.max(tri, axis=0, keepdims=True)
                    if s.shape[1] > nb:
                        s = jnp.concatenate([tri, s[:, nb:]], axis=1)
                        smax = jnp.concatenate([tri_max, smax[:, nb:]], axis=1)
                    else:
                        s, smax = tri, tri_max
                vt = vt_ref[0, n, h * VX:(h + 1) * VX, :]
                m_old = m_ref[h, :, c0:]
                m_new = jnp.maximum(m_old, smax)
                m_ref[h, :, c0:] = m_new
                acc_ref[h, :, c0:] = (jnp.exp2(m_old - m_new) * acc_ref[h, :, c0:]
                                      + _dot(vt, jnp.exp2(s - m_new).astype(BF16)))
                issue()

    m_ref[...] = jnp.full(m_ref.shape, NEG_INF, F32)
    acc_ref[...] = jnp.zeros_like(acc_ref)
    for h in range(QK_AHEAD):
        produce_scores(0, h)

    assert n_sub % LOOP_BLOCKS == 0

    def body(i, c):
        n = i * LOOP_BLOCKS
        attend([n + d for d in range(LOOP_BLOCKS)], [None] * LOOP_BLOCKS, n + LOOP_BLOCKS)
        return c

    lax.fori_loop(0, first_own // LOOP_BLOCKS, body, 0)
    attend([first_own + j for j in range(n_sub)], list(range(n_sub)), None)
    for h in range(MOBA_HEADS):
        o = acc_ref[h, 0:hd, :] / acc_ref[h, hd:hd + 1, :]
        for j in range(n_sub):
            gate = smgt_ref[0, j, h * hd:(h + 1) * hd, :].astype(F32)
            o_ref[0, j, h * hd:(h + 1) * hd, :] = (o[:, j * nb:(j + 1) * nb] * gate).astype(BF16)


def _moba(qx, k, vt, smgt):
    B, nt, W, rt = smgt.shape
    S = k.shape[1]
    n_sub = MOBA_Q_TILE // rt
    qw = MOBA_Q_TILE
    tile = pl.BlockSpec((1, n_sub, W, rt), lambda b, t: (b, t, 0, 0))
    return pl.pallas_call(
        _moba_kernel,
        grid=(B, nt // n_sub),
        in_specs=[
            pl.BlockSpec((1, MOBA_HEADS, KX, qw), lambda b, t: (b, 0, 0, t)),
            pl.BlockSpec((1, S, MOBA_HEADS * KX), lambda b, t: (b, 0, 0)),
            pl.BlockSpec((1, nt, MOBA_HEADS * VX, rt), lambda b, t: (b, 0, 0, 0)),
            tile,
        ],
        out_specs=tile,
        out_shape=jax.ShapeDtypeStruct((B, nt, W, rt), BF16),
        scratch_shapes=[
            pltpu.VMEM((MOBA_HEADS, MOBA_BLOCK, qw), F32),
            pltpu.VMEM((MOBA_HEADS, 1, qw), F32),
            pltpu.VMEM((MOBA_HEADS, 1, qw), F32),
            pltpu.VMEM((MOBA_HEADS, VX, qw), F32),
        ],
        compiler_params=pltpu.CompilerParams(
            dimension_semantics=("arbitrary", "arbitrary"),
            vmem_limit_bytes=VMEM_LIMIT),
        name="moba",
    )(qx, k, vt, smgt)


def _gla_out_kernel(q_ref, k_ref, v_ref, la_ref, sgg_ref, gn_ref,
                    oat_ref, sga_ref, sgb_ref, x_ref, wpa_ref, wpb_ref, wo_ref, gf_ref,
                    y_ref, st_ref, *, final_norm):
    C = GLA_CHUNK
    hk, hv = GLA_HEAD_K, GLA_HEAD_V
    n_sub, rt = oat_ref.shape[1], oat_ref.shape[3]
    sub = lambda j: slice(j * rt, (j + 1) * rt)

    @pl.when(pl.program_id(1) == 0)
    def _():
        st_ref[...] = jnp.zeros_like(st_ref)

    nc = TAIL_STEP_ROWS // C
    H = GLA_HEADS
    tril_b = (lax.broadcasted_iota(jnp.int32, (C, C), 0)
              >= lax.broadcasted_iota(jnp.int32, (C, C), 1)).astype(BF16)
    tril_stack = (lax.broadcasted_iota(jnp.int32, (H * C, C), 0) % C
                  >= lax.broadcasted_iota(jnp.int32, (H * C, C), 1))
    lane_head = lax.broadcasted_iota(jnp.int32, (C, GLA_DK), 1) // hk
    zero_b = jnp.zeros((C, GLA_DK), BF16)
    gn = gn_ref[...]

    def stack_heads(x):
        return jnp.concatenate([jnp.where(lane_head == h, x, zero_b) for h in range(H)], axis=0)

    chunk = lambda c: slice(c * C, (c + 1) * C)
    g = la_ref[0]
    g_hi = g.astype(BF16)
    g_lo = (g - g_hi.astype(F32)).astype(BF16)
    b = [_dot(tril_b, g_hi[chunk(c)]) + _dot(tril_b, g_lo[chunk(c)]) for c in range(nc)]

    ya = [_dot_tn(oat_ref[0, j], wpa_ref[...]) for j in range(n_sub)]

    qe_s, qb_s, ke_b, kl_s, decay = [], [], [], [], []
    for c in range(nc):
        b_mid = b[c][C // 2 - 1:C // 2]
        b_last = b[c][C - 1:C]
        qe = q_ref[0, chunk(c), :].astype(F32) * jnp.exp(b[c] - b_mid)
        ke = k_ref[0, chunk(c), :].astype(F32) * jnp.exp(b_mid - b[c])
        qe_s.append(stack_heads(qe.astype(BF16)))
        qb_s.append(stack_heads((qe * jnp.exp(b_mid)).astype(BF16)))
        kl_s.append(stack_heads((ke * jnp.exp(b_last - b_mid)).astype(BF16)))
        ke_b.append(ke.astype(BF16))
        decay.append(jnp.exp(b_last))

    attn = [jnp.where(tril_stack, _dot_nt(qe_s[c], ke_b[c]), 0.0).astype(BF16) for c in range(nc)]
    intra = [jnp.concatenate(
        [_dot(attn[c][h * C:(h + 1) * C], v_ref[0, chunk(c), h * hv:(h + 1) * hv])
         for h in range(H)], axis=1) for c in range(nc)]

    v_s = [jnp.concatenate([v_ref[0, chunk(c), h * hv:(h + 1) * hv] for h in range(H)], axis=0)
           for c in range(nc)]
    upd = [_dot_tn(v_s[c], kl_s[c]) for c in range(nc)]

    st = st_ref[...]
    inter = []
    for c in range(nc):
        o_s = _dot_nt(qb_s[c], st.astype(BF16))
        inter.append(jnp.concatenate([o_s[h * C:(h + 1) * C] for h in range(H)], axis=1))
        st = st * decay[c] + upd[c]
    st_ref[...] = st

    ob = []
    for c in range(nc):
        o = inter[c] + intra[c]
        heads = []
        for h in range(H):
            oh = o[:, h * hv:(h + 1) * hv]
            ms = jnp.mean(oh * oh, axis=-1, keepdims=True)
            y = oh * lax.rsqrt(ms + RMS_EPS) * gn
            gate = sgg_ref[0, chunk(c), h * hv:(h + 1) * hv].astype(F32)
            heads.append((y * gate).astype(BF16))
        ob.append(jnp.concatenate(heads, axis=1))

    per_sub = rt // C
    yb = [_dot(jnp.concatenate(ob[j * per_sub:(j + 1) * per_sub], axis=0), wpb_ref[...])
          for j in range(n_sub)]
    merged = [(sga_ref[0, sub(j), :].astype(F32) * ya[j]
               + sgb_ref[0, sub(j), :].astype(F32) * yb[j]).astype(BF16) for j in range(n_sub)]
    proj = [_dot(merged[j], wo_ref[...]) for j in range(n_sub)]
    for j in range(n_sub):
        r = x_ref[0, sub(j), :] + proj[j]
        if final_norm:
            ms = jnp.mean(r * r, axis=-1, keepdims=True)
            r = r * lax.rsqrt(ms + RMS_EPS) * gf_ref[...]
        y_ref[0, sub(j), :] = r


def _gla_out(gq, gk, gv, la, sgg, gn, oat, sga, sgb, x, wpa, wpb, wo, gf, final_norm):
    B, S, D = x.shape
    rt = TAIL_STEP_ROWS
    n_sub = rt // ROW_TILE
    const = lambda shape: pl.BlockSpec(shape, lambda b, t: (0,) * len(shape))
    rows = lambda w: pl.BlockSpec((1, rt, w), lambda b, t: (b, t, 0))
    return pl.pallas_call(
        functools.partial(_gla_out_kernel, final_norm=final_norm),
        grid=(B, S // rt),
        in_specs=[
            rows(GLA_DK), rows(GLA_DK), rows(GLA_DV), rows(GLA_DK), rows(GLA_DV),
            const((1, GLA_HEAD_V)),
            pl.BlockSpec((1, n_sub, MOBA_WIDTH, ROW_TILE), lambda b, t: (b, t, 0, 0)),
            rows(D), rows(D), rows(D),
            const(wpa.shape), const(wpb.shape), const(wo.shape), const((1, D)),
        ],
        out_specs=rows(D),
        out_shape=jax.ShapeDtypeStruct((B, S, D), F32),
        scratch_shapes=[pltpu.VMEM((GLA_HEAD_V, GLA_DK), F32)],
        compiler_params=pltpu.CompilerParams(
            dimension_semantics=("arbitrary", "arbitrary"),
            vmem_limit_bytes=VMEM_LIMIT),
        name="gla_out",
    )(gq, gk, gv, la, sgg, gn, oat, sga, sgb, x, wpa, wpb, wo, gf)


def _rope_tables(S):
    inv_freq = 1.0 / (ROPE_THETA ** (jnp.arange(HALF, dtype=F32) / HALF))
    ang = jnp.arange(S, dtype=F32)[:, None] * inv_freq[None, :]
    return jnp.cos(ang), jnp.sin(ang)


def _swap_half_head(a):
    lead = a.shape[:-1]
    a = a.reshape(*lead, N_PAIRS, 2, 2, HALF)
    return jnp.swapaxes(a, -2, -3).reshape(*lead, MOBA_WIDTH)


def kernel(x, norm_in_g, w_in, b_merge, w_gla_fg2, b_gla_fg, gla_norm_g,
           w_proj_a, w_proj_b, w_out, norm_f_g):
    B, S, D = x.shape
    depth = w_in.shape[0]
    nt = S // MOBA_BLOCK
    cos, sin = _rope_tables(S)
    q_scale = MOBA_HEAD_DIM ** -0.5 * math.log2(math.e)
    cq = (jnp.concatenate([cos, cos], axis=1) * q_scale).T
    sq = (jnp.concatenate([-sin, sin], axis=1) * q_scale).T
    ck = jnp.tile(cos, (1, KX // HALF))
    sk = jnp.tile(sin, (1, KX // HALF)) * jnp.where(jnp.arange(KX) < KX_HI, -1.0, 1.0)[None, :]

    for layer in range(depth):
        w = w_in[layer]
        o = np.cumsum([0, MOBA_WIDTH, MOBA_WIDTH, MOBA_WIDTH, MOBA_WIDTH, GLA_DK, GLA_DK,
                       GLA_DV, GLA_DV, GLA_GATE_RANK, D_MODEL, D_MODEL])
        wq, wk, wv, wmg, wgq, wgk, wgv, wgg, wfg, wga, wgb = [
            w[:, o[i]:o[i + 1]] for i in range(11)]
        wn = jnp.concatenate([_swap_half_head(wk), wgq, wgk, wgv, wgg, wga, wgb],
                             axis=1).astype(BF16)
        wt = jnp.concatenate([wq, wv, wmg, wfg], axis=1).T.astype(BF16)
        wfg2 = w_gla_fg2[layer].astype(BF16)

        (qx, vt, smgt, k, gq, gk, gv, sgg, la, sga, sgb) = _in_proj(
            x, norm_in_g[layer][None, :], wn, wt, cq, sq, ck, sk, wfg2,
            b_gla_fg[layer][None, :], b_merge[layer])
        oat = _moba(qx, k, vt, smgt)
        x = _gla_out(gq, gk, gv, la, sgg, gla_norm_g[layer][None, :], oat, sga, sgb, x,
                     w_proj_a[layer].astype(BF16), w_proj_b[layer].astype(BF16),
                     w_out[layer].astype(BF16), norm_f_g[None, :],
                     final_norm=(layer == depth - 1))
    return x
```

```python
import functools
import math

import jax
import jax.numpy as jnp
import numpy as np
from jax import lax
from jax.experimental import pallas as pl
from jax.experimental.pallas import tpu as pltpu

F32 = jnp.float32
BF16 = jnp.bfloat16

D_MODEL = 1024
MOBA_HEADS = 8
MOBA_HEAD_DIM = 64
MOBA_WIDTH = MOBA_HEADS * MOBA_HEAD_DIM
MOBA_BLOCK = 256
MOBA_TOPK = 3
ROPE_THETA = 10000.0
GLA_HEADS = 4
GLA_DK = 256
GLA_DV = 512
GLA_HEAD_K = 64
GLA_HEAD_V = 128
GLA_GATE_RANK = 16
GLA_GATE_NORM = 16.0
GLA_CHUNK = 64
RMS_EPS = 1e-6
NEG_INF = -1e30

LANES = 128
BF16_ROWS = 16
ROW_TILE = MOBA_BLOCK
PAIR = 2 * MOBA_HEAD_DIM
N_PAIRS = MOBA_HEADS // 2
TAIL_STEP_ROWS = 1024
IN_STEP_ROWS = 512
QK_AHEAD = 4
QK_LEAD = 1
MOBA_Q_TILE = 512
LOOP_BLOCKS = 2
VMEM_LIMIT = 56 * 1024 * 1024

HALF = MOBA_HEAD_DIM // 2
KX = LANES
KX_LO, KX_OH, KX_HI = 0, HALF, 2 * HALF
N_BLOCKS_MAX = BF16_ROWS
VX = MOBA_HEAD_DIM + BF16_ROWS

_C_K = 0
_C_GQ = _C_K + MOBA_WIDTH
_C_GK = _C_GQ + GLA_DK
_C_GV = _C_GK + GLA_DK
_C_GG = _C_GV + GLA_DV
_C_GA = _C_GG + GLA_DV
_C_GB = _C_GA + D_MODEL
_C_END = _C_GB + D_MODEL
_R_Q = 0
_R_V = _R_Q + MOBA_WIDTH
_R_MG = _R_V + MOBA_WIDTH
_R_FG = _R_MG + MOBA_WIDTH
_R_END = _R_FG + GLA_GATE_RANK


def _dot(a, b):
    return jnp.dot(a, b, preferred_element_type=F32)


def _dot_nt(a, b):
    return lax.dot_general(a, b, (((1,), (1,)), ((), ())), preferred_element_type=F32)


def _dot_tn(a, b):
    return lax.dot_general(a, b, (((0,), (0,)), ((), ())), preferred_element_type=F32)


def _sigmoid(x):
    return 1.0 / (1.0 + jnp.exp(-x))


def _silu(x):
    return x * _sigmoid(x)


def _select_blocks(gates, first_own, n_blocks, nb):
    H = MOBA_HEADS
    qw = gates.shape[1]
    col_own = (first_own + lax.broadcasted_iota(jnp.int32, (1, qw), 1) // nb).astype(F32)
    past = [col_own > n for n in range(n_blocks)]
    g = [jnp.where(past[n], gates[n * H:(n + 1) * H], NEG_INF) for n in range(n_blocks)]
    picked = [jnp.zeros((H, qw), jnp.bool_) for _ in range(n_blocks)]
    for _ in range(MOBA_TOPK):
        best = functools.reduce(jnp.maximum, g)
        first = functools.reduce(jnp.minimum, [
            jnp.where(g[n] == best, float(n), float(n_blocks)) for n in range(n_blocks)])
        for n in range(n_blocks):
            hit = first == float(n)
            picked[n] = picked[n] | hit
            g[n] = jnp.where(hit, -jnp.inf, g[n])
    keep = jnp.concatenate(
        [jnp.where((col_own == n) | (past[n] & picked[n]), 1.0, 0.0) for n in range(n_blocks)],
        axis=0).astype(BF16)
    r_out = lax.broadcasted_iota(jnp.int32, (H * n_blocks, n_blocks * H), 0)
    r_in = lax.broadcasted_iota(jnp.int32, (H * n_blocks, n_blocks * H), 1)
    perm = ((r_out // n_blocks == r_in % H) & (r_out % n_blocks == r_in // H)).astype(BF16)
    return jnp.where(_dot(perm, keep) > 0.5, 0.0, NEG_INF).astype(BF16)


def _in_proj_kernel(x_ref, g_ref, wn_ref, wt_ref, cq_ref, sq_ref, ck_ref, sk_ref,
                    wfg2_ref, bfg_ref, bm_ref,
                    qx_ref, vt_ref, smgt_ref, k_ref, gq_ref, gk_ref, gv_ref,
                    sgg_ref, la_ref, sga_ref, sgb_ref, km_ref, *, n_blocks):
    t = pl.program_id(1)
    rt = x_ref.shape[1]
    nb = MOBA_BLOCK
    n_sub = rt // nb
    hd = MOBA_HEAD_DIM
    H = MOBA_HEADS
    assert n_blocks <= N_BLOCKS_MAX
    blk_cols = lambda j: slice(j * nb, (j + 1) * nb)

    @pl.when(t == 0)
    def _():
        km_ref[...] = jnp.zeros_like(km_ref)

    x = x_ref[0]
    ms = jnp.mean(x * x, axis=-1, keepdims=True)
    hb = (x * lax.rsqrt(ms + RMS_EPS) * g_ref[...]).astype(BF16)

    pt = _dot_nt(wt_ref[...], hb)
    cq = cq_ref[...]
    sq = sq_ref[...]
    ones_rows = jnp.where(lax.broadcasted_iota(jnp.int32, (BF16_ROWS, nb), 0) == 0,
                          1.0, 0.0).astype(BF16)
    q_rot = []
    for h in range(H):
        blk = pt[_R_Q + h * hd:_R_Q + (h + 1) * hd]
        swapped = jnp.concatenate([blk[HALF:], blk[:HALF]], axis=0)
        q_rot.append((blk * cq + swapped * sq).astype(BF16))
        v_rows = pt[_R_V + h * hd:_R_V + (h + 1) * hd].astype(BF16)
        for j in range(n_sub):
            vt_ref[0, j, h * VX:h * VX + hd, :] = v_rows[:, blk_cols(j)]
            vt_ref[0, j, h * VX + hd:(h + 1) * VX, :] = ones_rows

    smg = _silu(pt[_R_MG:_R_FG]).astype(BF16)
    for j in range(n_sub):
        smgt_ref[0, j] = smg[:, blk_cols(j)]

    pk = _dot(hb, wn_ref[:, _C_K:_C_GQ])
    pg = _dot(hb, wn_ref[:, _C_GQ:_C_GA])
    ck = ck_ref[...]
    sk = sk_ref[...]
    lane = lax.broadcasted_iota(jnp.int32, (rt, KX), 1)
    row_block = t * n_sub + lax.broadcasted_iota(jnp.int32, (rt, KX), 0) // nb
    block_onehot = jnp.where(lane == KX_OH + row_block, 1.0, 0.0)
    own_lanes = (lane % (2 * HALF)) < HALF
    k_means = [[] for _ in range(n_sub)]
    for p in range(N_PAIRS):
        blk = pk[:, p * PAIR:(p + 1) * PAIR]
        kr = blk * ck + pltpu.roll(blk, 2 * HALF, axis=1) * sk
        for j in range(n_sub):
            k_means[j].append(jnp.mean(kr[blk_cols(j)], axis=0, keepdims=True))
        for hh, src in ((0, kr), (1, pltpu.roll(kr, KX - HALF, axis=1))):
            h = 2 * p + hh
            k_ref[0, :, h * KX:(h + 1) * KX] = (
                jnp.where(own_lanes, src, 0.0) + block_onehot).astype(BF16)
    km_lane = lax.broadcasted_iota(jnp.int32, (H, MOBA_WIDTH), 1)
    km_head = 2 * (km_lane // PAIR) + (km_lane // HALF) % 2
    head_lanes = km_head == lax.broadcasted_iota(jnp.int32, (H, MOBA_WIDTH), 0)
    for j in range(n_sub):
        mean_row = jnp.concatenate(k_means[j], axis=1)
        km_ref[pl.ds(pl.multiple_of((t * n_sub + j) * H, H), H), :] = jnp.where(
            head_lanes, mean_row, 0.0)

    q_pair = jnp.concatenate(
        [q_rot[2 * p + hh][part * HALF:(part + 1) * HALF]
         for p in range(N_PAIRS) for part in range(2) for hh in range(2)], axis=0)
    km = km_ref[...]
    km_hi = km.astype(BF16)
    km_lo = (km - km_hi.astype(F32)).astype(BF16)
    gates = _dot(km_hi, q_pair) + _dot(km_lo, q_pair)

    pm = _dot(hb, wn_ref[:, _C_GA:_C_END])

    mask_rows = _select_blocks(gates, t * n_sub, n_blocks, nb)
    pad = jnp.zeros((KX_HI - KX_OH - n_blocks, rt), BF16)
    tail = jnp.zeros((KX - KX_HI - HALF, rt), BF16)
    for h in range(H):
        qx_ref[0, h] = jnp.concatenate(
            [q_rot[h][:HALF], mask_rows[h * n_blocks:(h + 1) * n_blocks], pad,
             q_rot[h][HALF:], tail], axis=0)

    gq_ref[0] = (pg[:, _C_GQ - _C_GQ:_C_GK - _C_GQ] * (GLA_HEAD_K ** -0.5)).astype(BF16)
    gk_ref[0] = pg[:, _C_GK - _C_GQ:_C_GV - _C_GQ].astype(BF16)
    gv_ref[0] = pg[:, _C_GV - _C_GQ:_C_GG - _C_GQ].astype(BF16)
    sgg_ref[0] = _silu(pg[:, _C_GG - _C_GQ:_C_GA - _C_GQ]).astype(BF16)

    z = _dot_tn(pt[_R_FG:_R_END].astype(BF16), wfg2_ref[...]) + bfg_ref[...]
    log_sig = jnp.minimum(z, 0.0) - jnp.log1p(jnp.exp(-jnp.abs(z)))
    la_ref[0] = log_sig * (1.0 / GLA_GATE_NORM)

    sga_ref[0] = _sigmoid(pm[:, :D_MODEL] + bm_ref[0:1, :]).astype(BF16)
    sgb_ref[0] = _sigmoid(pm[:, D_MODEL:] + bm_ref[1:2, :]).astype(BF16)


def _in_proj(x, g, wn, wt, cq, sq, ck, sk, wfg2, bfg, bm):
    B, S, D = x.shape
    nt = S // ROW_TILE
    rt = IN_STEP_ROWS
    n_sub = rt // ROW_TILE
    const = lambda shape: pl.BlockSpec(shape, lambda b, t: (0,) * len(shape))
    t_blocked = lambda r: pl.BlockSpec((1, n_sub, r, ROW_TILE), lambda b, t: (b, t, 0, 0))
    rows = lambda w: pl.BlockSpec((1, rt, w), lambda b, t: (b, t, 0))
    t_shape = lambda r: jax.ShapeDtypeStruct((B, nt, r, ROW_TILE), BF16)
    rshape = lambda w, dt=BF16: jax.ShapeDtypeStruct((B, S, w), dt)
    return pl.pallas_call(
        functools.partial(_in_proj_kernel, n_blocks=nt),
        grid=(B, S // rt),
        in_specs=[
            rows(D), const((1, D)), const(wn.shape), const(wt.shape),
            pl.BlockSpec((MOBA_HEAD_DIM, rt), lambda b, t: (0, t)),
            pl.BlockSpec((MOBA_HEAD_DIM, rt), lambda b, t: (0, t)),
            pl.BlockSpec((rt, KX), lambda b, t: (t, 0)),
            pl.BlockSpec((rt, KX), lambda b, t: (t, 0)),
            const(wfg2.shape), const((1, GLA_DK)), const((2, D)),
        ],
        out_specs=[
            pl.BlockSpec((1, MOBA_HEADS, KX, rt), lambda b, t: (b, 0, 0, t)),
            t_blocked(MOBA_HEADS * VX), t_blocked(MOBA_WIDTH),
            rows(MOBA_HEADS * KX),
            rows(GLA_DK), rows(GLA_DK), rows(GLA_DV), rows(GLA_DV), rows(GLA_DK),
            rows(D), rows(D),
        ],
        out_shape=[
            jax.ShapeDtypeStruct((B, MOBA_HEADS, KX, S), BF16),
            t_shape(MOBA_HEADS * VX), t_shape(MOBA_WIDTH),
            rshape(MOBA_HEADS * KX),
            rshape(GLA_DK), rshape(GLA_DK), rshape(GLA_DV), rshape(GLA_DV),
            rshape(GLA_DK, F32), rshape(D), rshape(D),
        ],
        scratch_shapes=[pltpu.VMEM((nt * MOBA_HEADS, MOBA_WIDTH), F32)],
        compiler_params=pltpu.CompilerParams(
            dimension_semantics=("arbitrary", "arbitrary"),
            vmem_limit_bytes=VMEM_LIMIT),
        name="in_proj",
    )(x, g, wn, wt, cq, sq, ck, sk, wfg2, bfg, bm)


def _moba_kernel(qx_ref, k_ref, vt_ref, smgt_ref, o_ref, s_ref, smax_ref, m_ref, acc_ref):
    nb = MOBA_BLOCK
    hd = MOBA_HEAD_DIM
    n_sub = smgt_ref.shape[1]
    first_own = pl.program_id(1) * n_sub

    def produce_scores(n, h, col0=0):
        kb = k_ref[0, pl.ds(pl.multiple_of(n * nb, nb), nb), h * KX:(h + 1) * KX]
        s = _dot(kb, qx_ref[0, h, :, col0:])
        s_ref[h, :, col0:] = s
        smax_ref[h, :, col0:] = jnp.max(s, axis=0, keepdims=True)

    causal = (lax.broadcasted_iota(jnp.int32, (nb, nb), 0)
              <= lax.broadcasted_iota(jnp.int32, (nb, nb), 1))

    def attend(blocks, diagonal, next_block):
        assert QK_LEAD <= MOBA_HEADS - QK_AHEAD
        col0 = [0 if d is None else d * nb for d in diagonal]
        jobs = [(blocks[0], h, col0[0]) for h in range(QK_AHEAD, MOBA_HEADS)]
        jobs += [(n, h, c0) for n, c0 in zip(blocks[1:], col0[1:]) for h in range(MOBA_HEADS)]
        if next_block is not None:
            jobs += [(next_block, h, 0) for h in range(QK_AHEAD)]

        def issue():
            if jobs:
                produce_scores(*jobs.pop(0))

        for _ in range(QK_LEAD):
            issue()
        for n, diag, c0 in zip(blocks, diagonal, col0):
            for h in range(MOBA_HEADS):
                s = s_ref[h, :, c0:]
                smax = smax_ref[h, :, c0:]
                if diag is not None:
                    tri = jnp.where(causal, s[:, :nb], NEG_INF)
                    tri_max = jnp.max(tri, axis=0, keepdims=True)
                    if s.shape[1] > nb:
                        s = jnp.concatenate([tri, s[:, nb:]], axis=1)
                        smax = jnp.concatenate([tri_max, smax[:, nb:]], axis=1)
                    else:
                        s, smax = tri, tri_max
                vt = vt_ref[0, n, h * VX:(h + 1) * VX, :]
                m_old = m_ref[h, :, c0:]
                m_new = jnp.maximum(m_old, smax)
                m_ref[h, :, c0:] = m_new
                acc_ref[h, :, c0:] = (jnp.exp2(m_old - m_new) * acc_ref[h, :, c0:]
                                      + _dot(vt, jnp.exp2(s - m_new).astype(BF16)))
                issue()

    m_ref[...] = jnp.full(m_ref.shape, NEG_INF, F32)
    acc_ref[...] = jnp.zeros_like(acc_ref)
    for h in range(QK_AHEAD):
        produce_scores(0, h)

    assert n_sub % LOOP_BLOCKS == 0

    def body(i, c):
        n = i * LOOP_BLOCKS
        attend([n + d for d in range(LOOP_BLOCKS)], [None] * LOOP_BLOCKS, n + LOOP_BLOCKS)
        return c

    lax.fori_loop(0, first_own // LOOP_BLOCKS, body, 0)
    attend([first_own + j for j in range(n_sub)], list(range(n_sub)), None)
    for h in range(MOBA_HEADS):
        o = acc_ref[h, 0:hd, :] / acc_ref[h, hd:hd + 1, :]
        for j in range(n_sub):
            gate = smgt_ref[0, j, h * hd:(h + 1) * hd, :].astype(F32)
            o_ref[0, j, h * hd:(h + 1) * hd, :] = (o[:, j * nb:(j + 1) * nb] * gate).astype(BF16)


def _moba(qx, k, vt, smgt):
    B, nt, W, rt = smgt.shape
    S = k.shape[1]
    n_sub = MOBA_Q_TILE // rt
    qw = MOBA_Q_TILE
    tile = pl.BlockSpec((1, n_sub, W, rt), lambda b, t: (b, t, 0, 0))
    return pl.pallas_call(
        _moba_kernel,
        grid=(B, nt // n_sub),
        in_specs=[
            pl.BlockSpec((1, MOBA_HEADS, KX, qw), lambda b, t: (b, 0, 0, t)),
            pl.BlockSpec((1, S, MOBA_HEADS * KX), lambda b, t: (b, 0, 0)),
            pl.BlockSpec((1, nt, MOBA_HEADS * VX, rt), lambda b, t: (b, 0, 0, 0)),
            tile,
        ],
        out_specs=tile,
        out_shape=jax.ShapeDtypeStruct((B, nt, W, rt), BF16),
        scratch_shapes=[
            pltpu.VMEM((MOBA_HEADS, MOBA_BLOCK, qw), F32),
            pltpu.VMEM((MOBA_HEADS, 1, qw), F32),
            pltpu.VMEM((MOBA_HEADS, 1, qw), F32),
            pltpu.VMEM((MOBA_HEADS, VX, qw), F32),
        ],
        compiler_params=pltpu.CompilerParams(
            dimension_semantics=("arbitrary", "arbitrary"),
            vmem_limit_bytes=VMEM_LIMIT),
        name="moba",
    )(qx, k, vt, smgt)


def _gla_out_kernel(q_ref, k_ref, v_ref, la_ref, sgg_ref, gn_ref,
                    oat_ref, sga_ref, sgb_ref, x_ref, wpa_ref, wpb_ref, wo_ref, gf_ref,
                    y_ref, st_ref, *, final_norm):
    C = GLA_CHUNK
    hk, hv = GLA_HEAD_K, GLA_HEAD_V
    n_sub, rt = oat_ref.shape[1], oat_ref.shape[3]
    sub = lambda j: slice(j * rt, (j + 1) * rt)

    @pl.when(pl.program_id(1) == 0)
    def _():
        st_ref[...] = jnp.zeros_like(st_ref)

    nc = TAIL_STEP_ROWS // C
    H = GLA_HEADS
    tril_b = (lax.broadcasted_iota(jnp.int32, (C, C), 0)
              >= lax.broadcasted_iota(jnp.int32, (C, C), 1)).astype(BF16)
    tril_stack = (lax.broadcasted_iota(jnp.int32, (H * C, C), 0) % C
                  >= lax.broadcasted_iota(jnp.int32, (H * C, C), 1))
    lane_head = lax.broadcasted_iota(jnp.int32, (C, GLA_DK), 1) // hk
    zero_b = jnp.zeros((C, GLA_DK), BF16)
    gn = gn_ref[...]

    def stack_heads(x):
        return jnp.concatenate([jnp.where(lane_head == h, x, zero_b) for h in range(H)], axis=0)

    chunk = lambda c: slice(c * C, (c + 1) * C)
    g = la_ref[0]
    g_hi = g.astype(BF16)
    g_lo = (g - g_hi.astype(F32)).astype(BF16)
    b = [_dot(tril_b, g_hi[chunk(c)]) + _dot(tril_b, g_lo[chunk(c)]) for c in range(nc)]

    ya = [_dot_tn(oat_ref[0, j], wpa_ref[...]) for j in range(n_sub)]

    qe_s, qb_s, ke_b, kl_s, decay = [], [], [], [], []
    for c in range(nc):
        b_mid = b[c][C // 2 - 1:C // 2]
        b_last = b[c][C - 1:C]
        qe = q_ref[0, chunk(c), :].astype(F32) * jnp.exp(b[c] - b_mid)
        ke = k_ref[0, chunk(c), :].astype(F32) * jnp.exp(b_mid - b[c])
        qe_s.append(stack_heads(qe.astype(BF16)))
        qb_s.append(stack_heads((qe * jnp.exp(b_mid)).astype(BF16)))
        kl_s.append(stack_heads((ke * jnp.exp(b_last - b_mid)).astype(BF16)))
        ke_b.append(ke.astype(BF16))
        decay.append(jnp.exp(b_last))

    attn = [jnp.where(tril_stack, _dot_nt(qe_s[c], ke_b[c]), 0.0).astype(BF16) for c in range(nc)]
    intra = [jnp.concatenate(
        [_dot(attn[c][h * C:(h + 1) * C], v_ref[0, chunk(c), h * hv:(h + 1) * hv])
         for h in range(H)], axis=1) for c in range(nc)]

    v_s = [jnp.concatenate([v_ref[0, chunk(c), h * hv:(h + 1) * hv] for h in range(H)], axis=0)
           for c in range(nc)]
    upd = [_dot_tn(v_s[c], kl_s[c]) for c in range(nc)]

    st = st_ref[...]
    inter = []
    for c in range(nc):
        o_s = _dot_nt(qb_s[c], st.astype(BF16))
        inter.append(jnp.concatenate([o_s[h * C:(h + 1) * C] for h in range(H)], axis=1))
        st = st * decay[c] + upd[c]
    st_ref[...] = st

    ob = []
    for c in range(nc):
        o = inter[c] + intra[c]
        heads = []
        for h in range(H):
            oh = o[:, h * hv:(h + 1) * hv]
            ms = jnp.mean(oh * oh, axis=-1, keepdims=True)
            y = oh * lax.rsqrt(ms + RMS_EPS) * gn
            gate = sgg_ref[0, chunk(c), h * hv:(h + 1) * hv].astype(F32)
            heads.append((y * gate).astype(BF16))
        ob.append(jnp.concatenate(heads, axis=1))

    per_sub = rt // C
    yb = [_dot(jnp.concatenate(ob[j * per_sub:(j + 1) * per_sub], axis=0), wpb_ref[...])
          for j in range(n_sub)]
    merged = [(sga_ref[0, sub(j), :].astype(F32) * ya[j]
               + sgb_ref[0, sub(j), :].astype(F32) * yb[j]).astype(BF16) for j in range(n_sub)]
    proj = [_dot(merged[j], wo_ref[...]) for j in range(n_sub)]
    for j in range(n_sub):
        r = x_ref[0, sub(j), :] + proj[j]
        if final_norm:
            ms = jnp.mean(r * r, axis=-1, keepdims=True)
            r = r * lax.rsqrt(ms + RMS_EPS) * gf_ref[...]
        y_ref[0, sub(j), :] = r


def _gla_out(gq, gk, gv, la, sgg, gn, oat, sga, sgb, x, wpa, wpb, wo, gf, final_norm):
    B, S, D = x.shape
    rt = TAIL_STEP_ROWS
    n_sub = rt // ROW_TILE
    const = lambda shape: pl.BlockSpec(shape, lambda b, t: (0,) * len(shape))
    rows = lambda w: pl.BlockSpec((1, rt, w), lambda b, t: (b, t, 0))
    return pl.pallas_call(
        functools.partial(_gla_out_kernel, final_norm=final_norm),
        grid=(B, S // rt),
        in_specs=[
            rows(GLA_DK), rows(GLA_DK), rows(GLA_DV), rows(GLA_DK), rows(GLA_DV),
            const((1, GLA_HEAD_V)),
            pl.BlockSpec((1, n_sub, MOBA_WIDTH, ROW_TILE), lambda b, t: (b, t, 0, 0)),
            rows(D), rows(D), rows(D),
            const(wpa.shape), const(wpb.shape), const(wo.shape), const((1, D)),
        ],
        out_specs=rows(D),
        out_shape=jax.ShapeDtypeStruct((B, S, D), F32),
        scratch_shapes=[pltpu.VMEM((GLA_HEAD_V, GLA_DK), F32)],
        compiler_params=pltpu.CompilerParams(
            dimension_semantics=("arbitrary", "arbitrary"),
            vmem_limit_bytes=VMEM_LIMIT),
        name="gla_out",
    )(gq, gk, gv, la, sgg, gn, oat, sga, sgb, x, wpa, wpb, wo, gf)


def _rope_tables(S):
    inv_freq = 1.0 / (ROPE_THETA ** (jnp.arange(HALF, dtype=F32) / HALF))
    ang = jnp.arange(S, dtype=F32)[:, None] * inv_freq[None, :]
    return jnp.cos(ang), jnp.sin(ang)


def _swap_half_head(a):
    lead = a.shape[:-1]
    a = a.reshape(*lead, N_PAIRS, 2, 2, HALF)
    return jnp.swapaxes(a, -2, -3).reshape(*lead, MOBA_WIDTH)


def kernel(x, norm_in_g, w_in, b_merge, w_gla_fg2, b_gla_fg, gla_norm_g,
           w_proj_a, w_proj_b, w_out, norm_f_g):
    B, S, D = x.shape
    depth = w_in.shape[0]
    nt = S // MOBA_BLOCK
    cos, sin = _rope_tables(S)
    q_scale = MOBA_HEAD_DIM ** -0.5 * math.log2(math.e)
    cq = (jnp.concatenate([cos, cos], axis=1) * q_scale).T
    sq = (jnp.concatenate([-sin, sin], axis=1) * q_scale).T
    ck = jnp.tile(cos, (1, KX // HALF))
    sk = jnp.tile(sin, (1, KX // HALF)) * jnp.where(jnp.arange(KX) < KX_HI, -1.0, 1.0)[None, :]

    for layer in range(depth):
        w = w_in[layer]
        o = np.cumsum([0, MOBA_WIDTH, MOBA_WIDTH, MOBA_WIDTH, MOBA_WIDTH, GLA_DK, GLA_DK,
                       GLA_DV, GLA_DV, GLA_GATE_RANK, D_MODEL, D_MODEL])
        wq, wk, wv, wmg, wgq, wgk, wgv, wgg, wfg, wga, wgb = [
            w[:, o[i]:o[i + 1]] for i in range(11)]
        wn = jnp.concatenate([_swap_half_head(wk), wgq, wgk, wgv, wgg, wga, wgb],
                             axis=1).astype(BF16)
        wt = jnp.concatenate([wq, wv, wmg, wfg], axis=1).T.astype(BF16)
        wfg2 = w_gla_fg2[layer].astype(BF16)

        (qx, vt, smgt, k, gq, gk, gv, sgg, la, sga, sgb) = _in_proj(
            x, norm_in_g[layer][None, :], wn, wt, cq, sq, ck, sk, wfg2,
            b_gla_fg[layer][None, :], b_merge[layer])
        oat = _moba(qx, k, vt, smgt)
        x = _gla_out(gq, gk, gv, la, sgg, gla_norm_g[layer][None, :], oat, sga, sgb, x,
                     w_proj_a[layer].astype(BF16), w_proj_b[layer].astype(BF16),
                     w_out[layer].astype(BF16), norm_f_g[None, :],
                     final_norm=(layer == depth - 1))
    return x
```

```python
import functools
import math

import jax
import jax.numpy as jnp
import numpy as np
from jax import lax
from jax.experimental import pallas as pl
from jax.experimental.pallas import tpu as pltpu

F32 = jnp.float32
BF16 = jnp.bfloat16

D_MODEL = 1024
MOBA_HEADS = 8
MOBA_HEAD_DIM = 64
MOBA_WIDTH = MOBA_HEADS * MOBA_HEAD_DIM
MOBA_BLOCK = 256
MOBA_TOPK = 3
ROPE_THETA = 10000.0
GLA_HEADS = 4
GLA_DK = 256
GLA_DV = 512
GLA_HEAD_K = 64
GLA_HEAD_V = 128
GLA_GATE_RANK = 16
GLA_GATE_NORM = 16.0
GLA_CHUNK = 64
RMS_EPS = 1e-6
NEG_INF = -1e30

LANES = 128
BF16_ROWS = 16
ROW_TILE = MOBA_BLOCK
PAIR = 2 * MOBA_HEAD_DIM
N_PAIRS = MOBA_HEADS // 2
TAIL_STEP_ROWS = 1024
IN_STEP_ROWS = 512
QK_AHEAD = 4
QK_LEAD = 1
MOBA_Q_TILE = 1024
LOOP_BLOCKS = 4
VMEM_LIMIT = 56 * 1024 * 1024

HALF = MOBA_HEAD_DIM // 2
KX = LANES
KX_LO, KX_OH, KX_HI = 0, HALF, 2 * HALF
N_BLOCKS_MAX = BF16_ROWS
VX = MOBA_HEAD_DIM + BF16_ROWS

_C_K = 0
_C_GQ = _C_K + MOBA_WIDTH
_C_GK = _C_GQ + GLA_DK
_C_GV = _C_GK + GLA_DK
_C_GG = _C_GV + GLA_DV
_C_GA = _C_GG + GLA_DV
_C_GB = _C_GA + D_MODEL
_C_END = _C_GB + D_MODEL
_R_Q = 0
_R_V = _R_Q + MOBA_WIDTH
_R_MG = _R_V + MOBA_WIDTH
_R_FG = _R_MG + MOBA_WIDTH
_R_END = _R_FG + GLA_GATE_RANK


def _dot(a, b):
    return jnp.dot(a, b, preferred_element_type=F32)


def _dot_nt(a, b):
    return lax.dot_general(a, b, (((1,), (1,)), ((), ())), preferred_element_type=F32)


def _dot_tn(a, b):
    return lax.dot_general(a, b, (((0,), (0,)), ((), ())), preferred_element_type=F32)


def _sigmoid(x):
    return 1.0 / (1.0 + jnp.exp(-x))


def _silu(x):
    return x * _sigmoid(x)


def _select_blocks(gates, first_own, n_blocks, nb):
    H = MOBA_HEADS
    qw = gates.shape[1]
    col_own = (first_own + lax.broadcasted_iota(jnp.int32, (1, qw), 1) // nb).astype(F32)
    past = [col_own > n for n in range(n_blocks)]
    g = [jnp.where(past[n], gates[n * H:(n + 1) * H], NEG_INF) for n in range(n_blocks)]
    picked = [jnp.zeros((H, qw), jnp.bool_) for _ in range(n_blocks)]
    for _ in range(MOBA_TOPK):
        best = functools.reduce(jnp.maximum, g)
        first = functools.reduce(jnp.minimum, [
            jnp.where(g[n] == best, float(n), float(n_blocks)) for n in range(n_blocks)])
        for n in range(n_blocks):
            hit = first == float(n)
            picked[n] = picked[n] | hit
            g[n] = jnp.where(hit, -jnp.inf, g[n])
    keep = jnp.concatenate(
        [jnp.where((col_own == n) | (past[n] & picked[n]), 1.0, 0.0) for n in range(n_blocks)],
        axis=0).astype(BF16)
    r_out = lax.broadcasted_iota(jnp.int32, (H * n_blocks, n_blocks * H), 0)
    r_in = lax.broadcasted_iota(jnp.int32, (H * n_blocks, n_blocks * H), 1)
    perm = ((r_out // n_blocks == r_in % H) & (r_out % n_blocks == r_in // H)).astype(BF16)
    return jnp.where(_dot(perm, keep) > 0.5, 0.0, NEG_INF).astype(BF16)


def _in_proj_kernel(x_ref, g_ref, wn_ref, wt_ref, cq_ref, sq_ref, ck_ref, sk_ref,
                    wfg2_ref, bfg_ref, bm_ref,
                    qx_ref, vt_ref, smgt_ref, k_ref, gq_ref, gk_ref, gv_ref,
                    sgg_ref, la_ref, sga_ref, sgb_ref, km_ref, *, n_blocks):
    t = pl.program_id(1)
    rt = x_ref.shape[1]
    nb = MOBA_BLOCK
    n_sub = rt // nb
    hd = MOBA_HEAD_DIM
    H = MOBA_HEADS
    assert n_blocks <= N_BLOCKS_MAX
    blk_cols = lambda j: slice(j * nb, (j + 1) * nb)

    @pl.when(t == 0)
    def _():
        km_ref[...] = jnp.zeros_like(km_ref)

    x = x_ref[0]
    ms = jnp.mean(x * x, axis=-1, keepdims=True)
    hb = (x * lax.rsqrt(ms + RMS_EPS) * g_ref[...]).astype(BF16)

    pt = _dot_nt(wt_ref[...], hb)
    cq = cq_ref[...]
    sq = sq_ref[...]
    ones_rows = jnp.where(lax.broadcasted_iota(jnp.int32, (BF16_ROWS, nb), 0) == 0,
                          1.0, 0.0).astype(BF16)
    q_rot = []
    for h in range(H):
        blk = pt[_R_Q + h * hd:_R_Q + (h + 1) * hd]
        swapped = jnp.concatenate([blk[HALF:], blk[:HALF]], axis=0)
        q_rot.append((blk * cq + swapped * sq).astype(BF16))
        v_rows = pt[_R_V + h * hd:_R_V + (h + 1) * hd].astype(BF16)
        for j in range(n_sub):
            vt_ref[0, j, h * VX:h * VX + hd, :] = v_rows[:, blk_cols(j)]
            vt_ref[0, j, h * VX + hd:(h + 1) * VX, :] = ones_rows

    smg = _silu(pt[_R_MG:_R_FG]).astype(BF16)
    for j in range(n_sub):
        smgt_ref[0, j] = smg[:, blk_cols(j)]

    pk = _dot(hb, wn_ref[:, _C_K:_C_GQ])
    pg = _dot(hb, wn_ref[:, _C_GQ:_C_GA])
    ck = ck_ref[...]
    sk = sk_ref[...]
    lane = lax.broadcasted_iota(jnp.int32, (rt, KX), 1)
    row_block = t * n_sub + lax.broadcasted_iota(jnp.int32, (rt, KX), 0) // nb
    block_onehot = jnp.where(lane == KX_OH + row_block, 1.0, 0.0)
    own_lanes = (lane % (2 * HALF)) < HALF
    k_means = [[] for _ in range(n_sub)]
    for p in range(N_PAIRS):
        blk = pk[:, p * PAIR:(p + 1) * PAIR]
        kr = blk * ck + pltpu.roll(blk, 2 * HALF, axis=1) * sk
        for j in range(n_sub):
            k_means[j].append(jnp.mean(kr[blk_cols(j)], axis=0, keepdims=True))
        for hh, src in ((0, kr), (1, pltpu.roll(kr, KX - HALF, axis=1))):
            h = 2 * p + hh
            k_ref[0, :, h * KX:(h + 1) * KX] = (
                jnp.where(own_lanes, src, 0.0) + block_onehot).astype(BF16)
    km_lane = lax.broadcasted_iota(jnp.int32, (H, MOBA_WIDTH), 1)
    km_head = 2 * (km_lane // PAIR) + (km_lane // HALF) % 2
    head_lanes = km_head == lax.broadcasted_iota(jnp.int32, (H, MOBA_WIDTH), 0)
    for j in range(n_sub):
        mean_row = jnp.concatenate(k_means[j], axis=1)
        km_ref[pl.ds(pl.multiple_of((t * n_sub + j) * H, H), H), :] = jnp.where(
            head_lanes, mean_row, 0.0)

    q_pair = jnp.concatenate(
        [q_rot[2 * p + hh][part * HALF:(part + 1) * HALF]
         for p in range(N_PAIRS) for part in range(2) for hh in range(2)], axis=0)
    km = km_ref[...]
    km_hi = km.astype(BF16)
    km_lo = (km - km_hi.astype(F32)).astype(BF16)
    gates = _dot(km_hi, q_pair) + _dot(km_lo, q_pair)

    pm = _dot(hb, wn_ref[:, _C_GA:_C_END])

    mask_rows = _select_blocks(gates, t * n_sub, n_blocks, nb)
    pad = jnp.zeros((KX_HI - KX_OH - n_blocks, rt), BF16)
    tail = jnp.zeros((KX - KX_HI - HALF, rt), BF16)
    for h in range(H):
        qx_ref[0, h] = jnp.concatenate(
            [q_rot[h][:HALF], mask_rows[h * n_blocks:(h + 1) * n_blocks], pad,
             q_rot[h][HALF:], tail], axis=0)

    gq_ref[0] = (pg[:, _C_GQ - _C_GQ:_C_GK - _C_GQ] * (GLA_HEAD_K ** -0.5)).astype(BF16)
    gk_ref[0] = pg[:, _C_GK - _C_GQ:_C_GV - _C_GQ].astype(BF16)
    gv_ref[0] = pg[:, _C_GV - _C_GQ:_C_GG - _C_GQ].astype(BF16)
    sgg_ref[0] = _silu(pg[:, _C_GG - _C_GQ:_C_GA - _C_GQ]).astype(BF16)

    z = _dot_tn(pt[_R_FG:_R_END].astype(BF16), wfg2_ref[...]) + bfg_ref[...]
    log_sig = jnp.minimum(z, 0.0) - jnp.log1p(jnp.exp(-jnp.abs(z)))
    la_ref[0] = log_sig * (1.0 / GLA_GATE_NORM)

    sga_ref[0] = _sigmoid(pm[:, :D_MODEL] + bm_ref[0:1, :]).astype(BF16)
    sgb_ref[0] = _sigmoid(pm[:, D_MODEL:] + bm_ref[1:2, :]).astype(BF16)


def _in_proj(x, g, wn, wt, cq, sq, ck, sk, wfg2, bfg, bm):
    B, S, D = x.shape
    nt = S // ROW_TILE
    rt = IN_STEP_ROWS
    n_sub = rt // ROW_TILE
    const = lambda shape: pl.BlockSpec(shape, lambda b, t: (0,) * len(shape))
    t_blocked = lambda r: pl.BlockSpec((1, n_sub, r, ROW_TILE), lambda b, t: (b, t, 0, 0))
    rows = lambda w: pl.BlockSpec((1, rt, w), lambda b, t: (b, t, 0))
    t_shape = lambda r: jax.ShapeDtypeStruct((B, nt, r, ROW_TILE), BF16)
    rshape = lambda w, dt=BF16: jax.ShapeDtypeStruct((B, S, w), dt)
    return pl.pallas_call(
        functools.partial(_in_proj_kernel, n_blocks=nt),
        grid=(B, S // rt),
        in_specs=[
            rows(D), const((1, D)), const(wn.shape), const(wt.shape),
            pl.BlockSpec((MOBA_HEAD_DIM, rt), lambda b, t: (0, t)),
            pl.BlockSpec((MOBA_HEAD_DIM, rt), lambda b, t: (0, t)),
            pl.BlockSpec((rt, KX), lambda b, t: (t, 0)),
            pl.BlockSpec((rt, KX), lambda b, t: (t, 0)),
            const(wfg2.shape), const((1, GLA_DK)), const((2, D)),
        ],
        out_specs=[
            pl.BlockSpec((1, MOBA_HEADS, KX, rt), lambda b, t: (b, 0, 0, t)),
            t_blocked(MOBA_HEADS * VX), t_blocked(MOBA_WIDTH),
            rows(MOBA_HEADS * KX),
            rows(GLA_DK), rows(GLA_DK), rows(GLA_DV), rows(GLA_DV), rows(GLA_DK),
            rows(D), rows(D),
        ],
        out_shape=[
            jax.ShapeDtypeStruct((B, MOBA_HEADS, KX, S), BF16),
            t_shape(MOBA_HEADS * VX), t_shape(MOBA_WIDTH),
            rshape(MOBA_HEADS * KX),
            rshape(GLA_DK), rshape(GLA_DK), rshape(GLA_DV), rshape(GLA_DV),
            rshape(GLA_DK, F32), rshape(D), rshape(D),
        ],
        scratch_shapes=[pltpu.VMEM((nt * MOBA_HEADS, MOBA_WIDTH), F32)],
        compiler_params=pltpu.CompilerParams(
            dimension_semantics=("arbitrary", "arbitrary"),
            vmem_limit_bytes=VMEM_LIMIT),
        name="in_proj",
    )(x, g, wn, wt, cq, sq, ck, sk, wfg2, bfg, bm)


def _moba_kernel(qx_ref, k_ref, vt_ref, smgt_ref, o_ref, s_ref, smax_ref, m_ref, acc_ref):
    nb = MOBA_BLOCK
    hd = MOBA_HEAD_DIM
    n_sub = smgt_ref.shape[1]
    first_own = pl.program_id(1) * n_sub

    def produce_scores(n, h, col0=0):
        kb = k_ref[0, pl.ds(pl.multiple_of(n * nb, nb), nb), h * KX:(h + 1) * KX]
        s = _dot(kb, qx_ref[0, h, :, col0:])
        s_ref[h, :, col0:] = s
        smax_ref[h, :, col0:] = jnp.max(s, axis=0, keepdims=True)

    causal = (lax.broadcasted_iota(jnp.int32, (nb, nb), 0)
              <= lax.broadcasted_iota(jnp.int32, (nb, nb), 1))

    def attend(blocks, diagonal, next_block):
        assert QK_LEAD <= MOBA_HEADS - QK_AHEAD
        col0 = [0 if d is None else d * nb for d in diagonal]
        jobs = [(blocks[0], h, col0[0]) for h in range(QK_AHEAD, MOBA_HEADS)]
        jobs += [(n, h, c0) for n, c0 in zip(blocks[1:], col0[1:]) for h in range(MOBA_HEADS)]
        if next_block is not None:
            jobs += [(next_block, h, 0) for h in range(QK_AHEAD)]

        def issue():
            if jobs:
                produce_scores(*jobs.pop(0))

        for _ in range(QK_LEAD):
            issue()
        for n, diag, c0 in zip(blocks, diagonal, col0):
            for h in range(MOBA_HEADS):
                s = s_ref[h, :, c0:]
                smax = smax_ref[h, :, c0:]
                if diag is not None:
                    tri = jnp.where(causal, s[:, :nb], NEG_INF)
                    tri_max = jnp.max(tri, axis=0, keepdims=True)
                    if s.shape[1] > nb:
                        s = jnp.concatenate([tri, s[:, nb:]], axis=1)
                        smax = jnp.concatenate([tri_max, smax[:, nb:]], axis=1)
                    else:
                        s, smax = tri, tri_max
                vt = vt_ref[0, n, h * VX:(h + 1) * VX, :]
                m_old = m_ref[h, :, c0:]
                m_new = jnp.maximum(m_old, smax)
                m_ref[h, :, c0:] = m_new
                acc_ref[h, :, c0:] = (jnp.exp2(m_old - m_new) * acc_ref[h, :, c0:]
                                      + _dot(vt, jnp.exp2(s - m_new).astype(BF16)))
                issue()

    m_ref[...] = jnp.full(m_ref.shape, NEG_INF, F32)
    acc_ref[...] = jnp.zeros_like(acc_ref)
    for h in range(QK_AHEAD):
        produce_scores(0, h)

    assert n_sub % LOOP_BLOCKS == 0

    def body(i, c):
        n = i * LOOP_BLOCKS
        attend([n + d for d in range(LOOP_BLOCKS)], [None] * LOOP_BLOCKS, n + LOOP_BLOCKS)
        return c

    lax.fori_loop(0, first_own // LOOP_BLOCKS, body, 0)
    attend([first_own + j for j in range(n_sub)], list(range(n_sub)), None)
    for h in range(MOBA_HEADS):
        o = acc_ref[h, 0:hd, :] / acc_ref[h, hd:hd + 1, :]
        for j in range(n_sub):
            gate = smgt_ref[0, j, h * hd:(h + 1) * hd, :].astype(F32)
            o_ref[0, j, h * hd:(h + 1) * hd, :] = (o[:, j * nb:(j + 1) * nb] * gate).astype(BF16)


def _moba(qx, k, vt, smgt):
    B, nt, W, rt = smgt.shape
    S = k.shape[1]
    n_sub = MOBA_Q_TILE // rt
    qw = MOBA_Q_TILE
    tile = pl.BlockSpec((1, n_sub, W, rt), lambda b, t: (b, t, 0, 0))
    return pl.pallas_call(
        _moba_kernel,
        grid=(B, nt // n_sub),
        in_specs=[
            pl.BlockSpec((1, MOBA_HEADS, KX, qw), lambda b, t: (b, 0, 0, t)),
            pl.BlockSpec((1, S, MOBA_HEADS * KX), lambda b, t: (b, 0, 0)),
            pl.BlockSpec((1, nt, MOBA_HEADS * VX, rt), lambda b, t: (b, 0, 0, 0)),
            tile,
        ],
        out_specs=tile,
        out_shape=jax.ShapeDtypeStruct((B, nt, W, rt), BF16),
        scratch_shapes=[
            pltpu.VMEM((MOBA_HEADS, MOBA_BLOCK, qw), F32),
            pltpu.VMEM((MOBA_HEADS, 1, qw), F32),
            pltpu.VMEM((MOBA_HEADS, 1, qw), F32),
            pltpu.VMEM((MOBA_HEADS, VX, qw), F32),
        ],
        compiler_params=pltpu.CompilerParams(
            dimension_semantics=("arbitrary", "arbitrary"),
            vmem_limit_bytes=VMEM_LIMIT),
        name="moba",
    )(qx, k, vt, smgt)


def _gla_out_kernel(q_ref, k_ref, v_ref, la_ref, sgg_ref, gn_ref,
                    oat_ref, sga_ref, sgb_ref, x_ref, wpa_ref, wpb_ref, wo_ref, gf_ref,
                    y_ref, st_ref, *, final_norm):
    C = GLA_CHUNK
    hk, hv = GLA_HEAD_K, GLA_HEAD_V
    n_sub, rt = oat_ref.shape[1], oat_ref.shape[3]
    sub = lambda j: slice(j * rt, (j + 1) * rt)

    @pl.when(pl.program_id(1) == 0)
    def _():
        st_ref[...] = jnp.zeros_like(st_ref)

    nc = TAIL_STEP_ROWS // C
    H = GLA_HEADS
    tril_b = (lax.broadcasted_iota(jnp.int32, (C, C), 0)
              >= lax.broadcasted_iota(jnp.int32, (C, C), 1)).astype(BF16)
    tril_stack = (lax.broadcasted_iota(jnp.int32, (H * C, C), 0) % C
                  >= lax.broadcasted_iota(jnp.int32, (H * C, C), 1))
    lane_head = lax.broadcasted_iota(jnp.int32, (C, GLA_DK), 1) // hk
    zero_b = jnp.zeros((C, GLA_DK), BF16)
    gn = gn_ref[...]

    def stack_heads(x):
        return jnp.concatenate([jnp.where(lane_head == h, x, zero_b) for h in range(H)], axis=0)

    chunk = lambda c: slice(c * C, (c + 1) * C)
    g = la_ref[0]
    g_hi = g.astype(BF16)
    g_lo = (g - g_hi.astype(F32)).astype(BF16)
    b = [_dot(tril_b, g_hi[chunk(c)]) + _dot(tril_b, g_lo[chunk(c)]) for c in range(nc)]

    ya = [_dot_tn(oat_ref[0, j], wpa_ref[...]) for j in range(n_sub)]

    qe_s, qb_s, ke_b, kl_s, decay = [], [], [], [], []
    for c in range(nc):
        b_mid = b[c][C // 2 - 1:C // 2]
        b_last = b[c][C - 1:C]
        qe = q_ref[0, chunk(c), :].astype(F32) * jnp.exp(b[c] - b_mid)
        ke = k_ref[0, chunk(c), :].astype(F32) * jnp.exp(b_mid - b[c])
        qe_s.append(stack_heads(qe.astype(BF16)))
        qb_s.append(stack_heads((qe * jnp.exp(b_mid)).astype(BF16)))
        kl_s.append(stack_heads((ke * jnp.exp(b_last - b_mid)).astype(BF16)))
        ke_b.append(ke.astype(BF16))
        decay.append(jnp.exp(b_last))

    attn = [jnp.where(tril_stack, _dot_nt(qe_s[c], ke_b[c]), 0.0).astype(BF16) for c in range(nc)]
    intra = [jnp.concatenate(
        [_dot(attn[c][h * C:(h + 1) * C], v_ref[0, chunk(c), h * hv:(h + 1) * hv])
         for h in range(H)], axis=1) for c in range(nc)]

    v_s = [jnp.concatenate([v_ref[0, chunk(c), h * hv:(h + 1) * hv] for h in range(H)], axis=0)
           for c in range(nc)]
    upd = [_dot_tn(v_s[c], kl_s[c]) for c in range(nc)]

    st = st_ref[...]
    inter = []
    for c in range(nc):
        o_s = _dot_nt(qb_s[c], st.astype(BF16))
        inter.append(jnp.concatenate([o_s[h * C:(h + 1) * C] for h in range(H)], axis=1))
        st = st * decay[c] + upd[c]
    st_ref[...] = st

    ob = []
    for c in range(nc):
        o = inter[c] + intra[c]
        heads = []
        for h in range(H):
            oh = o[:, h * hv:(h + 1) * hv]
            ms = jnp.mean(oh * oh, axis=-1, keepdims=True)
            y = oh * lax.rsqrt(ms + RMS_EPS) * gn
            gate = sgg_ref[0, chunk(c), h * hv:(h + 1) * hv].astype(F32)
            heads.append((y * gate).astype(BF16))
        ob.append(jnp.concatenate(heads, axis=1))

    per_sub = rt // C
    yb = [_dot(jnp.concatenate(ob[j * per_sub:(j + 1) * per_sub], axis=0), wpb_ref[...])
          for j in range(n_sub)]
    merged = [(sga_ref[0, sub(j), :].astype(F32) * ya[j]
               + sgb_ref[0, sub(j), :].astype(F32) * yb[j]).astype(BF16) for j in range(n_sub)]
    proj = [_dot(merged[j], wo_ref[...]) for j in range(n_sub)]
    for j in range(n_sub):
        r = x_ref[0, sub(j), :] + proj[j]
        if final_norm:
            ms = jnp.mean(r * r, axis=-1, keepdims=True)
            r = r * lax.rsqrt(ms + RMS_EPS) * gf_ref[...]
        y_ref[0, sub(j), :] = r


def _gla_out(gq, gk, gv, la, sgg, gn, oat, sga, sgb, x, wpa, wpb, wo, gf, final_norm):
    B, S, D = x.shape
    rt = TAIL_STEP_ROWS
    n_sub = rt // ROW_TILE
    const = lambda shape: pl.BlockSpec(shape, lambda b, t: (0,) * len(shape))
    rows = lambda w: pl.BlockSpec((1, rt, w), lambda b, t: (b, t, 0))
    return pl.pallas_call(
        functools.partial(_gla_out_kernel, final_norm=final_norm),
        grid=(B, S // rt),
        in_specs=[
            rows(GLA_DK), rows(GLA_DK), rows(GLA_DV), rows(GLA_DK), rows(GLA_DV),
            const((1, GLA_HEAD_V)),
            pl.BlockSpec((1, n_sub, MOBA_WIDTH, ROW_TILE), lambda b, t: (b, t, 0, 0)),
            rows(D), rows(D), rows(D),
            const(wpa.shape), const(wpb.shape), const(wo.shape), const((1, D)),
        ],
        out_specs=rows(D),
        out_shape=jax.ShapeDtypeStruct((B, S, D), F32),
        scratch_shapes=[pltpu.VMEM((GLA_HEAD_V, GLA_DK), F32)],
        compiler_params=pltpu.CompilerParams(
            dimension_semantics=("arbitrary", "arbitrary"),
            vmem_limit_bytes=VMEM_LIMIT),
        name="gla_out",
    )(gq, gk, gv, la, sgg, gn, oat, sga, sgb, x, wpa, wpb, wo, gf)


def _rope_tables(S):
    inv_freq = 1.0 / (ROPE_THETA ** (jnp.arange(HALF, dtype=F32) / HALF))
    ang = jnp.arange(S, dtype=F32)[:, None] * inv_freq[None, :]
    return jnp.cos(ang), jnp.sin(ang)


def _swap_half_head(a):
    lead = a.shape[:-1]
    a = a.reshape(*lead, N_PAIRS, 2, 2, HALF)
    return jnp.swapaxes(a, -2, -3).reshape(*lead, MOBA_WIDTH)


def kernel(x, norm_in_g, w_in, b_merge, w_gla_fg2, b_gla_fg, gla_norm_g,
           w_proj_a, w_proj_b, w_out, norm_f_g):
    B, S, D = x.shape
    depth = w_in.shape[0]
    nt = S // MOBA_BLOCK
    cos, sin = _rope_tables(S)
    q_scale = MOBA_HEAD_DIM ** -0.5 * math.log2(math.e)
    cq = (jnp.concatenate([cos, cos], axis=1) * q_scale).T
    sq = (jnp.concatenate([-sin, sin], axis=1) * q_scale).T
    ck = jnp.tile(cos, (1, KX // HALF))
    sk = jnp.tile(sin, (1, KX // HALF)) * jnp.where(jnp.arange(KX) < KX_HI, -1.0, 1.0)[None, :]

    for layer in range(depth):
        w = w_in[layer]
        o = np.cumsum([0, MOBA_WIDTH, MOBA_WIDTH, MOBA_WIDTH, MOBA_WIDTH, GLA_DK, GLA_DK,
                       GLA_DV, GLA_DV, GLA_GATE_RANK, D_MODEL, D_MODEL])
        wq, wk, wv, wmg, wgq, wgk, wgv, wgg, wfg, wga, wgb = [
            w[:, o[i]:o[i + 1]] for i in range(11)]
        wn = jnp.concatenate([_swap_half_head(wk), wgq, wgk, wgv, wgg, wga, wgb],
                             axis=1).astype(BF16)
        wt = jnp.concatenate([wq, wv, wmg, wfg], axis=1).T.astype(BF16)
        wfg2 = w_gla_fg2[layer].astype(BF16)

        (qx, vt, smgt, k, gq, gk, gv, sgg, la, sga, sgb) = _in_proj(
            x, norm_in_g[layer][None, :], wn, wt, cq, sq, ck, sk, wfg2,
            b_gla_fg[layer][None, :], b_merge[layer])
        oat = _moba(qx, k, vt, smgt)
        x = _gla_out(gq, gk, gv, la, sgg, gla_norm_g[layer][None, :], oat, sga, sgb, x,
                     w_proj_a[layer].astype(BF16), w_proj_b[layer].astype(BF16),
                     w_out[layer].astype(BF16), norm_f_g[None, :],
                     final_norm=(layer == depth - 1))
    return x
```

```python
import functools
import math

import jax
import jax.numpy as jnp
import numpy as np
from jax import lax
from jax.experimental import pallas as pl
from jax.experimental.pallas import tpu as pltpu

F32 = jnp.float32
BF16 = jnp.bfloat16

D_MODEL = 1024
MOBA_HEADS = 8
MOBA_HEAD_DIM = 64
MOBA_WIDTH = MOBA_HEADS * MOBA_HEAD_DIM
MOBA_BLOCK = 256
MOBA_TOPK = 3
ROPE_THETA = 10000.0
GLA_HEADS = 4
GLA_DK = 256
GLA_DV = 512
GLA_HEAD_K = 64
GLA_HEAD_V = 128
GLA_GATE_RANK = 16
GLA_GATE_NORM = 16.0
GLA_CHUNK = 64
RMS_EPS = 1e-6
NEG_INF = -1e30

LANES = 128
BF16_ROWS = 16
ROW_TILE = MOBA_BLOCK
PAIR = 2 * MOBA_HEAD_DIM
N_PAIRS = MOBA_HEADS // 2
TAIL_STEP_ROWS = 1024
IN_STEP_ROWS = 512
QK_AHEAD = 4
QK_LEAD = 1
MOBA_Q_TILE = 512
LOOP_BLOCKS = 2
VMEM_LIMIT = 56 * 1024 * 1024

HALF = MOBA_HEAD_DIM // 2
KX = LANES
KX_LO, KX_OH, KX_HI = 0, HALF, 2 * HALF
N_BLOCKS_MAX = BF16_ROWS
VX = MOBA_HEAD_DIM + BF16_ROWS

_C_K = 0
_C_GQ = _C_K + MOBA_WIDTH
_C_GK = _C_GQ + GLA_DK
_C_GV = _C_GK + GLA_DK
_C_GG = _C_GV + GLA_DV
_C_GA = _C_GG + GLA_DV
_C_GB = _C_GA + D_MODEL
_C_END = _C_GB + D_MODEL
_R_Q = 0
_R_V = _R_Q + MOBA_WIDTH
_R_MG = _R_V + MOBA_WIDTH
_R_FG = _R_MG + MOBA_WIDTH
_R_END = _R_FG + GLA_GATE_RANK


def _dot(a, b):
    return jnp.dot(a, b, preferred_element_type=F32)


def _dot_nt(a, b):
    return lax.dot_general(a, b, (((1,), (1,)), ((), ())), preferred_element_type=F32)


def _dot_tn(a, b):
    return lax.dot_general(a, b, (((0,), (0,)), ((), ())), preferred_element_type=F32)


def _sigmoid(x):
    return 1.0 / (1.0 + jnp.exp(-x))


def _silu(x):
    return x * _sigmoid(x)


def _select_blocks(gates, first_own, n_blocks, nb):
    H = MOBA_HEADS
    qw = gates.shape[1]
    col_own = (first_own + lax.broadcasted_iota(jnp.int32, (1, qw), 1) // nb).astype(F32)
    past = [col_own > n for n in range(n_blocks)]
    g = [jnp.where(past[n], gates[n * H:(n + 1) * H], NEG_INF) for n in range(n_blocks)]
    picked = [jnp.zeros((H, qw), jnp.bool_) for _ in range(n_blocks)]
    for _ in range(MOBA_TOPK):
        best = functools.reduce(jnp.maximum, g)
        first = functools.reduce(jnp.minimum, [
            jnp.where(g[n] == best, float(n), float(n_blocks)) for n in range(n_blocks)])
        for n in range(n_blocks):
            hit = first == float(n)
            picked[n] = picked[n] | hit
            g[n] = jnp.where(hit, -jnp.inf, g[n])
    keep = jnp.concatenate(
        [jnp.where((col_own == n) | (past[n] & picked[n]), 1.0, 0.0) for n in range(n_blocks)],
        axis=0).astype(BF16)
    r_out = lax.broadcasted_iota(jnp.int32, (H * n_blocks, n_blocks * H), 0)
    r_in = lax.broadcasted_iota(jnp.int32, (H * n_blocks, n_blocks * H), 1)
    perm = ((r_out // n_blocks == r_in % H) & (r_out % n_blocks == r_in // H)).astype(BF16)
    return jnp.where(_dot(perm, keep) > 0.5, 0.0, NEG_INF).astype(BF16)


def _in_proj_kernel(x_ref, g_ref, wn_ref, wt_ref, cq_ref, sq_ref, ck_ref, sk_ref,
                    wfg2_ref, bfg_ref, bm_ref,
                    qx_ref, vt_ref, smgt_ref, k_ref, gq_ref, gk_ref, gv_ref,
                    sgg_ref, la_ref, sga_ref, sgb_ref, km_ref, *, n_blocks):
    t = pl.program_id(1)
    rt = x_ref.shape[1]
    nb = MOBA_BLOCK
    n_sub = rt // nb
    hd = MOBA_HEAD_DIM
    H = MOBA_HEADS
    assert n_blocks <= N_BLOCKS_MAX
    blk_cols = lambda j: slice(j * nb, (j + 1) * nb)

    @pl.when(t == 0)
    def _():
        km_ref[...] = jnp.zeros_like(km_ref)

    x = x_ref[0]
    ms = jnp.mean(x * x, axis=-1, keepdims=True)
    hb = (x * lax.rsqrt(ms + RMS_EPS) * g_ref[...]).astype(BF16)

    pt = _dot_nt(wt_ref[...], hb)
    cq = cq_ref[...]
    sq = sq_ref[...]
    ones_rows = jnp.where(lax.broadcasted_iota(jnp.int32, (BF16_ROWS, nb), 0) == 0,
                          1.0, 0.0).astype(BF16)
    q_rot = []
    for h in range(H):
        blk = pt[_R_Q + h * hd:_R_Q + (h + 1) * hd]
        swapped = jnp.concatenate([blk[HALF:], blk[:HALF]], axis=0)
        q_rot.append((blk * cq + swapped * sq).astype(BF16))
        v_rows = pt[_R_V + h * hd:_R_V + (h + 1) * hd].astype(BF16)
        for j in range(n_sub):
            vt_ref[0, j, h * VX:h * VX + hd, :] = v_rows[:, blk_cols(j)]
            vt_ref[0, j, h * VX + hd:(h + 1) * VX, :] = ones_rows

    smg = _silu(pt[_R_MG:_R_FG]).astype(BF16)
    for j in range(n_sub):
        smgt_ref[0, j] = smg[:, blk_cols(j)]

    pk = _dot(hb, wn_ref[:, _C_K:_C_GQ])
    pg = _dot(hb, wn_ref[:, _C_GQ:_C_GA])
    ck = ck_ref[...]
    sk = sk_ref[...]
    lane = lax.broadcasted_iota(jnp.int32, (rt, KX), 1)
    row_block = t * n_sub + lax.broadcasted_iota(jnp.int32, (rt, KX), 0) // nb
    block_onehot = jnp.where(lane == KX_OH + row_block, 1.0, 0.0)
    own_lanes = (lane % (2 * HALF)) < HALF
    k_means = [[] for _ in range(n_sub)]
    for p in range(N_PAIRS):
        blk = pk[:, p * PAIR:(p + 1) * PAIR]
        kr = blk * ck + pltpu.roll(blk, 2 * HALF, axis=1) * sk
        for j in range(n_sub):
            k_means[j].append(jnp.mean(kr[blk_cols(j)], axis=0, keepdims=True))
        for hh, src in ((0, kr), (1, pltpu.roll(kr, KX - HALF, axis=1))):
            h = 2 * p + hh
            k_ref[0, :, h * KX:(h + 1) * KX] = (
                jnp.where(own_lanes, src, 0.0) + block_onehot).astype(BF16)
    km_lane = lax.broadcasted_iota(jnp.int32, (H, MOBA_WIDTH), 1)
    km_head = 2 * (km_lane // PAIR) + (km_lane // HALF) % 2
    head_lanes = km_head == lax.broadcasted_iota(jnp.int32, (H, MOBA_WIDTH), 0)
    for j in range(n_sub):
        mean_row = jnp.concatenate(k_means[j], axis=1)
        km_ref[pl.ds(pl.multiple_of((t * n_sub + j) * H, H), H), :] = jnp.where(
            head_lanes, mean_row, 0.0)

    q_pair = jnp.concatenate(
        [q_rot[2 * p + hh][part * HALF:(part + 1) * HALF]
         for p in range(N_PAIRS) for part in range(2) for hh in range(2)], axis=0)
    km = km_ref[...]
    km_hi = km.astype(BF16)
    km_lo = (km - km_hi.astype(F32)).astype(BF16)
    gates = _dot(km_hi, q_pair) + _dot(km_lo, q_pair)

    pm = _dot(hb, wn_ref[:, _C_GA:_C_END])

    mask_rows = _select_blocks(gates, t * n_sub, n_blocks, nb)
    pad = jnp.zeros((KX_HI - KX_OH - n_blocks, rt), BF16)
    tail = jnp.zeros((KX - KX_HI - HALF, rt), BF16)
    for h in range(H):
        qx_ref[0, h] = jnp.concatenate(
            [q_rot[h][:HALF], mask_rows[h * n_blocks:(h + 1) * n_blocks], pad,
             q_rot[h][HALF:], tail], axis=0)

    gq_ref[0] = (pg[:, _C_GQ - _C_GQ:_C_GK - _C_GQ] * (GLA_HEAD_K ** -0.5)).astype(BF16)
    gk_ref[0] = pg[:, _C_GK - _C_GQ:_C_GV - _C_GQ].astype(BF16)
    gv_ref[0] = pg[:, _C_GV - _C_GQ:_C_GG - _C_GQ].astype(BF16)
    sgg_ref[0] = _silu(pg[:, _C_GG - _C_GQ:_C_GA - _C_GQ]).astype(BF16)

    z = _dot_tn(pt[_R_FG:_R_END].astype(BF16), wfg2_ref[...]) + bfg_ref[...]
    log_sig = jnp.minimum(z, 0.0) - jnp.log1p(jnp.exp(-jnp.abs(z)))
    la_ref[0] = log_sig * (1.0 / GLA_GATE_NORM)

    sga_ref[0] = _sigmoid(pm[:, :D_MODEL] + bm_ref[0:1, :]).astype(BF16)
    sgb_ref[0] = _sigmoid(pm[:, D_MODEL:] + bm_ref[1:2, :]).astype(BF16)


def _in_proj(x, g, wn, wt, cq, sq, ck, sk, wfg2, bfg, bm):
    B, S, D = x.shape
    nt = S // ROW_TILE
    rt = IN_STEP_ROWS
    n_sub = rt // ROW_TILE
    const = lambda shape: pl.BlockSpec(shape, lambda b, t: (0,) * len(shape))
    t_blocked = lambda r: pl.BlockSpec((1, n_sub, r, ROW_TILE), lambda b, t: (b, t, 0, 0))
    rows = lambda w: pl.BlockSpec((1, rt, w), lambda b, t: (b, t, 0))
    t_shape = lambda r: jax.ShapeDtypeStruct((B, nt, r, ROW_TILE), BF16)
    rshape = lambda w, dt=BF16: jax.ShapeDtypeStruct((B, S, w), dt)
    return pl.pallas_call(
        functools.partial(_in_proj_kernel, n_blocks=nt),
        grid=(B, S // rt),
        in_specs=[
            rows(D), const((1, D)), const(wn.shape), const(wt.shape),
            pl.BlockSpec((MOBA_HEAD_DIM, rt), lambda b, t: (0, t)),
            pl.BlockSpec((MOBA_HEAD_DIM, rt), lambda b, t: (0, t)),
            pl.BlockSpec((rt, KX), lambda b, t: (t, 0)),
            pl.BlockSpec((rt, KX), lambda b, t: (t, 0)),
            const(wfg2.shape), const((1, GLA_DK)), const((2, D)),
        ],
        out_specs=[
            pl.BlockSpec((1, MOBA_HEADS, KX, rt), lambda b, t: (b, 0, 0, t)),
            t_blocked(MOBA_HEADS * VX), t_blocked(MOBA_WIDTH),
            rows(MOBA_HEADS * KX),
            rows(GLA_DK), rows(GLA_DK), rows(GLA_DV), rows(GLA_DV), rows(GLA_DK),
            rows(D), rows(D),
        ],
        out_shape=[
            jax.ShapeDtypeStruct((B, MOBA_HEADS, KX, S), BF16),
            t_shape(MOBA_HEADS * VX), t_shape(MOBA_WIDTH),
            rshape(MOBA_HEADS * KX),
            rshape(GLA_DK), rshape(GLA_DK), rshape(GLA_DV), rshape(GLA_DV),
            rshape(GLA_DK, F32), rshape(D), rshape(D),
        ],
        scratch_shapes=[pltpu.VMEM((nt * MOBA_HEADS, MOBA_WIDTH), F32)],
        compiler_params=pltpu.CompilerParams(
            dimension_semantics=("arbitrary", "arbitrary"),
            vmem_limit_bytes=VMEM_LIMIT),
        name="in_proj",
    )(x, g, wn, wt, cq, sq, ck, sk, wfg2, bfg, bm)


def _moba_kernel(qx_ref, k_ref, vt_ref, smgt_ref, o_ref, s_ref, smax_ref, m_ref, acc_ref):
    nb = MOBA_BLOCK
    hd = MOBA_HEAD_DIM
    n_sub = smgt_ref.shape[1]
    first_own = pl.program_id(1) * n_sub

    def produce_scores(n, h, col0=0):
        kb = k_ref[0, pl.ds(pl.multiple_of(n * nb, nb), nb), h * KX:(h + 1) * KX]
        s = _dot(kb, qx_ref[0, h, :, col0:])
        s_ref[h, :, col0:] = s
        smax_ref[h, :, col0:] = jnp.max(s, axis=0, keepdims=True)

    causal = (lax.broadcasted_iota(jnp.int32, (nb, nb), 0)
              <= lax.broadcasted_iota(jnp.int32, (nb, nb), 1))

    def attend(blocks, diagonal, next_block):
        assert QK_LEAD <= MOBA_HEADS - QK_AHEAD
        col0 = [0 if d is None else d * nb for d in diagonal]
        jobs = [(blocks[0], h, col0[0]) for h in range(QK_AHEAD, MOBA_HEADS)]
        jobs += [(n, h, c0) for n, c0 in zip(blocks[1:], col0[1:]) for h in range(MOBA_HEADS)]
        if next_block is not None:
            jobs += [(next_block, h, 0) for h in range(QK_AHEAD)]

        def issue():
            if jobs:
                produce_scores(*jobs.pop(0))

        for _ in range(QK_LEAD):
            issue()
        for n, diag, c0 in zip(blocks, diagonal, col0):
            for h in range(MOBA_HEADS):
                s = s_ref[h, :, c0:]
                smax = smax_ref[h, :, c0:]
                if diag is not None:
                    tri = jnp.where(causal, s[:, :nb], NEG_INF)
                    tri_max = jnp.max(tri, axis=0, keepdims=True)
                    if s.shape[1] > nb:
                        s = jnp.concatenate([tri, s[:, nb:]], axis=1)
                        smax = jnp.concatenate([tri_max, smax[:, nb:]], axis=1)
                    else:
                        s, smax = tri, tri_max
                vt = vt_ref[0, n, h * VX:(h + 1) * VX, :]
                m_old = m_ref[h, :, c0:]
                m_new = jnp.maximum(m_old, smax)
                m_ref[h, :, c0:] = m_new
                acc_ref[h, :, c0:] = (jnp.exp2(m_old - m_new) * acc_ref[h, :, c0:]
                                      + _dot(vt, jnp.exp2(s - m_new).astype(BF16)))
                issue()

    m_ref[...] = jnp.full(m_ref.shape, NEG_INF, F32)
    acc_ref[...] = jnp.zeros_like(acc_ref)
    for h in range(QK_AHEAD):
        produce_scores(0, h)

    assert n_sub % LOOP_BLOCKS == 0

    def body(i, c):
        n = i * LOOP_BLOCKS
        attend([n + d for d in range(LOOP_BLOCKS)], [None] * LOOP_BLOCKS, n + LOOP_BLOCKS)
        return c

    lax.fori_loop(0, first_own // LOOP_BLOCKS, body, 0)
    attend([first_own + j for j in range(n_sub)], list(range(n_sub)), None)
    for h in range(MOBA_HEADS):
        o = acc_ref[h, 0:hd, :] / acc_ref[h, hd:hd + 1, :]
        for j in range(n_sub):
            gate = smgt_ref[0, j, h * hd:(h + 1) * hd, :].astype(F32)
            o_ref[0, j, h * hd:(h + 1) * hd, :] = (o[:, j * nb:(j + 1) * nb] * gate).astype(BF16)


def _moba(qx, k, vt, smgt):
    B, nt, W, rt = smgt.shape
    S = k.shape[1]
    n_sub = MOBA_Q_TILE // rt
    qw = MOBA_Q_TILE
    tile = pl.BlockSpec((1, n_sub, W, rt), lambda b, t: (b, t, 0, 0))
    return pl.pallas_call(
        _moba_kernel,
        grid=(B, nt // n_sub),
        in_specs=[
            pl.BlockSpec((1, MOBA_HEADS, KX, qw), lambda b, t: (b, 0, 0, t)),
            pl.BlockSpec((1, S, MOBA_HEADS * KX), lambda b, t: (b, 0, 0)),
            pl.BlockSpec((1, nt, MOBA_HEADS * VX, rt), lambda b, t: (b, 0, 0, 0)),
            tile,
        ],
        out_specs=tile,
        out_shape=jax.ShapeDtypeStruct((B, nt, W, rt), BF16),
        scratch_shapes=[
            pltpu.VMEM((MOBA_HEADS, MOBA_BLOCK, qw), F32),
            pltpu.VMEM((MOBA_HEADS, 1, qw), F32),
            pltpu.VMEM((MOBA_HEADS, 1, qw), F32),
            pltpu.VMEM((MOBA_HEADS, VX, qw), F32),
        ],
        compiler_params=pltpu.CompilerParams(
            dimension_semantics=("arbitrary", "arbitrary"),
            vmem_limit_bytes=VMEM_LIMIT),
        name="moba",
    )(qx, k, vt, smgt)


def _gla_out_kernel(q_ref, k_ref, v_ref, la_ref, sgg_ref, gn_ref,
                    oat_ref, sga_ref, sgb_ref, x_ref, wpa_ref, wpb_ref, wo_ref, gf_ref,
                    y_ref, st_ref, *, final_norm):
    C = GLA_CHUNK
    hk, hv = GLA_HEAD_K, GLA_HEAD_V
    n_sub, rt = oat_ref.shape[1], oat_ref.shape[3]
    sub = lambda j: slice(j * rt, (j + 1) * rt)

    @pl.when(pl.program_id(1) == 0)
    def _():
        st_ref[...] = jnp.zeros_like(st_ref)

    nc = TAIL_STEP_ROWS // C
    H = GLA_HEADS
    tril_b = (lax.broadcasted_iota(jnp.int32, (C, C), 0)
              >= lax.broadcasted_iota(jnp.int32, (C, C), 1)).astype(BF16)
    tril_stack = (lax.broadcasted_iota(jnp.int32, (H * C, C), 0) % C
                  >= lax.broadcasted_iota(jnp.int32, (H * C, C), 1))
    lane_head = lax.broadcasted_iota(jnp.int32, (C, GLA_DK), 1) // hk
    zero_b = jnp.zeros((C, GLA_DK), BF16)
    gn = gn_ref[...]

    def stack_heads(x):
        return jnp.concatenate([jnp.where(lane_head == h, x, zero_b) for h in range(H)], axis=0)

    chunk = lambda c: slice(c * C, (c + 1) * C)
    g = la_ref[0]
    g_hi = g.astype(BF16)
    g_lo = (g - g_hi.astype(F32)).astype(BF16)
    b = [_dot(tril_b, g_hi[chunk(c)]) + _dot(tril_b, g_lo[chunk(c)]) for c in range(nc)]

    ya = [_dot_tn(oat_ref[0, j], wpa_ref[...]) for j in range(n_sub)]

    qe_s, qb_s, ke_b, kl_s, decay = [], [], [], [], []
    for c in range(nc):
        b_mid = b[c][C // 2 - 1:C // 2]
        b_last = b[c][C - 1:C]
        qe = q_ref[0, chunk(c), :].astype(F32) * jnp.exp(b[c] - b_mid)
        ke = k_ref[0, chunk(c), :].astype(F32) * jnp.exp(b_mid - b[c])
        qe_s.append(stack_heads(qe.astype(BF16)))
        qb_s.append(stack_heads((qe * jnp.exp(b_mid)).astype(BF16)))
        kl_s.append(stack_heads((ke * jnp.exp(b_last - b_mid)).astype(BF16)))
        ke_b.append(ke.astype(BF16))
        decay.append(jnp.exp(b_last))

    attn = [jnp.where(tril_stack, _dot_nt(qe_s[c], ke_b[c]), 0.0).astype(BF16) for c in range(nc)]
    intra = [jnp.concatenate(
        [_dot(attn[c][h * C:(h + 1) * C], v_ref[0, chunk(c), h * hv:(h + 1) * hv])
         for h in range(H)], axis=1) for c in range(nc)]

    v_s = [jnp.concatenate([v_ref[0, chunk(c), h * hv:(h + 1) * hv] for h in range(H)], axis=0)
           for c in range(nc)]
    upd = [_dot_tn(v_s[c], kl_s[c]) for c in range(nc)]

    st = st_ref[...]
    inter = []
    for c in range(nc):
        o_s = _dot_nt(qb_s[c], st.astype(BF16))
        inter.append(jnp.concatenate([o_s[h * C:(h + 1) * C] for h in range(H)], axis=1))
        st = st * decay[c] + upd[c]
    st_ref[...] = st

    ob = []
    for c in range(nc):
        o = inter[c] + intra[c]
        heads = []
        for h in range(H):
            oh = o[:, h * hv:(h + 1) * hv]
            ms = jnp.mean(oh * oh, axis=-1, keepdims=True)
            y = oh * lax.rsqrt(ms + RMS_EPS) * gn
            gate = sgg_ref[0, chunk(c), h * hv:(h + 1) * hv].astype(F32)
            heads.append((y * gate).astype(BF16))
        ob.append(jnp.concatenate(heads, axis=1))

    per_sub = rt // C
    yb = [_dot(jnp.concatenate(ob[j * per_sub:(j + 1) * per_sub], axis=0), wpb_ref[...])
          for j in range(n_sub)]
    merged = [(sga_ref[0, sub(j), :].astype(F32) * ya[j]
               + sgb_ref[0, sub(j), :].astype(F32) * yb[j]).astype(BF16) for j in range(n_sub)]
    proj = [_dot(merged[j], wo_ref[...]) for j in range(n_sub)]
    for j in range(n_sub):
        r = x_ref[0, sub(j), :] + proj[j]
        if final_norm:
            ms = jnp.mean(r * r, axis=-1, keepdims=True)
            r = r * lax.rsqrt(ms + RMS_EPS) * gf_ref[...]
        y_ref[0, sub(j), :] = r


def _gla_out(gq, gk, gv, la, sgg, gn, oat, sga, sgb, x, wpa, wpb, wo, gf, final_norm):
    B, S, D = x.shape
    rt = TAIL_STEP_ROWS
    n_sub = rt // ROW_TILE
    const = lambda shape: pl.BlockSpec(shape, lambda b, t: (0,) * len(shape))
    rows = lambda w: pl.BlockSpec((1, rt, w), lambda b, t: (b, t, 0))
    return pl.pallas_call(
        functools.partial(_gla_out_kernel, final_norm=final_norm),
        grid=(B, S // rt),
        in_specs=[
            rows(GLA_DK), rows(GLA_DK), rows(GLA_DV), rows(GLA_DK), rows(GLA_DV),
            const((1, GLA_HEAD_V)),
            pl.BlockSpec((1, n_sub, MOBA_WIDTH, ROW_TILE), lambda b, t: (b, t, 0, 0)),
            rows(D), rows(D), rows(D),
            const(wpa.shape), const(wpb.shape), const(wo.shape), const((1, D)),
        ],
        out_specs=rows(D),
        out_shape=jax.ShapeDtypeStruct((B, S, D), F32),
        scratch_shapes=[pltpu.VMEM((GLA_HEAD_V, GLA_DK), F32)],
        compiler_params=pltpu.CompilerParams(
            dimension_semantics=("arbitrary", "arbitrary"),
            vmem_limit_bytes=VMEM_LIMIT),
        name="gla_out",
    )(gq, gk, gv, la, sgg, gn, oat, sga, sgb, x, wpa, wpb, wo, gf)


def _rope_tables(S):
    inv_freq = 1.0 / (ROPE_THETA ** (jnp.arange(HALF, dtype=F32) / HALF))
    ang = jnp.arange(S, dtype=F32)[:, None] * inv_freq[None, :]
    return jnp.cos(ang), jnp.sin(ang)


def _swap_half_head(a):
    lead = a.shape[:-1]
    a = a.reshape(*lead, N_PAIRS, 2, 2, HALF)
    return jnp.swapaxes(a, -2, -3).reshape(*lead, MOBA_WIDTH)


def kernel(x, norm_in_g, w_in, b_merge, w_gla_fg2, b_gla_fg, gla_norm_g,
           w_proj_a, w_proj_b, w_out, norm_f_g):
    B, S, D = x.shape
    depth = w_in.shape[0]
    nt = S // MOBA_BLOCK
    cos, sin = _rope_tables(S)
    q_scale = MOBA_HEAD_DIM ** -0.5 * math.log2(math.e)
    cq = (jnp.concatenate([cos, cos], axis=1) * q_scale).T
    sq = (jnp.concatenate([-sin, sin], axis=1) * q_scale).T
    ck = jnp.tile(cos, (1, KX // HALF))
    sk = jnp.tile(sin, (1, KX // HALF)) * jnp.where(jnp.arange(KX) < KX_HI, -1.0, 1.0)[None, :]

    for layer in range(depth):
        w = w_in[layer]
        o = np.cumsum([0, MOBA_WIDTH, MOBA_WIDTH, MOBA_WIDTH, MOBA_WIDTH, GLA_DK, GLA_DK,
                       GLA_DV, GLA_DV, GLA_GATE_RANK, D_MODEL, D_MODEL])
        wq, wk, wv, wmg, wgq, wgk, wgv, wgg, wfg, wga, wgb = [
            w[:, o[i]:o[i + 1]] for i in range(11)]
        wn = jnp.concatenate([_swap_half_head(wk), wgq, wgk, wgv, wgg, wga, wgb],
                             axis=1).astype(BF16)
        wt = jnp.concatenate([wq, wv, wmg, wfg], axis=1).astype(BF16).T
        wfg2 = w_gla_fg2[layer].astype(BF16)

        (qx, vt, smgt, k, gq, gk, gv, sgg, la, sga, sgb) = _in_proj(
            x, norm_in_g[layer][None, :], wn, wt, cq, sq, ck, sk, wfg2,
            b_gla_fg[layer][None, :], b_merge[layer])
        oat = _moba(qx, k, vt, smgt)
        x = _gla_out(gq, gk, gv, la, sgg, gla_norm_g[layer][None, :], oat, sga, sgb, x,
                     w_proj_a[layer].astype(BF16), w_proj_b[layer].astype(BF16),
                     w_out[layer].astype(BF16), norm_f_g[None, :],
                     final_norm=(layer == depth - 1))
    return x
```

```python
import functools
import math

import jax
import jax.numpy as jnp
import numpy as np
from jax import lax
from jax.experimental import pallas as pl
from jax.experimental.pallas import tpu as pltpu

F32 = jnp.float32
BF16 = jnp.bfloat16

D_MODEL = 1024
MOBA_HEADS = 8
MOBA_HEAD_DIM = 64
MOBA_WIDTH = MOBA_HEADS * MOBA_HEAD_DIM
MOBA_BLOCK = 256
MOBA_TOPK = 3
ROPE_THETA = 10000.0
GLA_HEADS = 4
GLA_DK = 256
GLA_DV = 512
GLA_HEAD_K = 64
GLA_HEAD_V = 128
GLA_GATE_RANK = 16
GLA_GATE_NORM = 16.0
GLA_CHUNK = 64
RMS_EPS = 1e-6
NEG_INF = -1e30

LANES = 128
BF16_ROWS = 16
ROW_TILE = MOBA_BLOCK
PAIR = 2 * MOBA_HEAD_DIM
N_PAIRS = MOBA_HEADS // 2
TAIL_STEP_ROWS = 1024
IN_STEP_ROWS = 512
QK_AHEAD = 4
QK_LEAD = 1
MOBA_Q_TILE = 512
LOOP_BLOCKS = 2
VMEM_LIMIT = 56 * 1024 * 1024

HALF = MOBA_HEAD_DIM // 2
KX = LANES
KX_LO, KX_OH, KX_HI = 0, HALF, 2 * HALF
N_BLOCKS_MAX = BF16_ROWS
VX = MOBA_HEAD_DIM + BF16_ROWS

_C_K = 0
_C_GQ = _C_K + MOBA_WIDTH
_C_GK = _C_GQ + GLA_DK
_C_GV = _C_GK + GLA_DK
_C_GG = _C_GV + GLA_DV
_C_GA = _C_GG + GLA_DV
_C_GB = _C_GA + D_MODEL
_C_END = _C_GB + D_MODEL
_R_Q = 0
_R_V = _R_Q + MOBA_WIDTH
_R_MG = _R_V + MOBA_WIDTH
_R_FG = _R_MG + MOBA_WIDTH
_R_END = _R_FG + GLA_GATE_RANK


def _dot(a, b):
    return jnp.dot(a, b, preferred_element_type=F32)


def _dot_nt(a, b):
    return lax.dot_general(a, b, (((1,), (1,)), ((), ())), preferred_element_type=F32)


def _dot_tn(a, b):
    return lax.dot_general(a, b, (((0,), (0,)), ((), ())), preferred_element_type=F32)


def _sigmoid(x):
    return 1.0 / (1.0 + jnp.exp(-x))


def _silu(x):
    return x * _sigmoid(x)


def _select_blocks(gates, first_own, n_blocks, nb):
    H = MOBA_HEADS
    qw = gates.shape[1]
    col_own = (first_own + lax.broadcasted_iota(jnp.int32, (1, qw), 1) // nb).astype(F32)
    past = [col_own > n for n in range(n_blocks)]
    g = [jnp.where(past[n], gates[n * H:(n + 1) * H], NEG_INF) for n in range(n_blocks)]
    picked = [jnp.zeros((H, qw), jnp.bool_) for _ in range(n_blocks)]
    for _ in range(MOBA_TOPK):
        best = functools.reduce(jnp.maximum, g)
        first = functools.reduce(jnp.minimum, [
            jnp.where(g[n] == best, float(n), float(n_blocks)) for n in range(n_blocks)])
        for n in range(n_blocks):
            hit = first == float(n)
            picked[n] = picked[n] | hit
            g[n] = jnp.where(hit, -jnp.inf, g[n])
    keep = jnp.concatenate(
        [jnp.where((col_own == n) | (past[n] & picked[n]), 1.0, 0.0) for n in range(n_blocks)],
        axis=0).astype(BF16)
    r_out = lax.broadcasted_iota(jnp.int32, (H * n_blocks, n_blocks * H), 0)
    r_in = lax.broadcasted_iota(jnp.int32, (H * n_blocks, n_blocks * H), 1)
    perm = ((r_out // n_blocks == r_in % H) & (r_out % n_blocks == r_in // H)).astype(BF16)
    return jnp.where(_dot(perm, keep) > 0.5, 0.0, NEG_INF).astype(BF16)


def _in_proj_kernel(x_ref, g_ref, wn_ref, wt_ref, cq_ref, sq_ref, ck_ref, sk_ref,
                    wfg2_ref, bfg_ref, bm_ref,
                    qx_ref, vt_ref, smgt_ref, k_ref, gq_ref, gk_ref, gv_ref,
                    sgg_ref, la_ref, sga_ref, sgb_ref, km_ref, *, n_blocks):
    t = pl.program_id(1)
    rt = x_ref.shape[1]
    nb = MOBA_BLOCK
    n_sub = rt // nb
    hd = MOBA_HEAD_DIM
    H = MOBA_HEADS
    assert n_blocks <= N_BLOCKS_MAX
    blk_cols = lambda j: slice(j * nb, (j + 1) * nb)

    @pl.when(t == 0)
    def _():
        km_ref[...] = jnp.zeros_like(km_ref)

    x = x_ref[0]
    ms = jnp.mean(x * x, axis=-1, keepdims=True)
    hb = (x * lax.rsqrt(ms + RMS_EPS) * g_ref[...]).astype(BF16)

    pt = _dot_nt(wt_ref[...], hb)
    cq = cq_ref[...]
    sq = sq_ref[...]
    ones_rows = jnp.where(lax.broadcasted_iota(jnp.int32, (BF16_ROWS, nb), 0) == 0,
                          1.0, 0.0).astype(BF16)
    q_rot = []
    for h in range(H):
        blk = pt[_R_Q + h * hd:_R_Q + (h + 1) * hd]
        swapped = jnp.concatenate([blk[HALF:], blk[:HALF]], axis=0)
        q_rot.append((blk * cq + swapped * sq).astype(BF16))
        v_rows = pt[_R_V + h * hd:_R_V + (h + 1) * hd].astype(BF16)
        for j in range(n_sub):
            vt_ref[0, j, h * VX:h * VX + hd, :] = v_rows[:, blk_cols(j)]
            vt_ref[0, j, h * VX + hd:(h + 1) * VX, :] = ones_rows

    smg = _silu(pt[_R_MG:_R_FG]).astype(BF16)
    for j in range(n_sub):
        smgt_ref[0, j] = smg[:, blk_cols(j)]

    pk = _dot(hb, wn_ref[:, _C_K:_C_GQ])
    pg = _dot(hb, wn_ref[:, _C_GQ:_C_GA])
    ck = ck_ref[...]
    sk = sk_ref[...]
    lane = lax.broadcasted_iota(jnp.int32, (rt, KX), 1)
    row_block = t * n_sub + lax.broadcasted_iota(jnp.int32, (rt, KX), 0) // nb
    block_onehot = jnp.where(lane == KX_OH + row_block, 1.0, 0.0)
    own_lanes = (lane % (2 * HALF)) < HALF
    k_means = [[] for _ in range(n_sub)]
    for p in range(N_PAIRS):
        blk = pk[:, p * PAIR:(p + 1) * PAIR]
        kr = blk * ck + pltpu.roll(blk, 2 * HALF, axis=1) * sk
        for j in range(n_sub):
            k_means[j].append(jnp.mean(kr[blk_cols(j)], axis=0, keepdims=True))
        for hh, src in ((0, kr), (1, pltpu.roll(kr, KX - HALF, axis=1))):
            h = 2 * p + hh
            k_ref[0, :, h * KX:(h + 1) * KX] = (
                jnp.where(own_lanes, src, 0.0) + block_onehot).astype(BF16)
    km_lane = lax.broadcasted_iota(jnp.int32, (H, MOBA_WIDTH), 1)
    km_head = 2 * (km_lane // PAIR) + (km_lane // HALF) % 2
    head_lanes = km_head == lax.broadcasted_iota(jnp.int32, (H, MOBA_WIDTH), 0)
    for j in range(n_sub):
        mean_row = jnp.concatenate(k_means[j], axis=1)
        km_ref[pl.ds(pl.multiple_of((t * n_sub + j) * H, H), H), :] = jnp.where(
            head_lanes, mean_row, 0.0)

    q_pair = jnp.concatenate(
        [q_rot[2 * p + hh][part * HALF:(part + 1) * HALF]
         for p in range(N_PAIRS) for part in range(2) for hh in range(2)], axis=0)
    km = km_ref[...]
    km_hi = km.astype(BF16)
    km_lo = (km - km_hi.astype(F32)).astype(BF16)
    gates = _dot(km_hi, q_pair) + _dot(km_lo, q_pair)

    pm = _dot(hb, wn_ref[:, _C_GA:_C_END])

    mask_rows = _select_blocks(gates, t * n_sub, n_blocks, nb)
    pad = jnp.zeros((KX_HI - KX_OH - n_blocks, rt), BF16)
    tail = jnp.zeros((KX - KX_HI - HALF, rt), BF16)
    for h in range(H):
        qx_ref[0, h] = jnp.concatenate(
            [q_rot[h][:HALF], mask_rows[h * n_blocks:(h + 1) * n_blocks], pad,
             q_rot[h][HALF:], tail], axis=0)

    gq_ref[0] = (pg[:, _C_GQ - _C_GQ:_C_GK - _C_GQ] * (GLA_HEAD_K ** -0.5)).astype(BF16)
    gk_ref[0] = pg[:, _C_GK - _C_GQ:_C_GV - _C_GQ].astype(BF16)
    gv_ref[0] = pg[:, _C_GV - _C_GQ:_C_GG - _C_GQ].astype(BF16)
    sgg_ref[0] = _silu(pg[:, _C_GG - _C_GQ:_C_GA - _C_GQ]).astype(BF16)

    z = _dot_tn(pt[_R_FG:_R_END].astype(BF16), wfg2_ref[...]) + bfg_ref[...]
    log_sig = jnp.minimum(z, 0.0) - jnp.log1p(jnp.exp(-jnp.abs(z)))
    la_ref[0] = log_sig * (1.0 / GLA_GATE_NORM)

    sga_ref[0] = _sigmoid(pm[:, :D_MODEL] + bm_ref[0:1, :]).astype(BF16)
    sgb_ref[0] = _sigmoid(pm[:, D_MODEL:] + bm_ref[1:2, :]).astype(BF16)


def _in_proj(x, g, wn, wt, cq, sq, ck, sk, wfg2, bfg, bm):
    B, S, D = x.shape
    nt = S // ROW_TILE
    rt = IN_STEP_ROWS
    n_sub = rt // ROW_TILE
    const = lambda shape: pl.BlockSpec(shape, lambda b, t: (0,) * len(shape))
    t_blocked = lambda r: pl.BlockSpec((1, n_sub, r, ROW_TILE), lambda b, t: (b, t, 0, 0))
    rows = lambda w: pl.BlockSpec((1, rt, w), lambda b, t: (b, t, 0))
    t_shape = lambda r: jax.ShapeDtypeStruct((B, nt, r, ROW_TILE), BF16)
    rshape = lambda w, dt=BF16: jax.ShapeDtypeStruct((B, S, w), dt)
    return pl.pallas_call(
        functools.partial(_in_proj_kernel, n_blocks=nt),
        grid=(B, S // rt),
        in_specs=[
            rows(D), const((1, D)), const(wn.shape), const(wt.shape),
            pl.BlockSpec((MOBA_HEAD_DIM, rt), lambda b, t: (0, t)),
            pl.BlockSpec((MOBA_HEAD_DIM, rt), lambda b, t: (0, t)),
            pl.BlockSpec((rt, KX), lambda b, t: (t, 0)),
            pl.BlockSpec((rt, KX), lambda b, t: (t, 0)),
            const(wfg2.shape), const((1, GLA_DK)), const((2, D)),
        ],
        out_specs=[
            pl.BlockSpec((1, MOBA_HEADS, KX, rt), lambda b, t: (b, 0, 0, t)),
            t_blocked(MOBA_HEADS * VX), t_blocked(MOBA_WIDTH),
            rows(MOBA_HEADS * KX),
            rows(GLA_DK), rows(GLA_DK), rows(GLA_DV), rows(GLA_DV), rows(GLA_DK),
            rows(D), rows(D),
        ],
        out_shape=[
            jax.ShapeDtypeStruct((B, MOBA_HEADS, KX, S), BF16),
            t_shape(MOBA_HEADS * VX), t_shape(MOBA_WIDTH),
            rshape(MOBA_HEADS * KX),
            rshape(GLA_DK), rshape(GLA_DK), rshape(GLA_DV), rshape(GLA_DV),
            rshape(GLA_DK, F32), rshape(D), rshape(D),
        ],
        scratch_shapes=[pltpu.VMEM((nt * MOBA_HEADS, MOBA_WIDTH), F32)],
        compiler_params=pltpu.CompilerParams(
            dimension_semantics=("arbitrary", "arbitrary"),
            vmem_limit_bytes=VMEM_LIMIT),
        name="in_proj",
    )(x, g, wn, wt, cq, sq, ck, sk, wfg2, bfg, bm)


def _moba_kernel(qx_ref, k_ref, vt_ref, o_ref, s_ref, smax_ref, m_ref, acc_ref):
    nb = MOBA_BLOCK
    hd = MOBA_HEAD_DIM
    n_sub = o_ref.shape[1]
    first_own = pl.program_id(1) * n_sub

    def produce_scores(n, h, col0=0):
        kb = k_ref[0, pl.ds(pl.multiple_of(n * nb, nb), nb), h * KX:(h + 1) * KX]
        s = _dot(kb, qx_ref[0, h, :, col0:])
        s_ref[h, :, col0:] = s
        smax_ref[h, :, col0:] = jnp.max(s, axis=0, keepdims=True)

    causal = (lax.broadcasted_iota(jnp.int32, (nb, nb), 0)
              <= lax.broadcasted_iota(jnp.int32, (nb, nb), 1))

    def attend(blocks, diagonal, next_block):
        assert QK_LEAD <= MOBA_HEADS - QK_AHEAD
        col0 = [0 if d is None else d * nb for d in diagonal]
        jobs = [(blocks[0], h, col0[0]) for h in range(QK_AHEAD, MOBA_HEADS)]
        jobs += [(n, h, c0) for n, c0 in zip(blocks[1:], col0[1:]) for h in range(MOBA_HEADS)]
        if next_block is not None:
            jobs += [(next_block, h, 0) for h in range(QK_AHEAD)]

        def issue():
            if jobs:
                produce_scores(*jobs.pop(0))

        for _ in range(QK_LEAD):
            issue()
        for n, diag, c0 in zip(blocks, diagonal, col0):
            for h in range(MOBA_HEADS):
                s = s_ref[h, :, c0:]
                smax = smax_ref[h, :, c0:]
                if diag is not None:
                    tri = jnp.where(causal, s[:, :nb], NEG_INF)
                    tri_max = jnp.max(tri, axis=0, keepdims=True)
                    if s.shape[1] > nb:
                        s = jnp.concatenate([tri, s[:, nb:]], axis=1)
                        smax = jnp.concatenate([tri_max, smax[:, nb:]], axis=1)
                    else:
                        s, smax = tri, tri_max
                vt = vt_ref[0, n, h * VX:(h + 1) * VX, :]
                m_old = m_ref[h, :, c0:]
                m_new = jnp.maximum(m_old, smax)
                m_ref[h, :, c0:] = m_new
                acc_ref[h, :, c0:] = (jnp.exp2(m_old - m_new) * acc_ref[h, :, c0:]
                                      + _dot(vt, jnp.exp2(s - m_new).astype(BF16)))
                issue()

    m_ref[...] = jnp.full(m_ref.shape, NEG_INF, F32)
    acc_ref[...] = jnp.zeros_like(acc_ref)
    for h in range(QK_AHEAD):
        produce_scores(0, h)

    assert n_sub % LOOP_BLOCKS == 0

    def body(i, c):
        n = i * LOOP_BLOCKS
        attend([n + d for d in range(LOOP_BLOCKS)], [None] * LOOP_BLOCKS, n + LOOP_BLOCKS)
        return c

    lax.fori_loop(0, first_own // LOOP_BLOCKS, body, 0)
    attend([first_own + j for j in range(n_sub)], list(range(n_sub)), None)
    for h in range(MOBA_HEADS):
        o = acc_ref[h, 0:hd, :] / acc_ref[h, hd:hd + 1, :]
        for j in range(n_sub):
            o_ref[0, j, h * hd:(h + 1) * hd, :] = o[:, j * nb:(j + 1) * nb].astype(BF16)


def _moba(qx, k, vt):
    B, nt, rt, W = vt.shape[0], vt.shape[1], vt.shape[3], MOBA_WIDTH
    S = k.shape[1]
    n_sub = MOBA_Q_TILE // rt
    qw = MOBA_Q_TILE
    tile = pl.BlockSpec((1, n_sub, W, rt), lambda b, t: (b, t, 0, 0))
    return pl.pallas_call(
        _moba_kernel,
        grid=(B, nt // n_sub),
        in_specs=[
            pl.BlockSpec((1, MOBA_HEADS, KX, qw), lambda b, t: (b, 0, 0, t)),
            pl.BlockSpec((1, S, MOBA_HEADS * KX), lambda b, t: (b, 0, 0)),
            pl.BlockSpec((1, nt, MOBA_HEADS * VX, rt), lambda b, t: (b, 0, 0, 0)),
        ],
        out_specs=tile,
        out_shape=jax.ShapeDtypeStruct((B, nt, W, rt), BF16),
        scratch_shapes=[
            pltpu.VMEM((MOBA_HEADS, MOBA_BLOCK, qw), F32),
            pltpu.VMEM((MOBA_HEADS, 1, qw), F32),
            pltpu.VMEM((MOBA_HEADS, 1, qw), F32),
            pltpu.VMEM((MOBA_HEADS, VX, qw), F32),
        ],
        compiler_params=pltpu.CompilerParams(
            dimension_semantics=("arbitrary", "arbitrary"),
            vmem_limit_bytes=VMEM_LIMIT),
        name="moba",
    )(qx, k, vt)


def _gla_out_kernel(q_ref, k_ref, v_ref, la_ref, sgg_ref, gn_ref,
                    oat_ref, smgt_ref, sga_ref, sgb_ref, x_ref, wpa_ref, wpb_ref, wo_ref, gf_ref,
                    y_ref, st_ref, *, final_norm):
    C = GLA_CHUNK
    hk, hv = GLA_HEAD_K, GLA_HEAD_V
    n_sub, rt = oat_ref.shape[1], oat_ref.shape[3]
    sub = lambda j: slice(j * rt, (j + 1) * rt)

    @pl.when(pl.program_id(1) == 0)
    def _():
        st_ref[...] = jnp.zeros_like(st_ref)

    nc = TAIL_STEP_ROWS // C
    H = GLA_HEADS
    tril_b = (lax.broadcasted_iota(jnp.int32, (C, C), 0)
              >= lax.broadcasted_iota(jnp.int32, (C, C), 1)).astype(BF16)
    tril_stack = (lax.broadcasted_iota(jnp.int32, (H * C, C), 0) % C
                  >= lax.broadcasted_iota(jnp.int32, (H * C, C), 1))
    lane_head = lax.broadcasted_iota(jnp.int32, (C, GLA_DK), 1) // hk
    zero_b = jnp.zeros((C, GLA_DK), BF16)
    gn = gn_ref[...]

    def stack_heads(x):
        return jnp.concatenate([jnp.where(lane_head == h, x, zero_b) for h in range(H)], axis=0)

    chunk = lambda c: slice(c * C, (c + 1) * C)
    g = la_ref[0]
    g_hi = g.astype(BF16)
    g_lo = (g - g_hi.astype(F32)).astype(BF16)
    b = [_dot(tril_b, g_hi[chunk(c)]) + _dot(tril_b, g_lo[chunk(c)]) for c in range(nc)]

    ya = [_dot_tn((oat_ref[0, j].astype(F32) * smgt_ref[0, j].astype(F32)).astype(BF16),
                  wpa_ref[...]) for j in range(n_sub)]

    qe_s, qb_s, ke_b, kl_s, decay = [], [], [], [], []
    for c in range(nc):
        b_mid = b[c][C // 2 - 1:C // 2]
        b_last = b[c][C - 1:C]
        qe = q_ref[0, chunk(c), :].astype(F32) * jnp.exp(b[c] - b_mid)
        ke = k_ref[0, chunk(c), :].astype(F32) * jnp.exp(b_mid - b[c])
        qe_s.append(stack_heads(qe.astype(BF16)))
        qb_s.append(stack_heads((qe * jnp.exp(b_mid)).astype(BF16)))
        kl_s.append(stack_heads((ke * jnp.exp(b_last - b_mid)).astype(BF16)))
        ke_b.append(ke.astype(BF16))
        decay.append(jnp.exp(b_last))

    attn = [jnp.where(tril_stack, _dot_nt(qe_s[c], ke_b[c]), 0.0).astype(BF16) for c in range(nc)]
    intra = [jnp.concatenate(
        [_dot(attn[c][h * C:(h + 1) * C], v_ref[0, chunk(c), h * hv:(h + 1) * hv])
         for h in range(H)], axis=1) for c in range(nc)]

    v_s = [jnp.concatenate([v_ref[0, chunk(c), h * hv:(h + 1) * hv] for h in range(H)], axis=0)
           for c in range(nc)]
    upd = [_dot_tn(v_s[c], kl_s[c]) for c in range(nc)]

    st = st_ref[...]
    inter = []
    for c in range(nc):
        o_s = _dot_nt(qb_s[c], st.astype(BF16))
        inter.append(jnp.concatenate([o_s[h * C:(h + 1) * C] for h in range(H)], axis=1))
        st = st * decay[c] + upd[c]
    st_ref[...] = st

    ob = []
    for c in range(nc):
        o = inter[c] + intra[c]
        heads = []
        for h in range(H):
            oh = o[:, h * hv:(h + 1) * hv]
            ms = jnp.mean(oh * oh, axis=-1, keepdims=True)
            y = oh * lax.rsqrt(ms + RMS_EPS) * gn
            gate = sgg_ref[0, chunk(c), h * hv:(h + 1) * hv].astype(F32)
            heads.append((y * gate).astype(BF16))
        ob.append(jnp.concatenate(heads, axis=1))

    per_sub = rt // C
    yb = [_dot(jnp.concatenate(ob[j * per_sub:(j + 1) * per_sub], axis=0), wpb_ref[...])
          for j in range(n_sub)]
    merged = [(sga_ref[0, sub(j), :].astype(F32) * ya[j]
               + sgb_ref[0, sub(j), :].astype(F32) * yb[j]).astype(BF16) for j in range(n_sub)]
    proj = [_dot(merged[j], wo_ref[...]) for j in range(n_sub)]
    for j in range(n_sub):
        r = x_ref[0, sub(j), :] + proj[j]
        if final_norm:
            ms = jnp.mean(r * r, axis=-1, keepdims=True)
            r = r * lax.rsqrt(ms + RMS_EPS) * gf_ref[...]
        y_ref[0, sub(j), :] = r


def _gla_out(gq, gk, gv, la, sgg, gn, oat, smgt, sga, sgb, x, wpa, wpb, wo, gf, final_norm):
    B, S, D = x.shape
    rt = TAIL_STEP_ROWS
    n_sub = rt // ROW_TILE
    const = lambda shape: pl.BlockSpec(shape, lambda b, t: (0,) * len(shape))
    rows = lambda w: pl.BlockSpec((1, rt, w), lambda b, t: (b, t, 0))
    return pl.pallas_call(
        functools.partial(_gla_out_kernel, final_norm=final_norm),
        grid=(B, S // rt),
        in_specs=[
            rows(GLA_DK), rows(GLA_DK), rows(GLA_DV), rows(GLA_DK), rows(GLA_DV),
            const((1, GLA_HEAD_V)),
            pl.BlockSpec((1, n_sub, MOBA_WIDTH, ROW_TILE), lambda b, t: (b, t, 0, 0)),
            pl.BlockSpec((1, n_sub, MOBA_WIDTH, ROW_TILE), lambda b, t: (b, t, 0, 0)),
            rows(D), rows(D), rows(D),
            const(wpa.shape), const(wpb.shape), const(wo.shape), const((1, D)),
        ],
        out_specs=rows(D),
        out_shape=jax.ShapeDtypeStruct((B, S, D), F32),
        scratch_shapes=[pltpu.VMEM((GLA_HEAD_V, GLA_DK), F32)],
        compiler_params=pltpu.CompilerParams(
            dimension_semantics=("arbitrary", "arbitrary"),
            vmem_limit_bytes=VMEM_LIMIT),
        name="gla_out",
    )(gq, gk, gv, la, sgg, gn, oat, smgt, sga, sgb, x, wpa, wpb, wo, gf)


def _rope_tables(S):
    inv_freq = 1.0 / (ROPE_THETA ** (jnp.arange(HALF, dtype=F32) / HALF))
    ang = jnp.arange(S, dtype=F32)[:, None] * inv_freq[None, :]
    return jnp.cos(ang), jnp.sin(ang)


def _swap_half_head(a):
    lead = a.shape[:-1]
    a = a.reshape(*lead, N_PAIRS, 2, 2, HALF)
    return jnp.swapaxes(a, -2, -3).reshape(*lead, MOBA_WIDTH)


def kernel(x, norm_in_g, w_in, b_merge, w_gla_fg2, b_gla_fg, gla_norm_g,
           w_proj_a, w_proj_b, w_out, norm_f_g):
    B, S, D = x.shape
    depth = w_in.shape[0]
    nt = S // MOBA_BLOCK
    cos, sin = _rope_tables(S)
    q_scale = MOBA_HEAD_DIM ** -0.5 * math.log2(math.e)
    cq = (jnp.concatenate([cos, cos], axis=1) * q_scale).T
    sq = (jnp.concatenate([-sin, sin], axis=1) * q_scale).T
    ck = jnp.tile(cos, (1, KX // HALF))
    sk = jnp.tile(sin, (1, KX // HALF)) * jnp.where(jnp.arange(KX) < KX_HI, -1.0, 1.0)[None, :]

    for layer in range(depth):
        w = w_in[layer]
        o = np.cumsum([0, MOBA_WIDTH, MOBA_WIDTH, MOBA_WIDTH, MOBA_WIDTH, GLA_DK, GLA_DK,
                       GLA_DV, GLA_DV, GLA_GATE_RANK, D_MODEL, D_MODEL])
        wq, wk, wv, wmg, wgq, wgk, wgv, wgg, wfg, wga, wgb = [
            w[:, o[i]:o[i + 1]] for i in range(11)]
        wn = jnp.concatenate([_swap_half_head(wk), wgq, wgk, wgv, wgg, wga, wgb],
                             axis=1).astype(BF16)
        wt = jnp.concatenate([wq, wv, wmg, wfg], axis=1).astype(BF16).T
        wfg2 = w_gla_fg2[layer].astype(BF16)

        (qx, vt, smgt, k, gq, gk, gv, sgg, la, sga, sgb) = _in_proj(
            x, norm_in_g[layer][None, :], wn, wt, cq, sq, ck, sk, wfg2,
            b_gla_fg[layer][None, :], b_merge[layer])
        oat = _moba(qx, k, vt)
        x = _gla_out(gq, gk, gv, la, sgg, gla_norm_g[layer][None, :], oat, smgt, sga, sgb, x,
                     w_proj_a[layer].astype(BF16), w_proj_b[layer].astype(BF16),
                     w_out[layer].astype(BF16), norm_f_g[None, :],
                     final_norm=(layer == depth - 1))
    return x
```

```python
import functools
import math

import jax
import jax.numpy as jnp
import numpy as np
from jax import lax
from jax.experimental import pallas as pl
from jax.experimental.pallas import tpu as pltpu

F32 = jnp.float32
BF16 = jnp.bfloat16

D_MODEL = 1024
MOBA_HEADS = 8
MOBA_HEAD_DIM = 64
MOBA_WIDTH = MOBA_HEADS * MOBA_HEAD_DIM
MOBA_BLOCK = 256
MOBA_TOPK = 3
ROPE_THETA = 10000.0
GLA_HEADS = 4
GLA_DK = 256
GLA_DV = 512
GLA_HEAD_K = 64
GLA_HEAD_V = 128
GLA_GATE_RANK = 16
GLA_GATE_NORM = 16.0
GLA_CHUNK = 64
RMS_EPS = 1e-6
NEG_INF = -1e30

LANES = 128
BF16_ROWS = 16
ROW_TILE = MOBA_BLOCK
PAIR = 2 * MOBA_HEAD_DIM
N_PAIRS = MOBA_HEADS // 2
TAIL_STEP_ROWS = 1024
IN_STEP_ROWS = 512
QK_AHEAD = 4
QK_LEAD = 1
MOBA_Q_TILE = 512
LOOP_BLOCKS = 2
VMEM_LIMIT = 56 * 1024 * 1024

HALF = MOBA_HEAD_DIM // 2
KX = LANES
KX_LO, KX_OH, KX_HI = 0, HALF, 2 * HALF
N_BLOCKS_MAX = BF16_ROWS
VX = MOBA_HEAD_DIM + BF16_ROWS

_C_K = 0
_C_GQ = _C_K + MOBA_WIDTH
_C_GK = _C_GQ + GLA_DK
_C_GV = _C_GK + GLA_DK
_C_GG = _C_GV + GLA_DV
_C_GA = _C_GG + GLA_DV
_C_GB = _C_GA + D_MODEL
_C_END = _C_GB + D_MODEL
_R_Q = 0
_R_V = _R_Q + MOBA_WIDTH
_R_MG = _R_V + MOBA_WIDTH
_R_FG = _R_MG + MOBA_WIDTH
_R_END = _R_FG + GLA_GATE_RANK


def _dot(a, b):
    return jnp.dot(a, b, preferred_element_type=F32)


def _dot_nt(a, b):
    return lax.dot_general(a, b, (((1,), (1,)), ((), ())), preferred_element_type=F32)


def _dot_tn(a, b):
    return lax.dot_general(a, b, (((0,), (0,)), ((), ())), preferred_element_type=F32)


def _sigmoid(x):
    return 1.0 / (1.0 + jnp.exp(-x))


def _silu(x):
    return x * _sigmoid(x)


def _select_blocks(gates, first_own, n_blocks, nb):
    H = MOBA_HEADS
    qw = gates.shape[1]
    col_own = (first_own + lax.broadcasted_iota(jnp.int32, (1, qw), 1) // nb).astype(F32)
    past = [col_own > n for n in range(n_blocks)]
    g = [jnp.where(past[n], gates[n * H:(n + 1) * H], NEG_INF) for n in range(n_blocks)]
    picked = [jnp.zeros((H, qw), jnp.bool_) for _ in range(n_blocks)]
    for _ in range(MOBA_TOPK):
        best = functools.reduce(jnp.maximum, g)
        first = functools.reduce(jnp.minimum, [
            jnp.where(g[n] == best, float(n), float(n_blocks)) for n in range(n_blocks)])
        for n in range(n_blocks):
            hit = first == float(n)
            picked[n] = picked[n] | hit
            g[n] = jnp.where(hit, -jnp.inf, g[n])
    keep = jnp.concatenate(
        [jnp.where((col_own == n) | (past[n] & picked[n]), 1.0, 0.0) for n in range(n_blocks)],
        axis=0).astype(BF16)
    r_out = lax.broadcasted_iota(jnp.int32, (H * n_blocks, n_blocks * H), 0)
    r_in = lax.broadcasted_iota(jnp.int32, (H * n_blocks, n_blocks * H), 1)
    perm = ((r_out // n_blocks == r_in % H) & (r_out % n_blocks == r_in // H)).astype(BF16)
    return jnp.where(_dot(perm, keep) > 0.5, 0.0, NEG_INF).astype(BF16)


def _in_proj_kernel(x_ref, g_ref, wn_ref, wt_ref, cq_ref, sq_ref, ck_ref, sk_ref,
                    wfg2_ref, bfg_ref, bm_ref,
                    qx_ref, vt_ref, smgt_ref, k_ref, gq_ref, gk_ref, gv_ref,
                    sgg_ref, la_ref, sga_ref, sgb_ref, km_ref, *, n_blocks):
    t = pl.program_id(1)
    rt = x_ref.shape[1]
    nb = MOBA_BLOCK
    n_sub = rt // nb
    hd = MOBA_HEAD_DIM
    H = MOBA_HEADS
    assert n_blocks <= N_BLOCKS_MAX
    blk_cols = lambda j: slice(j * nb, (j + 1) * nb)

    @pl.when(t == 0)
    def _():
        km_ref[...] = jnp.zeros_like(km_ref)

    x = x_ref[0]
    ms = jnp.mean(x * x, axis=-1, keepdims=True)
    hb = (x * lax.rsqrt(ms + RMS_EPS) * g_ref[...]).astype(BF16)

    pt = _dot_nt(wt_ref[...], hb)
    cq = cq_ref[...]
    sq = sq_ref[...]
    ones_rows = jnp.where(lax.broadcasted_iota(jnp.int32, (BF16_ROWS, nb), 0) == 0,
                          1.0, 0.0).astype(BF16)
    q_rot = []
    for h in range(H):
        blk = pt[_R_Q + h * hd:_R_Q + (h + 1) * hd]
        swapped = jnp.concatenate([blk[HALF:], blk[:HALF]], axis=0)
        q_rot.append((blk * cq + swapped * sq).astype(BF16))
        v_rows = pt[_R_V + h * hd:_R_V + (h + 1) * hd].astype(BF16)
        for j in range(n_sub):
            vt_ref[0, j, h * VX:h * VX + hd, :] = v_rows[:, blk_cols(j)]
            vt_ref[0, j, h * VX + hd:(h + 1) * VX, :] = ones_rows

    smg = _silu(pt[_R_MG:_R_FG]).astype(BF16)
    for j in range(n_sub):
        smgt_ref[0, j] = smg[:, blk_cols(j)]

    pk = _dot(hb, wn_ref[:, _C_K:_C_GQ])
    pg = _dot(hb, wn_ref[:, _C_GQ:_C_GA])
    ck = ck_ref[...]
    sk = sk_ref[...]
    lane = lax.broadcasted_iota(jnp.int32, (rt, KX), 1)
    row_block = t * n_sub + lax.broadcasted_iota(jnp.int32, (rt, KX), 0) // nb
    block_onehot = jnp.where(lane == KX_OH + row_block, 1.0, 0.0)
    own_lanes = (lane % (2 * HALF)) < HALF
    k_means = [[] for _ in range(n_sub)]
    for p in range(N_PAIRS):
        blk = pk[:, p * PAIR:(p + 1) * PAIR]
        kr = blk * ck + pltpu.roll(blk, 2 * HALF, axis=1) * sk
        for j in range(n_sub):
            k_means[j].append(jnp.mean(kr[blk_cols(j)], axis=0, keepdims=True))
        for hh, src in ((0, kr), (1, pltpu.roll(kr, KX - HALF, axis=1))):
            h = 2 * p + hh
            k_ref[0, :, h * KX:(h + 1) * KX] = (
                jnp.where(own_lanes, src, 0.0) + block_onehot).astype(BF16)
    km_lane = lax.broadcasted_iota(jnp.int32, (H, MOBA_WIDTH), 1)
    km_head = 2 * (km_lane // PAIR) + (km_lane // HALF) % 2
    head_lanes = km_head == lax.broadcasted_iota(jnp.int32, (H, MOBA_WIDTH), 0)
    for j in range(n_sub):
        mean_row = jnp.concatenate(k_means[j], axis=1)
        km_ref[pl.ds(pl.multiple_of((t * n_sub + j) * H, H), H), :] = jnp.where(
            head_lanes, mean_row, 0.0)

    q_pair = jnp.concatenate(
        [q_rot[2 * p + hh][part * HALF:(part + 1) * HALF]
         for p in range(N_PAIRS) for part in range(2) for hh in range(2)], axis=0)
    km = km_ref[...]
    km_hi = km.astype(BF16)
    km_lo = (km - km_hi.astype(F32)).astype(BF16)
    gates = _dot(km_hi, q_pair) + _dot(km_lo, q_pair)

    pm = _dot(hb, wn_ref[:, _C_GA:_C_END])

    mask_rows = _select_blocks(gates, t * n_sub, n_blocks, nb)
    pad = jnp.zeros((KX_HI - KX_OH - n_blocks, rt), BF16)
    tail = jnp.zeros((KX - KX_HI - HALF, rt), BF16)
    for h in range(H):
        qx_ref[0, h] = jnp.concatenate(
            [q_rot[h][:HALF], mask_rows[h * n_blocks:(h + 1) * n_blocks], pad,
             q_rot[h][HALF:], tail], axis=0)

    gq_ref[0] = (pg[:, _C_GQ - _C_GQ:_C_GK - _C_GQ] * (GLA_HEAD_K ** -0.5)).astype(BF16)
    gk_ref[0] = pg[:, _C_GK - _C_GQ:_C_GV - _C_GQ].astype(BF16)
    gv_ref[0] = pg[:, _C_GV - _C_GQ:_C_GG - _C_GQ].astype(BF16)
    sgg_ref[0] = _silu(pg[:, _C_GG - _C_GQ:_C_GA - _C_GQ]).astype(BF16)

    z = _dot_tn(pt[_R_FG:_R_END].astype(BF16), wfg2_ref[...]) + bfg_ref[...]
    log_sig = jnp.minimum(z, 0.0) - jnp.log1p(jnp.exp(-jnp.abs(z)))
    la_ref[0] = log_sig * (1.0 / GLA_GATE_NORM)

    sga_ref[0] = _sigmoid(pm[:, :D_MODEL] + bm_ref[0:1, :]).astype(BF16)
    sgb_ref[0] = _sigmoid(pm[:, D_MODEL:] + bm_ref[1:2, :]).astype(BF16)


def _in_proj(x, g, wn, wt, cq, sq, ck, sk, wfg2, bfg, bm):
    B, S, D = x.shape
    nt = S // ROW_TILE
    rt = IN_STEP_ROWS
    n_sub = rt // ROW_TILE
    const = lambda shape: pl.BlockSpec(shape, lambda b, t: (0,) * len(shape))
    t_blocked = lambda r: pl.BlockSpec((1, n_sub, r, ROW_TILE), lambda b, t: (b, t, 0, 0))
    rows = lambda w: pl.BlockSpec((1, rt, w), lambda b, t: (b, t, 0))
    t_shape = lambda r: jax.ShapeDtypeStruct((B, nt, r, ROW_TILE), BF16)
    rshape = lambda w, dt=BF16: jax.ShapeDtypeStruct((B, S, w), dt)
    return pl.pallas_call(
        functools.partial(_in_proj_kernel, n_blocks=nt),
        grid=(B, S // rt),
        in_specs=[
            rows(D), const((1, D)), const(wn.shape), const(wt.shape),
            pl.BlockSpec((MOBA_HEAD_DIM, rt), lambda b, t: (0, t)),
            pl.BlockSpec((MOBA_HEAD_DIM, rt), lambda b, t: (0, t)),
            pl.BlockSpec((rt, KX), lambda b, t: (t, 0)),
            pl.BlockSpec((rt, KX), lambda b, t: (t, 0)),
            const(wfg2.shape), const((1, GLA_DK)), const((2, D)),
        ],
        out_specs=[
            pl.BlockSpec((1, MOBA_HEADS, KX, rt), lambda b, t: (b, 0, 0, t)),
            t_blocked(MOBA_HEADS * VX), t_blocked(MOBA_WIDTH),
            rows(MOBA_HEADS * KX),
            rows(GLA_DK), rows(GLA_DK), rows(GLA_DV), rows(GLA_DV), rows(GLA_DK),
            rows(D), rows(D),
        ],
        out_shape=[
            jax.ShapeDtypeStruct((B, MOBA_HEADS, KX, S), BF16),
            t_shape(MOBA_HEADS * VX), t_shape(MOBA_WIDTH),
            rshape(MOBA_HEADS * KX),
            rshape(GLA_DK), rshape(GLA_DK), rshape(GLA_DV), rshape(GLA_DV),
            rshape(GLA_DK, F32), rshape(D), rshape(D),
        ],
        scratch_shapes=[pltpu.VMEM((nt * MOBA_HEADS, MOBA_WIDTH), F32)],
        compiler_params=pltpu.CompilerParams(
            dimension_semantics=("arbitrary", "arbitrary"),
            vmem_limit_bytes=VMEM_LIMIT),
        name="in_proj",
    )(x, g, wn, wt, cq, sq, ck, sk, wfg2, bfg, bm)


def _moba_kernel(qx_ref, k_ref, vt_ref, o_ref, s_ref, smax_ref, m_ref, acc_ref):
    nb = MOBA_BLOCK
    hd = MOBA_HEAD_DIM
    n_sub = o_ref.shape[1]
    first_own = pl.program_id(1) * n_sub

    def produce_scores(n, h, col0=0):
        kb = k_ref[0, pl.ds(pl.multiple_of(n * nb, nb), nb), h * KX:(h + 1) * KX]
        s = _dot(kb, qx_ref[0, h, :, col0:])
        s_ref[h, :, col0:] = s
        smax_ref[h, :, col0:] = jnp.max(s, axis=0, keepdims=True)

    causal = (lax.broadcasted_iota(jnp.int32, (nb, nb), 0)
              <= lax.broadcasted_iota(jnp.int32, (nb, nb), 1))

    def attend(blocks, diagonal, next_block):
        assert QK_LEAD <= MOBA_HEADS - QK_AHEAD
        col0 = [0 if d is None else d * nb for d in diagonal]
        jobs = [(blocks[0], h, col0[0]) for h in range(QK_AHEAD, MOBA_HEADS)]
        jobs += [(n, h, c0) for n, c0 in zip(blocks[1:], col0[1:]) for h in range(MOBA_HEADS)]
        if next_block is not None:
            jobs += [(next_block, h, 0) for h in range(QK_AHEAD)]

        def issue():
            if jobs:
                produce_scores(*jobs.pop(0))

        for _ in range(QK_LEAD):
            issue()
        for n, diag, c0 in zip(blocks, diagonal, col0):
            for h in range(MOBA_HEADS):
                s = s_ref[h, :, c0:]
                smax = smax_ref[h, :, c0:]
                if diag is not None:
                    tri = jnp.where(causal, s[:, :nb], NEG_INF)
                    tri_max = jnp.max(tri, axis=0, keepdims=True)
                    if s.shape[1] > nb:
                        s = jnp.concatenate([tri, s[:, nb:]], axis=1)
                        smax = jnp.concatenate([tri_max, smax[:, nb:]], axis=1)
                    else:
                        s, smax = tri, tri_max
                vt = vt_ref[0, n, h * VX:(h + 1) * VX, :]
                m_old = m_ref[h, :, c0:]
                m_new = jnp.maximum(m_old, smax)
                m_ref[h, :, c0:] = m_new
                acc_ref[h, :, c0:] = (jnp.exp2(m_old - m_new) * acc_ref[h, :, c0:]
                                      + _dot(vt, jnp.exp2(s - m_new).astype(BF16)))
                issue()

    m_ref[...] = jnp.full(m_ref.shape, NEG_INF, F32)
    acc_ref[...] = jnp.zeros_like(acc_ref)
    for h in range(QK_AHEAD):
        produce_scores(0, h)

    assert n_sub % LOOP_BLOCKS == 0

    def body(i, c):
        n = i * LOOP_BLOCKS
        attend([n + d for d in range(LOOP_BLOCKS)], [None] * LOOP_BLOCKS, n + LOOP_BLOCKS)
        return c

    lax.fori_loop(0, first_own // LOOP_BLOCKS, body, 0)
    attend([first_own + j for j in range(n_sub)], list(range(n_sub)), None)
    for h in range(MOBA_HEADS):
        o = acc_ref[h, 0:hd, :] / acc_ref[h, hd:hd + 1, :]
        for j in range(n_sub):
            o_ref[0, j, h * hd:(h + 1) * hd, :] = o[:, j * nb:(j + 1) * nb].astype(BF16)


def _moba(qx, k, vt):
    B, nt, rt, W = vt.shape[0], vt.shape[1], vt.shape[3], MOBA_WIDTH
    S = k.shape[1]
    n_sub = MOBA_Q_TILE // rt
    qw = MOBA_Q_TILE
    tile = pl.BlockSpec((1, n_sub, W, rt), lambda b, t: (b, t, 0, 0))
    return pl.pallas_call(
        _moba_kernel,
        grid=(B, nt // n_sub),
        in_specs=[
            pl.BlockSpec((1, MOBA_HEADS, KX, qw), lambda b, t: (b, 0, 0, t)),
            pl.BlockSpec((1, S, MOBA_HEADS * KX), lambda b, t: (b, 0, 0)),
            pl.BlockSpec((1, nt, MOBA_HEADS * VX, rt), lambda b, t: (b, 0, 0, 0)),
        ],
        out_specs=tile,
        out_shape=jax.ShapeDtypeStruct((B, nt, W, rt), BF16),
        scratch_shapes=[
            pltpu.VMEM((MOBA_HEADS, MOBA_BLOCK, qw), F32),
            pltpu.VMEM((MOBA_HEADS, 1, qw), F32),
            pltpu.VMEM((MOBA_HEADS, 1, qw), F32),
            pltpu.VMEM((MOBA_HEADS, VX, qw), F32),
        ],
        compiler_params=pltpu.CompilerParams(
            dimension_semantics=("arbitrary", "arbitrary"),
            vmem_limit_bytes=VMEM_LIMIT),
        name="moba",
    )(qx, k, vt)


def _gla_out_kernel(q_ref, k_ref, v_ref, la_ref, sgg_ref, gn_ref,
                    oat_ref, smgt_ref, sga_ref, sgb_ref, x_ref, wpa_ref, wpb_ref, wo_ref, gf_ref,
                    y_ref, st_ref, *, final_norm):
    C = GLA_CHUNK
    hk, hv = GLA_HEAD_K, GLA_HEAD_V
    n_sub, rt = oat_ref.shape[1], oat_ref.shape[3]
    sub = lambda j: slice(j * rt, (j + 1) * rt)

    @pl.when(pl.program_id(1) == 0)
    def _():
        st_ref[...] = jnp.zeros_like(st_ref)

    nc = TAIL_STEP_ROWS // C
    H = GLA_HEADS
    tril_b = (lax.broadcasted_iota(jnp.int32, (C, C), 0)
              >= lax.broadcasted_iota(jnp.int32, (C, C), 1)).astype(BF16)
    tril_stack = (lax.broadcasted_iota(jnp.int32, (H * C, C), 0) % C
                  >= lax.broadcasted_iota(jnp.int32, (H * C, C), 1))
    lane_head = lax.broadcasted_iota(jnp.int32, (C, GLA_DK), 1) // hk
    zero_b = jnp.zeros((C, GLA_DK), BF16)
    gn = gn_ref[...]

    def stack_heads(x):
        return jnp.concatenate([jnp.where(lane_head == h, x, zero_b) for h in range(H)], axis=0)

    chunk = lambda c: slice(c * C, (c + 1) * C)
    g = la_ref[0]
    g_hi = g.astype(BF16)
    g_lo = (g - g_hi.astype(F32)).astype(BF16)
    b = [_dot(tril_b, g_hi[chunk(c)]) + _dot(tril_b, g_lo[chunk(c)]) for c in range(nc)]

    ya = [_dot_tn(oat_ref[0, j] * smgt_ref[0, j], wpa_ref[...])
          for j in range(n_sub)]

    qe_s, qb_s, ke_b, kl_s, decay = [], [], [], [], []
    for c in range(nc):
        b_mid = b[c][C // 2 - 1:C // 2]
        b_last = b[c][C - 1:C]
        qe = q_ref[0, chunk(c), :].astype(F32) * jnp.exp(b[c] - b_mid)
        ke = k_ref[0, chunk(c), :].astype(F32) * jnp.exp(b_mid - b[c])
        qe_s.append(stack_heads(qe.astype(BF16)))
        qb_s.append(stack_heads((qe * jnp.exp(b_mid)).astype(BF16)))
        kl_s.append(stack_heads((ke * jnp.exp(b_last - b_mid)).astype(BF16)))
        ke_b.append(ke.astype(BF16))
        decay.append(jnp.exp(b_last))

    attn = [jnp.where(tril_stack, _dot_nt(qe_s[c], ke_b[c]), 0.0).astype(BF16) for c in range(nc)]
    intra = [jnp.concatenate(
        [_dot(attn[c][h * C:(h + 1) * C], v_ref[0, chunk(c), h * hv:(h + 1) * hv])
         for h in range(H)], axis=1) for c in range(nc)]

    v_s = [jnp.concatenate([v_ref[0, chunk(c), h * hv:(h + 1) * hv] for h in range(H)], axis=0)
           for c in range(nc)]
    upd = [_dot_tn(v_s[c], kl_s[c]) for c in range(nc)]

    st = st_ref[...]
    inter = []
    for c in range(nc):
        o_s = _dot_nt(qb_s[c], st.astype(BF16))
        inter.append(jnp.concatenate([o_s[h * C:(h + 1) * C] for h in range(H)], axis=1))
        st = st * decay[c] + upd[c]
    st_ref[...] = st

    ob = []
    for c in range(nc):
        o = inter[c] + intra[c]
        heads = []
        for h in range(H):
            oh = o[:, h * hv:(h + 1) * hv]
            ms = jnp.mean(oh * oh, axis=-1, keepdims=True)
            y = oh * lax.rsqrt(ms + RMS_EPS) * gn
            gate = sgg_ref[0, chunk(c), h * hv:(h + 1) * hv].astype(F32)
            heads.append((y * gate).astype(BF16))
        ob.append(jnp.concatenate(heads, axis=1))

    per_sub = rt // C
    yb = [_dot(jnp.concatenate(ob[j * per_sub:(j + 1) * per_sub], axis=0), wpb_ref[...])
          for j in range(n_sub)]
    merged = [(sga_ref[0, sub(j), :].astype(F32) * ya[j]
               + sgb_ref[0, sub(j), :].astype(F32) * yb[j]).astype(BF16) for j in range(n_sub)]
    proj = [_dot(merged[j], wo_ref[...]) for j in range(n_sub)]
    for j in range(n_sub):
        r = x_ref[0, sub(j), :] + proj[j]
        if final_norm:
            ms = jnp.mean(r * r, axis=-1, keepdims=True)
            r = r * lax.rsqrt(ms + RMS_EPS) * gf_ref[...]
        y_ref[0, sub(j), :] = r


def _gla_out(gq, gk, gv, la, sgg, gn, oat, smgt, sga, sgb, x, wpa, wpb, wo, gf, final_norm):
    B, S, D = x.shape
    rt = TAIL_STEP_ROWS
    n_sub = rt // ROW_TILE
    const = lambda shape: pl.BlockSpec(shape, lambda b, t: (0,) * len(shape))
    rows = lambda w: pl.BlockSpec((1, rt, w), lambda b, t: (b, t, 0))
    return pl.pallas_call(
        functools.partial(_gla_out_kernel, final_norm=final_norm),
        grid=(B, S // rt),
        in_specs=[
            rows(GLA_DK), rows(GLA_DK), rows(GLA_DV), rows(GLA_DK), rows(GLA_DV),
            const((1, GLA_HEAD_V)),
            pl.BlockSpec((1, n_sub, MOBA_WIDTH, ROW_TILE), lambda b, t: (b, t, 0, 0)),
            pl.BlockSpec((1, n_sub, MOBA_WIDTH, ROW_TILE), lambda b, t: (b, t, 0, 0)),
            rows(D), rows(D), rows(D),
            const(wpa.shape), const(wpb.shape), const(wo.shape), const((1, D)),
        ],
        out_specs=rows(D),
        out_shape=jax.ShapeDtypeStruct((B, S, D), F32),
        scratch_shapes=[pltpu.VMEM((GLA_HEAD_V, GLA_DK), F32)],
        compiler_params=pltpu.CompilerParams(
            dimension_semantics=("arbitrary", "arbitrary"),
            vmem_limit_bytes=VMEM_LIMIT),
        name="gla_out",
    )(gq, gk, gv, la, sgg, gn, oat, smgt, sga, sgb, x, wpa, wpb, wo, gf)


def _rope_tables(S):
    inv_freq = 1.0 / (ROPE_THETA ** (jnp.arange(HALF, dtype=F32) / HALF))
    ang = jnp.arange(S, dtype=F32)[:, None] * inv_freq[None, :]
    return jnp.cos(ang), jnp.sin(ang)


def _swap_half_head(a):
    lead = a.shape[:-1]
    a = a.reshape(*lead, N_PAIRS, 2, 2, HALF)
    return jnp.swapaxes(a, -2, -3).reshape(*lead, MOBA_WIDTH)


def kernel(x, norm_in_g, w_in, b_merge, w_gla_fg2, b_gla_fg, gla_norm_g,
           w_proj_a, w_proj_b, w_out, norm_f_g):
    B, S, D = x.shape
    depth = w_in.shape[0]
    nt = S // MOBA_BLOCK
    cos, sin = _rope_tables(S)
    q_scale = MOBA_HEAD_DIM ** -0.5 * math.log2(math.e)
    cq = (jnp.concatenate([cos, cos], axis=1) * q_scale).T
    sq = (jnp.concatenate([-sin, sin], axis=1) * q_scale).T
    ck = jnp.tile(cos, (1, KX // HALF))
    sk = jnp.tile(sin, (1, KX // HALF)) * jnp.where(jnp.arange(KX) < KX_HI, -1.0, 1.0)[None, :]

    for layer in range(depth):
        w = w_in[layer]
        o = np.cumsum([0, MOBA_WIDTH, MOBA_WIDTH, MOBA_WIDTH, MOBA_WIDTH, GLA_DK, GLA_DK,
                       GLA_DV, GLA_DV, GLA_GATE_RANK, D_MODEL, D_MODEL])
        wq, wk, wv, wmg, wgq, wgk, wgv, wgg, wfg, wga, wgb = [
            w[:, o[i]:o[i + 1]] for i in range(11)]
        wn = jnp.concatenate([_swap_half_head(wk), wgq, wgk, wgv, wgg, wga, wgb],
                             axis=1).astype(BF16)
        wt = jnp.concatenate([wq, wv, wmg, wfg], axis=1).astype(BF16).T
        wfg2 = w_gla_fg2[layer].astype(BF16)

        (qx, vt, smgt, k, gq, gk, gv, sgg, la, sga, sgb) = _in_proj(
            x, norm_in_g[layer][None, :], wn, wt, cq, sq, ck, sk, wfg2,
            b_gla_fg[layer][None, :], b_merge[layer])
        oat = _moba(qx, k, vt)
        x = _gla_out(gq, gk, gv, la, sgg, gla_norm_g[layer][None, :], oat, smgt, sga, sgb, x,
                     w_proj_a[layer].astype(BF16), w_proj_b[layer].astype(BF16),
                     w_out[layer].astype(BF16), norm_f_g[None, :],
                     final_norm=(layer == depth - 1))
    return x
```
